```python
import math
import jax
import jax.numpy as jnp
from jax import lax
import numpy as np

D_MODEL = 1024
BATCH = 16
SEQ = 2048
DEPTH = 4

GRID_W = 64
CTX_LEN = 256
HEAD_DIM = 64
EPS = 1e-6
NEG_INF = -1e30

S5_CH = 256
S5_GROUP_CH = 16
S5_GROUPS = S5_CH // S5_GROUP_CH
S5_STATE = 64
NA_HEADS = 4
NA_DIM = NA_HEADS * HEAD_DIM
NA_KH = 8
NA_KW = 16
NA_QB = NA_KW
GLA_HEADS = 4
GLA_DK = 64
GLA_DV = 64
GLA_QK = GLA_HEADS * GLA_DK
GLA_V = GLA_HEADS * GLA_DV
GLA_RANK = 16
GLA_TAU = 16.0
GLA_CHUNK = 64
SWA_HEADS = 4
SWA_KV_HEADS = 2
SWA_Q = SWA_HEADS * HEAD_DIM
SWA_KV = SWA_KV_HEADS * HEAD_DIM
SWA_WINDOW = 128
SWA_BLOCK = 128
ROPE_BASE = 10000.0
D_FF = 2816
FFN_CONV = 3

IN_SIZES = (S5_CH, NA_DIM, NA_DIM, NA_DIM, GLA_QK, GLA_QK, GLA_V, GLA_RANK, GLA_RANK, GLA_V, SWA_Q, SWA_KV, SWA_KV)
D_IN = sum(IN_SIZES)
D_MIX = S5_CH + NA_DIM + GLA_V + SWA_Q

kernel_name = "hybrid_parallel_heads_diffusion_trunk"


def rmsnorm(x, g):
    xf = x.astype(jnp.float32)
    y = xf * lax.rsqrt(jnp.mean(xf * xf, axis=-1, keepdims=True) + EPS)
    return (y * g.astype(jnp.float32)).astype(x.dtype)


def split_in(p):
    out, off = [], 0
    for s in IN_SIZES:
        out.append(p[..., off:off + s])
        off += s
    return out


def joint_softmax(*logits):
    sizes = [t.shape[-1] for t in logits]
    p = jax.nn.softmax(jnp.concatenate([t.astype(jnp.float32) for t in logits], axis=-1), axis=-1)
    out, off = [], 0
    for s in sizes:
        out.append(p[..., off:off + s])
        off += s
    return out


def _linear_combine(e1, e2):
    a1, b1 = e1
    a2, b2 = e2
    return a2 * a1, a2 * b1 + b2


def s5_discretize(lam_re, lam_im, b_re, b_im, log_step):
    lam = lax.complex(lam_re.astype(jnp.float32), lam_im.astype(jnp.float32))
    step = jnp.exp(log_step.astype(jnp.float32))[:, None]
    lam_bar = jnp.exp(lam * step)
    b = lax.complex(b_re.astype(jnp.float32), b_im.astype(jnp.float32))
    b_bar = ((lam_bar - 1.0) / lam)[..., None] * b
    return lam_bar, b_bar


def s5_scan(u, lam_bar, b_bar, h0, reverse):
    bu = jnp.einsum("gpc,blgc->blgp", b_bar, u.astype(jnp.complex64))
    if h0 is not None:
        bu = bu.at[:, -1 if reverse else 0].add(lam_bar * h0)
    a = jnp.broadcast_to(lam_bar, bu.shape)
    _, h = lax.associative_scan(_linear_combine, (a, bu), reverse=reverse, axis=1)
    return h


def s5_mixer(u_c, u_l, lam_re, lam_im, b_re, b_im, c_re, c_im, log_step, d_skip, w_glu, ctx_out):
    bsz = u_l.shape[0]
    uc = u_c.astype(jnp.float32).reshape(bsz, -1, S5_GROUPS, S5_GROUP_CH)
    ul = u_l.astype(jnp.float32).reshape(bsz, -1, S5_GROUPS, S5_GROUP_CH)
    dsk = d_skip.astype(jnp.float32).reshape(S5_GROUPS, S5_GROUP_CH)
    y_l = dsk * ul
    y_c = dsk * uc if ctx_out else None
    for direction, reverse in ((0, False), (1, True)):
        lam_bar, b_bar = s5_discretize(lam_re[direction], lam_im[direction], b_re[direction], b_im[direction], log_step[direction])
        c_mat = lax.complex(c_re[direction].astype(jnp.float32), c_im[direction].astype(jnp.float32))
        h_c = s5_scan(uc, lam_bar, b_bar, None, reverse)
        h0 = h_c[:, 0] if reverse else h_c[:, -1]
        h_l = s5_scan(ul, lam_bar, b_bar, h0, reverse)
        y_l = y_l + jnp.einsum("gcp,blgp->blgc", c_mat, h_l).real
        if ctx_out:
            y_c = y_c + jnp.einsum("gcp,blgp->blgc", c_mat, h_c).real

    def glu(y):
        z = jax.nn.gelu(y.reshape(bsz, -1, S5_CH))
        return z * jax.nn.sigmoid(z @ w_glu.astype(jnp.float32))

    return (glu(y_c) if ctx_out else None), glu(y_l)


def dense_ctx_attention(q, k, v):
    s = jnp.einsum("bqhd,bkhd->bhqk", q, k).astype(jnp.float32) * HEAD_DIM ** -0.5
    p = jax.nn.softmax(s, axis=-1)
    return jnp.einsum("bhqk,bkhd->bqhd", p, v)


def na_mixer(q_c, k_c, v_c, q_l, k_l, v_l, rpb, rows, ctx_out):
    bsz, seq, _ = q_l.shape
    kh = min(NA_KH, rows)
    kw = NA_KW
    ncb = GRID_W // NA_QB
    kbw = 2 * kw
    scale = HEAD_DIM ** -0.5
    kc = k_c.reshape(bsz, -1, NA_HEADS, HEAD_DIM)
    vc = v_c.reshape(bsz, -1, NA_HEADS, HEAD_DIM)
    ql = q_l.reshape(bsz, rows, ncb, NA_QB, NA_HEADS, HEAD_DIM)
    kg = k_l.reshape(bsz, rows, GRID_W, NA_HEADS, HEAD_DIM)
    vg = v_l.reshape(bsz, rows, GRID_W, NA_HEADS, HEAD_DIM)
    r = jnp.arange(rows)
    row_idx = jnp.clip(r - kh // 2, 0, rows - kh)[:, None] + jnp.arange(kh)[None, :]
    q_col = jnp.arange(GRID_W).reshape(ncb, NA_QB)
    col_idx = jnp.clip(q_col[:, 0] - kw // 2, 0, GRID_W - kbw)[:, None] + jnp.arange(kbw)[None, :]
    k_blk = kg[:, row_idx[:, None, :, None], col_idx[None, :, None, :]]
    v_blk = vg[:, row_idx[:, None, :, None], col_idx[None, :, None, :]]
    win_start = jnp.clip(q_col - kw // 2, 0, GRID_W - kw)
    kcol = col_idx[:, None, :]
    in_win = (kcol >= win_start[..., None]) & (kcol < win_start[..., None] + kw)
    dr = row_idx - r[:, None] + (NA_KH - 1)
    dc = jnp.clip(kcol - q_col[..., None], -(kw - 1), kw - 1) + (NA_KW - 1)
    bias = rpb[:, dr[:, None, None, :, None], dc[None, :, :, None, :]]
    s_nb = jnp.einsum("brnqhd,brnkwhd->bhrnqkw", ql, k_blk).astype(jnp.float32) * scale + bias.astype(jnp.float32)
    s_nb = jnp.where(in_win[:, :, None, :], s_nb, NEG_INF)
    s_cx = jnp.einsum("brnqhd,bchd->bhrnqc", ql, kc).astype(jnp.float32) * scale
    p_nb, p_cx = joint_softmax(s_nb.reshape(s_nb.shape[:5] + (kh * kbw,)), s_cx)
    p_nb = p_nb.reshape(s_nb.shape)
    o = jnp.einsum("bhrnqkw,brnkwhd->brnqhd", p_nb, v_blk) + jnp.einsum("bhrnqc,bchd->brnqhd", p_cx, vc)
    o_l = o.reshape(bsz, seq, NA_DIM)
    o_c = None
    if ctx_out:
        qc = q_c.reshape(bsz, -1, NA_HEADS, HEAD_DIM)
        o_c = dense_ctx_attention(qc, kc, vc).reshape(bsz, -1, NA_DIM)
    return o_c, o_l


def gla_chunked(q, k, v, log_a, s0):
    bsz, L, H, dk = q.shape
    dv = v.shape[-1]
    n = L // GLA_CHUNK

    def chunks(t):
        return t.reshape(bsz, n, GLA_CHUNK, H, t.shape[-1]).transpose(0, 3, 1, 2, 4)

    qh, kh, vh, g = chunks(q), chunks(k), chunks(v), chunks(log_a)
    b = jnp.cumsum(g, axis=3)
    b_last = b[:, :, :, -1:]
    q_dec = qh * jnp.exp(b)
    k_dec = kh * jnp.exp(-b)
    k_end = kh * jnp.exp(b_last - b)
    causal = jnp.tril(jnp.ones((GLA_CHUNK, GLA_CHUNK), dtype=bool))
    a_intra = jnp.where(causal, jnp.einsum("bhnid,bhnjd->bhnij", q_dec, k_dec), 0.0)
    o_intra = jnp.einsum("bhnij,bhnjv->bhniv", a_intra, vh)
    ds = jnp.einsum("bhnjd,bhnjv->bhndv", k_end, vh)
    decay = jnp.exp(b_last[:, :, :, 0])
    if s0 is None:
        s0 = jnp.zeros((bsz, H, dk, dv), jnp.float32)

    def step(s, inp):
        dec, d = inp
        return dec[..., None] * s + d, s

    s_final, s_prev = lax.scan(step, s0, (jnp.moveaxis(decay, 2, 0), jnp.moveaxis(ds, 2, 0)))
    s_prev = jnp.moveaxis(s_prev, 0, 2)
    o_inter = jnp.einsum("bhnid,bhndv->bhniv", q_dec, s_prev)
    o = (o_intra + o_inter).transpose(0, 2, 3, 1, 4).reshape(bsz, L, H, dv)
    return o, s_final


def gla_prep(q, k, v, gf, gb, w_gate2, b_gate):
    bsz, L, _ = q.shape
    shp = (bsz, L, GLA_HEADS, GLA_DK)
    qh = q.astype(jnp.float32).reshape(shp) * GLA_DK ** -0.5
    kh = k.astype(jnp.float32).reshape(shp)
    vh = v.astype(jnp.float32).reshape(bsz, L, GLA_HEADS, GLA_DV)
    la_f = (jax.nn.log_sigmoid((gf @ w_gate2[0] + b_gate[0]).astype(jnp.float32)) / GLA_TAU).reshape(shp)
    la_b = (jax.nn.log_sigmoid((gb @ w_gate2[1] + b_gate[1]).astype(jnp.float32)) / GLA_TAU).reshape(shp)
    return qh, kh, vh, la_f, la_b


def _flip(t):
    return jnp.flip(t, axis=1)


def gla_mixer(q_c, k_c, v_c, gf_c, gb_c, r_c, q_l, k_l, v_l, gf_l, gb_l, r_l, w_gate2, b_gate, g_norm, ctx_out):
    qc, kc, vc, lfc, lbc = gla_prep(q_c, k_c, v_c, gf_c, gb_c, w_gate2, b_gate)
    ql, kl, vl, lfl, lbl = gla_prep(q_l, k_l, v_l, gf_l, gb_l, w_gate2, b_gate)
    o_cf, s_cf = gla_chunked(qc, kc, vc, lfc, None)
    o_cb, s_cb = gla_chunked(_flip(qc), _flip(kc), _flip(vc), _flip(lbc), None)
    o_lf, _ = gla_chunked(ql, kl, vl, lfl, s_cf)
    o_lb, _ = gla_chunked(_flip(ql), _flip(kl), _flip(vl), _flip(lbl), s_cb)

    def finish(o, r):
        bsz, L = o.shape[:2]
        return rmsnorm(o, g_norm).reshape(bsz, L, GLA_V) * jax.nn.silu(r.astype(jnp.float32))

    o_l = finish(o_lf + _flip(o_lb), r_l)
    o_c = finish(o_cf + _flip(o_cb), r_c) if ctx_out else None
    return o_c, o_l


def rotate_axis(t, pos):
    d = t.shape[-1]
    inv_freq = ROPE_BASE ** (-jnp.arange(0, d, 2, dtype=jnp.float32) / d)
    ang = pos.astype(jnp.float32)[:, None] * inv_freq[None, :]
    cos = jnp.cos(ang)[None, :, None, :]
    sin = jnp.sin(ang)[None, :, None, :]
    t1 = t[..., : d // 2].astype(jnp.float32)
    t2 = t[..., d // 2:].astype(jnp.float32)
    return jnp.concatenate([t1 * cos - t2 * sin, t1 * sin + t2 * cos], axis=-1).astype(t.dtype)


def rope_2d(t, pos_row, pos_col):
    half = t.shape[-1] // 2
    return jnp.concatenate([rotate_axis(t[..., :half], pos_row), rotate_axis(t[..., half:], pos_col)], axis=-1)


def swa_mixer(q_c, k_c, v_c, q_l, k_l, v_l, sink, pos_row, pos_col, ctx_out):
    bsz, seq, _ = q_l.shape
    grp = SWA_HEADS // SWA_KV_HEADS
    nb = seq // SWA_BLOCK
    scale = HEAD_DIM ** -0.5
    ql = rope_2d(q_l.reshape(bsz, seq, SWA_HEADS, HEAD_DIM), pos_row, pos_col)
    ql = ql.reshape(bsz, nb, SWA_BLOCK, SWA_KV_HEADS, grp, HEAD_DIM)
    kl = rope_2d(k_l.reshape(bsz, seq, SWA_KV_HEADS, HEAD_DIM), pos_row, pos_col)
    vl = v_l.reshape(bsz, seq, SWA_KV_HEADS, HEAD_DIM)

    def band(t):
        tp = jnp.pad(t, ((0, 0), (SWA_BLOCK, SWA_BLOCK), (0, 0), (0, 0)))
        tp = tp.reshape(bsz, nb + 2, SWA_BLOCK, SWA_KV_HEADS, HEAD_DIM)
        return jnp.concatenate([tp[:, :-2], tp[:, 1:-1], tp[:, 2:]], axis=2)

    k_band, v_band = band(kl), band(vl)
    qpos = jnp.arange(nb)[:, None] * SWA_BLOCK + jnp.arange(SWA_BLOCK)[None, :]
    kpos = (jnp.arange(nb)[:, None] - 1) * SWA_BLOCK + jnp.arange(3 * SWA_BLOCK)[None, :]
    valid = ((kpos[:, None, :] >= 0) & (kpos[:, None, :] < seq)
             & (jnp.abs(qpos[:, :, None] - kpos[:, None, :]) <= SWA_WINDOW))
    kc = k_c.reshape(bsz, -1, SWA_KV_HEADS, HEAD_DIM)
    vc = v_c.reshape(bsz, -1, SWA_KV_HEADS, HEAD_DIM)
    sink_f = sink.astype(jnp.float32)
    s_band = jnp.einsum("bnqhgd,bnkhd->bhgnqk", ql, k_band).astype(jnp.float32) * scale
    s_band = jnp.where(valid, s_band, NEG_INF)
    s_cx = jnp.einsum("bnqhgd,bchd->bhgnqc", ql, kc).astype(jnp.float32) * scale
    s_sink = jnp.broadcast_to(sink_f.reshape(SWA_KV_HEADS, grp, 1, 1, 1), s_cx.shape[:-1] + (1,))
    p_band, p_cx, _ = joint_softmax(s_band, s_cx, s_sink)
    o = jnp.einsum("bhgnqk,bnkhd->bnqhgd", p_band, v_band) + jnp.einsum("bhgnqc,bchd->bnqhgd", p_cx, vc)
    o_l = o.reshape(bsz, seq, SWA_Q)
    o_c = None
    if ctx_out:
        qc = q_c.reshape(bsz, -1, SWA_KV_HEADS, grp, HEAD_DIM)
        sc = jnp.einsum("bqhgd,bkhd->bhgqk", qc, kc).astype(jnp.float32) * scale
        sc_sink = jnp.broadcast_to(sink_f.reshape(SWA_KV_HEADS, grp, 1, 1), sc.shape[:-1] + (1,))
        pc, _ = joint_softmax(sc, sc_sink)
        o_c = jnp.einsum("bhgqk,bkhd->bqhgd", pc, vc).reshape(bsz, -1, SWA_Q)
    return o_c, o_l


def conv_ffn(h, w_up, conv_w, w_down):
    gate, val = jnp.split(h @ w_up, 2, axis=-1)
    gp = jnp.pad(gate, ((0, 0), (1, 1), (0, 0)))
    gate = gp[:, :-2] * conv_w[0] + gp[:, 1:-1] * conv_w[1] + gp[:, 2:] * conv_w[2]
    return (jax.nn.gelu(gate) * val) @ w_down


def token_mixing(pc, pl, rows, pos_row, pos_col, s5_lam_re, s5_lam_im, s5_b_re, s5_b_im, s5_c_re, s5_c_im,
                 s5_log_step, s5_d, s5_w_glu, na_rpb, gla_w_gate2, gla_b_gate, gla_g_norm, swa_sink, ctx_out):
    (a_c, naq_c, nak_c, nav_c, gq_c, gk_c, gv_c, gf_c, gb_c, gr_c, sq_c, sk_c, sv_c) = split_in(pc)
    (a_l, naq_l, nak_l, nav_l, gq_l, gk_l, gv_l, gf_l, gb_l, gr_l, sq_l, sk_l, sv_l) = split_in(pl)
    oa_c, oa_l = s5_mixer(a_c, a_l, s5_lam_re, s5_lam_im, s5_b_re, s5_b_im, s5_c_re, s5_c_im,
                          s5_log_step, s5_d, s5_w_glu, ctx_out)
    ob_c, ob_l = na_mixer(naq_c, nak_c, nav_c, naq_l, nak_l, nav_l, na_rpb, rows, ctx_out)
    oc_c, oc_l = gla_mixer(gq_c, gk_c, gv_c, gf_c, gb_c, gr_c, gq_l, gk_l, gv_l, gf_l, gb_l, gr_l,
                           gla_w_gate2, gla_b_gate, gla_g_norm, ctx_out)
    od_c, od_l = swa_mixer(sq_c, sk_c, sv_c, sq_l, sk_l, sv_l, swa_sink, pos_row, pos_col, ctx_out)
    dt = pl.dtype
    y_l = jnp.concatenate([oa_l.astype(dt), ob_l.astype(dt), oc_l.astype(dt), od_l.astype(dt)], axis=-1)
    y_c = None
    if ctx_out:
        y_c = jnp.concatenate([oa_c.astype(dt), ob_c.astype(dt), oc_c.astype(dt), od_c.astype(dt)], axis=-1)
    return y_c, y_l


def setup_inputs(seed: int = 0) -> dict:
    key = jax.random.key(seed)
    ks = jax.random.split(key, 29)
    f32 = jnp.float32
    L = DEPTH

    def nrm(k, shape, s=1.0):
        return s * jax.random.normal(k, shape, f32)

    n_idx = jnp.arange(S5_STATE, dtype=f32)
    return {
        "x": nrm(ks[0], (BATCH, SEQ, D_MODEL)),
        "c": nrm(ks[1], (BATCH, D_MODEL)),
        "ctx": nrm(ks[2], (BATCH, CTX_LEN, D_MODEL)),
        "c_ctx": nrm(ks[3], (D_MODEL,)),
        "w_mod": nrm(ks[4], (L, D_MODEL, 6 * D_MODEL), D_MODEL ** -0.5),
        "b_mod": nrm(ks[5], (L, 6 * D_MODEL), 0.02),
        "g_pre_mix": 1.0 + nrm(ks[6], (L, D_MODEL), 0.05),
        "g_post_mix": 1.0 + nrm(ks[7], (L, D_MODEL), 0.05),
        "g_pre_ffn": 1.0 + nrm(ks[8], (L, D_MODEL), 0.05),
        "g_post_ffn": 1.0 + nrm(ks[9], (L, D_MODEL), 0.05),
        "w_in": nrm(ks[10], (L, D_MODEL, D_IN), D_MODEL ** -0.5),
        "w_out": nrm(ks[11], (L, D_MIX, D_MODEL), D_MIX ** -0.5),
        "s5_lam_re": -0.5 + nrm(ks[12], (L, 2, S5_GROUPS, S5_STATE), 0.01),
        "s5_lam_im": math.pi * n_idx + nrm(ks[13], (L, 2, S5_GROUPS, S5_STATE), 0.01),
        "s5_b_re": nrm(ks[14], (L, 2, S5_GROUPS, S5_STATE, S5_GROUP_CH), (2 * S5_GROUP_CH) ** -0.5),
        "s5_b_im": nrm(ks[15], (L, 2, S5_GROUPS, S5_STATE, S5_GROUP_CH), (2 * S5_GROUP_CH) ** -0.5),
        "s5_c_re": nrm(ks[16], (L, 2, S5_GROUPS, S5_GROUP_CH, S5_STATE), S5_STATE ** -0.5),
        "s5_c_im": nrm(ks[17], (L, 2, S5_GROUPS, S5_GROUP_CH, S5_STATE), S5_STATE ** -0.5),
        "s5_log_step": jax.random.uniform(ks[18], (L, 2, S5_GROUPS), f32, math.log(1e-3), math.log(1e-1)),
        "s5_d": nrm(ks[19], (L, S5_CH)),
        "s5_w_glu": nrm(ks[20], (L, S5_CH, S5_CH), S5_CH ** -0.5),
        "na_rpb": nrm(ks[21], (L, NA_HEADS, 2 * NA_KH - 1, 2 * NA_KW - 1), 0.1),
        "gla_w_gate2": nrm(ks[22], (L, 2, GLA_RANK, GLA_QK), GLA_RANK ** -0.5),
        "gla_b_gate": nrm(ks[23], (L, 2, GLA_QK), 0.1),
        "gla_g_norm": 1.0 + nrm(ks[24], (L, GLA_DV), 0.05),
        "swa_sink": nrm(ks[25], (L, SWA_HEADS), 0.5),
        "ffn_w_up": nrm(ks[26], (L, D_MODEL, 2 * D_FF), D_MODEL ** -0.5),
        "ffn_conv": nrm(ks[27], (L, FFN_CONV, D_FF), FFN_CONV ** -0.5),
        "ffn_w_down": nrm(ks[28], (L, D_FF, D_MODEL), D_FF ** -0.5),
    }


def reference(x, c, ctx, c_ctx, w_mod, b_mod, g_pre_mix, g_post_mix, g_pre_ffn, g_post_ffn, w_in, w_out,
              s5_lam_re, s5_lam_im, s5_b_re, s5_b_im, s5_c_re, s5_c_im, s5_log_step, s5_d, s5_w_glu,
              na_rpb, gla_w_gate2, gla_b_gate, gla_g_norm, swa_sink, ffn_w_up, ffn_conv, ffn_w_down):
    seq = x.shape[1]
    rows = seq // GRID_W
    t = jnp.arange(seq)
    pos_row, pos_col = t // GRID_W, t % GRID_W
    silu_c = jax.nn.silu(c)
    silu_cc = jax.nn.silu(c_ctx)
    xl, xc = x, ctx
    for l in range(DEPTH):
        ctx_out = l < DEPTH - 1
        mod_l = (silu_c @ w_mod[l] + b_mod[l])[:, None, :]
        mod_c = (silu_cc @ w_mod[l] + b_mod[l])[None, None, :]
        sh_ml, sc_ml, gt_ml, sh_fl, sc_fl, gt_fl = jnp.split(mod_l, 6, axis=-1)
        sh_mc, sc_mc, gt_mc, sh_fc, sc_fc, gt_fc = jnp.split(mod_c, 6, axis=-1)
        hl = rmsnorm(xl, g_pre_mix[l]) * (1.0 + sc_ml) + sh_ml
        hc = rmsnorm(xc, g_pre_mix[l]) * (1.0 + sc_mc) + sh_mc
        y_c, y_l = token_mixing(hc @ w_in[l], hl @ w_in[l], rows, pos_row, pos_col,
                                s5_lam_re[l], s5_lam_im[l], s5_b_re[l], s5_b_im[l], s5_c_re[l], s5_c_im[l],
                                s5_log_step[l], s5_d[l], s5_w_glu[l], na_rpb[l], gla_w_gate2[l], gla_b_gate[l],
                                gla_g_norm[l], swa_sink[l], ctx_out)
        xl = xl + gt_ml * rmsnorm(y_l @ w_out[l], g_post_mix[l])
        hl = rmsnorm(xl, g_pre_ffn[l]) * (1.0 + sc_fl) + sh_fl
        xl = xl + gt_fl * rmsnorm(conv_ffn(hl, ffn_w_up[l], ffn_conv[l], ffn_w_down[l]), g_post_ffn[l])
        if ctx_out:
            xc = xc + gt_mc * rmsnorm(y_c @ w_out[l], g_post_mix[l])
            hc = rmsnorm(xc, g_pre_ffn[l]) * (1.0 + sc_fc) + sh_fc
            xc = xc + gt_fc * rmsnorm(conv_ffn(hc, ffn_w_up[l], ffn_conv[l], ffn_w_down[l]), g_post_ffn[l])
    return xl
```

```python
import functools
import math

import numpy as np
import jax
import jax.numpy as jnp
from jax import lax
from jax.experimental import pallas as pl
from jax.experimental.pallas import tpu as pltpu

F32 = jnp.float32
BF16 = jnp.bfloat16

D_MODEL = 1024
GRID_W = 64
CTX_LEN = 256
HEAD_DIM = 64
EPS = 1e-6
NEG_INF = -1e30

S5_CH = 256
S5_GROUP_CH = 16
S5_GROUPS = S5_CH // S5_GROUP_CH
S5_STATE = 64
S5_CHUNK = 32
S5_K = S5_CHUNK * S5_GROUP_CH

NA_HEADS = 4
NA_KH = 8
NA_KW = 16
NA_QROWS = 4
NA_WROWS = NA_QROWS + NA_KH

GLA_HEADS = 4
GLA_DK = 64
GLA_RANK = 16
GLA_TAU = 16.0
GLA_CHUNK = 64

SWA_HEADS = 4
SWA_KV_HEADS = 2
SWA_WINDOW = 128
SWA_BLOCK = 128
ROPE_BASE = 10000.0

D_FF = 2816
FFN_HALO = 16

COL_S5, COL_NAQ, COL_NAK, COL_NAV = 0, 256, 512, 768
COL_GQ, COL_GK, COL_GV, COL_GR = 1024, 1280, 1536, 1792
COL_SQ, COL_SK, COL_SV, COL_GG = 2048, 2304, 2432, 2560
P_WIDTH = 2688

VMEM_LIMIT = 56 * 1024 * 1024


def _dot(a, b):
    return jnp.dot(a, b, preferred_element_type=F32)


def _dot_nt(a, b):
    return lax.dot_general(a, b, (((1,), (1,)), ((), ())), preferred_element_type=F32)


def _dot_tn(a, b):
    return lax.dot_general(a, b, (((0,), (0,)), ((), ())), preferred_element_type=F32)


def _gelu(x):
    return 0.5 * x * (1.0 + jnp.tanh(math.sqrt(2.0 / math.pi) * (x + 0.044715 * (x * x * x))))


def _sigmoid(x):
    return 1.0 / (1.0 + jnp.exp(-x))


def _rms(x):
    return x * lax.rsqrt(jnp.mean(x * x, axis=-1, keepdims=True) + EPS)


def _params(sem):
    return pltpu.CompilerParams(dimension_semantics=sem, vmem_limit_bytes=VMEM_LIMIT)


def _mod_kernel(c_ref, w_ref, b_ref, o_ref):
    c = c_ref[...]
    s = (c * _sigmoid(c)).astype(BF16)
    o_ref[0] = _dot(s, w_ref[0].astype(BF16)) + b_ref[0]


def _modulation(cvec, w_mod, b_mod):
    depth, d, n = w_mod.shape
    r = cvec.shape[0]
    tn = 1536
    return pl.pallas_call(
        _mod_kernel,
        grid=(depth, n // tn),
        in_specs=[
            pl.BlockSpec((r, d), lambda l, j: (0, 0)),
            pl.BlockSpec((1, d, tn), lambda l, j: (l, 0, j)),
            pl.BlockSpec((1, 1, tn), lambda l, j: (l, 0, j)),
        ],
        out_specs=pl.BlockSpec((1, r, tn), lambda l, j: (l, 0, j)),
        out_shape=jax.ShapeDtypeStruct((depth, r, n), F32),
        compiler_params=_params(("arbitrary", "arbitrary")),
        name="modulation",
    )(cvec, w_mod, b_mod.reshape(depth, 1, n))


def _nm_matmul_kernel(x_ref, g_ref, modl_ref, modc_ref, w_ref, o_ref, h_ref, *, tm, k_shift, k_scale):
    @pl.when(pl.program_id(2) == 0)
    def _():
        y = _rms(x_ref[0]) * g_ref[...]
        row = pl.program_id(1) * tm + lax.broadcasted_iota(jnp.int32, (tm, 1), 0)
        is_ctx = row < CTX_LEN
        scale = jnp.where(is_ctx, modc_ref[0, k_scale:k_scale + 1, :], modl_ref[0, k_scale:k_scale + 1, :])
        shift = jnp.where(is_ctx, modc_ref[0, k_shift:k_shift + 1, :], modl_ref[0, k_shift:k_shift + 1, :])
        h_ref[...] = (y * (1.0 + scale) + shift).astype(BF16)

    o_ref[0] = _dot(h_ref[...], w_ref[...]).astype(o_ref.dtype)


def _nm_matmul(x, g, modl, modc, w, *, k_shift, k_scale, tm, tn):
    b, t, d = x.shape
    n = w.shape[1]
    return pl.pallas_call(
        functools.partial(_nm_matmul_kernel, tm=tm, k_shift=k_shift, k_scale=k_scale),
        grid=(b, t // tm, n // tn),
        in_specs=[
            pl.BlockSpec((1, tm, d), lambda i, j, k: (i, j, 0)),
            pl.BlockSpec((1, d), lambda i, j, k: (0, 0)),
            pl.BlockSpec((1, 6, d), lambda i, j, k: (i, 0, 0)),
            pl.BlockSpec((1, 6, d), lambda i, j, k: (0, 0, 0)),
            pl.BlockSpec((d, tn), lambda i, j, k: (0, k)),
        ],
        out_specs=pl.BlockSpec((1, tm, tn), lambda i, j, k: (i, j, k)),
        out_shape=jax.ShapeDtypeStruct((b, t, n), BF16),
        scratch_shapes=[pltpu.VMEM((tm, d), BF16)],
        compiler_params=_params(("arbitrary", "arbitrary", "arbitrary")),
        name="norm_mod_matmul",
    )(x, g, modl, modc, w)


def _s5_tables(lam_re, lam_im, b_re, b_im, c_re, c_im, log_step):
    tc = S5_CHUNK
    lam = lax.complex(lam_re.astype(F32), lam_im.astype(F32))
    step = jnp.exp(log_step.astype(F32))[..., None]
    z = lam * step
    lam_bar = jnp.exp(z)
    bbar = ((lam_bar - 1.0) / lam)[..., None] * lax.complex(b_re.astype(F32), b_im.astype(F32))
    cm = lax.complex(c_re.astype(F32), c_im.astype(F32))
    tau = jnp.arange(tc + 1, dtype=F32)
    pw = jnp.exp(z[:, :, None, :] * tau[None, None, :, None])
    kk = jnp.einsum("dgop,dgtp,dgpi->dgtoi", cm, pw[:, :, :tc], bbar).real
    s_idx = np.arange(tc)[:, None]
    t_idx = np.arange(tc)[None, :]
    lag = t_idx - s_idx
    kt = kk[:, :, np.clip(lag, 0, tc - 1)]
    kt = jnp.where((lag >= 0)[None, None, :, :, None, None], kt, 0.0)
    tmat = kt.transpose(0, 1, 2, 5, 3, 4).reshape(2, S5_GROUPS, S5_K, S5_K)
    sm = pw[:, :, tc - 1 - np.arange(tc)][:, :, :, None, :] * bbar.transpose(0, 1, 3, 2)[:, :, None, :, :]
    smat = jnp.concatenate([sm.real, sm.imag], axis=-1).reshape(2, S5_GROUPS, S5_K, 2 * S5_STATE)
    cp = cm[:, :, None, :, :] * pw[:, :, 1:tc + 1][:, :, :, None, :]
    cmat = jnp.concatenate([cp.real, -cp.imag], axis=-1)
    cmat = cmat.transpose(0, 1, 4, 2, 3).reshape(2, S5_GROUPS, 2 * S5_STATE, S5_K)
    lt = pw[:, :, tc]
    a1 = jnp.concatenate([lt.real, lt.real], axis=-1)
    a2 = jnp.concatenate([-lt.imag, lt.imag], axis=-1)
    a12 = jnp.stack([a1, a2], axis=2)
    return tmat.astype(BF16), smat.astype(BF16), cmat.astype(BF16), a12


def _s5_kernel(u_ref, t_ref, s_ref, c_ref, a_ref, y_ref, loc_ref, hin_ref, *, bsz, nch):
    u = u_ref[0, 0]
    loc_ref[...] = _dot(u, s_ref[0, 0])
    a1 = a_ref[0, 0, 0:1, :]
    a2 = a_ref[0, 0, 1:2, :]

    def body(k, h):
        rows = pl.ds(pl.multiple_of(k * bsz, bsz), bsz)
        hin_ref[rows, :] = h
        return a1 * h + a2 * pltpu.roll(h, S5_STATE, 1) + loc_ref[rows, :]

    lax.fori_loop(0, nch, body, jnp.zeros((bsz, 2 * S5_STATE), F32))
    y_ref[0, 0] = _dot(u, t_ref[0, 0]) + _dot(hin_ref[...].astype(BF16), c_ref[0, 0])


def _s5_scan(u, tmat, smat, cmat, a12):
    b, t, _ = u.shape
    nch = t // S5_CHUNK
    m = nch * b

    def to_chunks(v):
        v = v.reshape(b, nch, S5_CHUNK, S5_GROUPS, S5_GROUP_CH)
        return v.transpose(3, 1, 0, 2, 4).reshape(S5_GROUPS, m, S5_K)

    def from_chunks(v):
        v = v.reshape(S5_GROUPS, nch, b, S5_CHUNK, S5_GROUP_CH)
        return v.transpose(2, 1, 3, 0, 4).reshape(b, t, S5_CH)

    def seg_flip(v):
        return jnp.concatenate([jnp.flip(v[:, :CTX_LEN], 1), jnp.flip(v[:, CTX_LEN:], 1)], axis=1)

    uu = jnp.stack([to_chunks(u), to_chunks(seg_flip(u))])
    y = pl.pallas_call(
        functools.partial(_s5_kernel, bsz=b, nch=nch),
        grid=(2, S5_GROUPS),
        in_specs=[
            pl.BlockSpec((1, 1, m, S5_K), lambda d, g: (d, g, 0, 0)),
            pl.BlockSpec((1, 1, S5_K, S5_K), lambda d, g: (d, g, 0, 0)),
            pl.BlockSpec((1, 1, S5_K, 2 * S5_STATE), lambda d, g: (d, g, 0, 0)),
            pl.BlockSpec((1, 1, 2 * S5_STATE, S5_K), lambda d, g: (d, g, 0, 0)),
            pl.BlockSpec((1, 1, 2, 2 * S5_STATE), lambda d, g: (d, g, 0, 0)),
        ],
        out_specs=pl.BlockSpec((1, 1, m, S5_K), lambda d, g: (d, g, 0, 0)),
        out_shape=jax.ShapeDtypeStruct((2, S5_GROUPS, m, S5_K), F32),
        scratch_shapes=[pltpu.VMEM((m, 2 * S5_STATE), F32), pltpu.VMEM((m, 2 * S5_STATE), F32)],
        compiler_params=_params(("arbitrary", "arbitrary")),
        name="s5_scan",
    )(uu, tmat, smat, cmat, a12)
    return from_chunks(y[0]), seg_flip(from_chunks(y[1]))


def _na_bias_tables(rpb, rows):
    w = GRID_W
    tabs = []
    for r0 in (0, NA_QROWS, rows - NA_QROWS):
        w0 = min(max(r0 - NA_KH // 2, 0), rows - NA_WROWS)
        qr = r0 + np.arange(NA_QROWS)[:, None, None, None]
        qc = np.arange(w)[None, :, None, None]
        kr = w0 + np.arange(NA_WROWS)[None, None, :, None]
        kc = np.arange(w)[None, None, None, :]
        start = np.clip(qr - NA_KH // 2, 0, rows - NA_KH)
        ws = np.clip(qc - NA_KW // 2, 0, w - NA_KW)
        valid = (kr >= start) & (kr < start + NA_KH) & (kc >= ws) & (kc < ws + NA_KW)
        dr = np.clip(kr - qr + (NA_KH - 1), 0, 2 * NA_KH - 2)
        dc = np.clip(kc - qc, -(NA_KW - 1), NA_KW - 1) + (NA_KW - 1)
        dr, dc, valid = np.broadcast_arrays(dr, dc, valid)
        bias = rpb.astype(F32)[:, dr, dc]
        bias = jnp.where(valid[None], bias, NEG_INF)
        tabs.append(bias.reshape(NA_HEADS, NA_QROWS * w, NA_WROWS * w))
    return jnp.stack(tabs)


def _na_kernel(q_ref, k_ref, v_ref, bt_ref, o_ref, *, rows):
    qn = NA_QROWS * GRID_W
    kn = NA_WROWS * GRID_W
    nblk = rows // NA_QROWS
    scale = HEAD_DIM ** -0.5
    for h in range(NA_HEADS):
        hs = slice(h * HEAD_DIM, (h + 1) * HEAD_DIM)
        kc = k_ref[0, 0:CTX_LEN, hs]
        vc = v_ref[0, 0:CTX_LEN, hs]
        s = _dot_nt(q_ref[0, 0:CTX_LEN, hs] * scale, kc)
        p = jnp.exp(s - jnp.max(s, axis=-1, keepdims=True))
        o = _dot(p.astype(BF16), vc) / jnp.sum(p, axis=-1, keepdims=True)
        o_ref[0, 0:CTX_LEN, hs] = o.astype(o_ref.dtype)

        def block(i, carry):
            w0 = jnp.clip(i * NA_QROWS - NA_KH // 2, 0, rows - NA_WROWS)
            typ = (i > 0).astype(jnp.int32) + (i == nblk - 1).astype(jnp.int32)
            qrows = pl.ds(pl.multiple_of(CTX_LEN + i * qn, qn), qn)
            krows = pl.ds(pl.multiple_of(CTX_LEN + w0 * GRID_W, GRID_W), kn)
            qb = q_ref[0, qrows, hs] * scale
            s_nb = _dot_nt(qb, k_ref[0, krows, hs]) + bt_ref[typ, h]
            s_cx = _dot_nt(qb, kc)
            mx = jnp.maximum(jnp.max(s_nb, axis=-1, keepdims=True), jnp.max(s_cx, axis=-1, keepdims=True))
            p_nb = jnp.exp(s_nb - mx)
            p_cx = jnp.exp(s_cx - mx)
            den = jnp.sum(p_nb, axis=-1, keepdims=True) + jnp.sum(p_cx, axis=-1, keepdims=True)
            ob = (_dot(p_nb.astype(BF16), v_ref[0, krows, hs]) + _dot(p_cx.astype(BF16), vc)) / den
            o_ref[0, qrows, hs] = ob.astype(o_ref.dtype)
            return carry

        lax.fori_loop(0, nblk, block, 0)


def _na_attention(p, bias_tab):
    b, t, _ = p.shape
    rows = (t - CTX_LEN) // GRID_W
    cb = lambda c: (lambda i: (i, 0, c // 256))
    return pl.pallas_call(
        functools.partial(_na_kernel, rows=rows),
        grid=(b,),
        in_specs=[
            pl.BlockSpec((1, t, 256), cb(COL_NAQ)),
            pl.BlockSpec((1, t, 256), cb(COL_NAK)),
            pl.BlockSpec((1, t, 256), cb(COL_NAV)),
            pl.BlockSpec(bias_tab.shape, lambda i: (0, 0, 0, 0)),
        ],
        out_specs=pl.BlockSpec((1, t, 256), lambda i: (i, 0, 0)),
        out_shape=jax.ShapeDtypeStruct((b, t, 256), BF16),
        compiler_params=_params(("arbitrary",)),
        name="na_attention",
    )(p, p, p, bias_tab)


def _log_sigmoid(x):
    return jnp.minimum(x, 0.0) - jnp.log(1.0 + jnp.exp(-jnp.abs(x)))


def _gla_kernel(q_ref, k_ref, v_ref, r_ref, gg_ref, w2_ref, bg_ref, gn_ref, o_ref, la_ref, acc_ref, st_ref, *, t):
    nch = t // GLA_CHUNK
    nctx = CTX_LEN // GLA_CHUNK
    c = GLA_CHUNK
    ri = lax.broadcasted_iota(jnp.int32, (c, c), 0)
    ci = lax.broadcasted_iota(jnp.int32, (c, c), 1)
    scale = GLA_DK ** -0.5
    for d in range(2):
        x = _dot(gg_ref[0], w2_ref[d]) + bg_ref[d:d + 1, :]
        la_ref[...] = _log_sigmoid(x) * (1.0 / GLA_TAU)
        st_ref[...] = jnp.zeros_like(st_ref)
        keep = (ci <= ri) if d == 0 else (ci >= ri)
        tri = keep.astype(F32)

        def chunk(s, carry):
            if d == 0:
                cidx = s
            else:
                cidx = jnp.where(s < nctx, nctx - 1 - s, nch + nctx - 1 - s)
            rows = pl.ds(pl.multiple_of(cidx * c, c), c)
            bc = jnp.dot(tri, la_ref[rows, :], preferred_element_type=F32, precision=lax.Precision.HIGHEST)
            bl = bc[c - 1:c, :] if d == 0 else bc[0:1, :]
            e_pos = jnp.exp(bc)
            e_neg = jnp.exp(-bc)
            e_end = jnp.exp(bl - bc)
            e_last = jnp.exp(bl)
            qd = (q_ref[0, rows, :].astype(F32) * scale * e_pos).astype(BF16)
            kf = k_ref[0, rows, :].astype(F32)
            kd = (kf * e_neg).astype(BF16)
            ke = (kf * e_end).astype(BF16)
            vv = v_ref[0, rows, :]
            for h in range(GLA_HEADS):
                hs = slice(h * GLA_DK, (h + 1) * GLA_DK)
                a = jnp.where(keep, _dot_nt(qd[:, hs], kd[:, hs]), 0.0)
                st = st_ref[h]
                o = _dot(a.astype(BF16), vv[:, hs]) + _dot_nt(qd[:, hs], st.astype(BF16))
                if d == 0:
                    acc_ref[rows, hs] = o
                else:
                    acc_ref[rows, hs] += o
                st_ref[h] = st * e_last[:, hs] + _dot_tn(vv[:, hs], ke[:, hs])
            return carry

        lax.fori_loop(0, nch, chunk, 0)

    gn = gn_ref[...]
    for h in range(GLA_HEADS):
        hs = slice(h * GLA_DK, (h + 1) * GLA_DK)
        r = r_ref[0, :, hs].astype(F32)
        o_ref[0, :, hs] = (_rms(acc_ref[:, hs]) * gn * (r * _sigmoid(r))).astype(o_ref.dtype)


def _gla_attention(p, w2cat, b_gate, g_norm):
    b, t, _ = p.shape
    cb = lambda c: (lambda i: (i, 0, c // 256))
    return pl.pallas_call(
        functools.partial(_gla_kernel, t=t),
        grid=(b,),
        in_specs=[
            pl.BlockSpec((1, t, 256), cb(COL_GQ)),
            pl.BlockSpec((1, t, 256), cb(COL_GK)),
            pl.BlockSpec((1, t, 256), cb(COL_GV)),
            pl.BlockSpec((1, t, 256), cb(COL_GR)),
            pl.BlockSpec((1, t, 128), lambda i: (i, 0, COL_GG // 128)),
            pl.BlockSpec((2, 128, 256), lambda i: (0, 0, 0)),
            pl.BlockSpec((2, 256), lambda i: (0, 0)),
            pl.BlockSpec((1, GLA_DK), lambda i: (0, 0)),
        ],
        out_specs=pl.BlockSpec((1, t, 256), lambda i: (i, 0, 0)),
        out_shape=jax.ShapeDtypeStruct((b, t, 256), BF16),
        scratch_shapes=[
            pltpu.VMEM((t, 256), F32),
            pltpu.VMEM((t, 256), F32),
            pltpu.VMEM((GLA_HEADS, GLA_DK, GLA_DK), F32),
        ],
        compiler_params=_params(("arbitrary",)),
        name="gla_attention",
    )(p, p, p, p, p, w2cat, b_gate, g_norm)


def _rope_tables(seq):
    pos = np.arange(seq)
    half = HEAD_DIM // 2
    inv_freq = ROPE_BASE ** (-np.arange(0, half, 2, dtype=np.float32) / half)
    cos_l, sin_l = [], []
    for p_axis in (pos // GRID_W, pos % GRID_W):
        ang = p_axis.astype(np.float32)[:, None] * inv_freq[None, :]
        cos_l += [np.cos(ang), np.cos(ang)]
        sin_l += [-np.sin(ang), np.sin(ang)]
    cos = np.concatenate(cos_l, axis=-1).astype(np.float32)
    sin = np.concatenate(sin_l, axis=-1).astype(np.float32)
    return jnp.asarray(np.tile(cos, (1, SWA_HEADS))), jnp.asarray(np.tile(sin, (1, SWA_HEADS)))


def _rope(x, cos, sin):
    n = x.shape[-1]
    lane = lax.broadcasted_iota(jnp.int32, x.shape, 1)
    partner = jnp.where((lane % 32) < 16, pltpu.roll(x, n - 16, 1), pltpu.roll(x, 16, 1))
    return x * cos + partner * sin


def _swa_kernel(q_ref, k_ref, v_ref, cos_ref, sin_ref, sink_ref, o_ref, qr_ref, kr_ref, *, t):
    seq = t - CTX_LEN
    nb = seq // SWA_BLOCK
    grp = SWA_HEADS // SWA_KV_HEADS
    kw = 3 * SWA_BLOCK
    scale = HEAD_DIM ** -0.5
    cos = cos_ref[...]
    sin = sin_ref[...]
    qr_ref[...] = (_rope(q_ref[0, CTX_LEN:, :].astype(F32), cos, sin) * scale).astype(BF16)
    kv_w = SWA_KV_HEADS * HEAD_DIM
    kr_ref[...] = _rope(k_ref[0, CTX_LEN:, :].astype(F32), cos[:, :kv_w], sin[:, :kv_w]).astype(BF16)
    row_c = lax.broadcasted_iota(jnp.int32, (grp * CTX_LEN, 1), 0)
    row_b = lax.broadcasted_iota(jnp.int32, (grp * SWA_BLOCK, 1), 0)
    col_b = lax.broadcasted_iota(jnp.int32, (1, kw), 1)
    for hk in range(SWA_KV_HEADS):
        ks = slice(hk * HEAD_DIM, (hk + 1) * HEAD_DIM)
        qs = [slice((hk * grp + g) * HEAD_DIM, (hk * grp + g + 1) * HEAD_DIM) for g in range(grp)]
        s0 = sink_ref[hk * grp]
        s1 = sink_ref[hk * grp + 1]
        kc = k_ref[0, 0:CTX_LEN, ks]
        vc = v_ref[0, 0:CTX_LEN, ks]
        q2 = jnp.concatenate([q_ref[0, 0:CTX_LEN, qs[g]] for g in range(grp)], axis=0) * scale
        s = _dot_nt(q2, kc)
        sk = jnp.where(row_c < CTX_LEN, s0, s1)
        mx = jnp.maximum(jnp.max(s, axis=-1, keepdims=True), sk)
        p = jnp.exp(s - mx)
        den = jnp.sum(p, axis=-1, keepdims=True) + jnp.exp(sk - mx)
        o = _dot(p.astype(BF16), vc) / den
        for g in range(grp):
            o_ref[0, 0:CTX_LEN, qs[g]] = o[g * CTX_LEN:(g + 1) * CTX_LEN].astype(o_ref.dtype)
        skb = jnp.where(row_b < SWA_BLOCK, s0, s1)

        def block(n, carry):
            ws = jnp.clip(n - 1, 0, nb - 3) * SWA_BLOCK
            qrows = pl.ds(pl.multiple_of(n * SWA_BLOCK, SWA_BLOCK), SWA_BLOCK)
            krows = pl.ds(pl.multiple_of(ws, SWA_BLOCK), kw)
            vrows = pl.ds(pl.multiple_of(CTX_LEN + ws, SWA_BLOCK), kw)
            qb = jnp.concatenate([qr_ref[qrows, qs[g]] for g in range(grp)], axis=0)
            qpos = n * SWA_BLOCK + (row_b % SWA_BLOCK)
            kpos = ws + col_b
            valid = jnp.abs(qpos - kpos) <= SWA_WINDOW
            s_b = jnp.where(valid, _dot_nt(qb, kr_ref[krows, ks]), NEG_INF)
            s_c = _dot_nt(qb, kc)
            mxb = jnp.maximum(jnp.maximum(jnp.max(s_b, axis=-1, keepdims=True),
                                          jnp.max(s_c, axis=-1, keepdims=True)), skb)
            p_b = jnp.exp(s_b - mxb)
            p_c = jnp.exp(s_c - mxb)
            denb = (jnp.sum(p_b, axis=-1, keepdims=True) + jnp.sum(p_c, axis=-1, keepdims=True)
                    + jnp.exp(skb - mxb))
            ob = (_dot(p_b.astype(BF16), v_ref[0, vrows, ks]) + _dot(p_c.astype(BF16), vc)) / denb
            orows = pl.ds(pl.multiple_of(CTX_LEN + n * SWA_BLOCK, SWA_BLOCK), SWA_BLOCK)
            for g in range(grp):
                o_ref[0, orows, qs[g]] = ob[g * SWA_BLOCK:(g + 1) * SWA_BLOCK].astype(o_ref.dtype)
            return carry

        lax.fori_loop(0, nb, block, 0)


def _swa_attention(p, cos, sin, sink):
    b, t, _ = p.shape
    seq = t - CTX_LEN
    return pl.pallas_call(
        functools.partial(_swa_kernel, t=t),
        grid=(b,),
        in_specs=[
            pl.BlockSpec((1, t, 256), lambda i: (i, 0, COL_SQ // 256)),
            pl.BlockSpec((1, t, 128), lambda i: (i, 0, COL_SK // 128)),
            pl.BlockSpec((1, t, 128), lambda i: (i, 0, COL_SV // 128)),
            pl.BlockSpec((seq, 256), lambda i: (0, 0)),
            pl.BlockSpec((seq, 256), lambda i: (0, 0)),
            pl.BlockSpec(memory_space=pltpu.SMEM),
        ],
        out_specs=pl.BlockSpec((1, t, 256), lambda i: (i, 0, 0)),
        out_shape=jax.ShapeDtypeStruct((b, t, 256), BF16),
        scratch_shapes=[pltpu.VMEM((seq, 256), BF16), pltpu.VMEM((seq, 128), BF16)],
        compiler_params=_params(("arbitrary",)),
        name="swa_attention",
    )(p, p, p, cos, sin, sink)


def _out_proj_kernel(u_ref, yf_ref, yr_ref, d_ref, wg_ref, na_ref, gla_ref, swa_ref, wo_ref, g_ref,
                     modl_ref, modc_ref, x_ref, o_ref, *, tm):
    y = d_ref[...] * u_ref[0].astype(F32) + yf_ref[0] + yr_ref[0]
    z = _gelu(y)
    a = z * _sigmoid(_dot(z.astype(BF16), wg_ref[...]))
    acc = _dot(a.astype(BF16), wo_ref[0:256, :])
    acc += _dot(na_ref[0], wo_ref[256:512, :])
    acc += _dot(gla_ref[0], wo_ref[512:768, :])
    acc += _dot(swa_ref[0], wo_ref[768:1024, :])
    row = pl.program_id(1) * tm + lax.broadcasted_iota(jnp.int32, (tm, 1), 0)
    gate = jnp.where(row < CTX_LEN, modc_ref[0, 2:3, :], modl_ref[0, 2:3, :])
    o_ref[0] = x_ref[0] + gate * (_rms(acc) * g_ref[...])


def _out_proj(p, yf, yr, s5_d, w_glu, y_na, y_gla, y_swa, w_out, g_post, modl, modc, x, *, tm):
    b, t, d = x.shape
    tok = lambda w: pl.BlockSpec((1, tm, w), lambda i, j: (i, j, 0))
    full = lambda a: pl.BlockSpec(a.shape, lambda i, j: (0,) * a.ndim)
    return pl.pallas_call(
        functools.partial(_out_proj_kernel, tm=tm),
        grid=(b, t // tm),
        in_specs=[
            tok(256), tok(256), tok(256), full(s5_d), full(w_glu), tok(256), tok(256), tok(256),
            full(w_out), full(g_post),
            pl.BlockSpec((1, 6, d), lambda i, j: (i, 0, 0)),
            pl.BlockSpec((1, 6, d), lambda i, j: (0, 0, 0)),
            tok(d),
        ],
        out_specs=tok(d),
        out_shape=jax.ShapeDtypeStruct((b, t, d), F32),
        compiler_params=_params(("arbitrary", "arbitrary")),
        name="out_proj",
    )(p, yf, yr, s5_d, w_glu, y_na, y_gla, y_swa, w_out, g_post, modl, modc, x)


def _ffn_down_kernel(gt_ref, vl_ref, prev_ref, next_ref, cw_ref, wd_ref, g_ref, modl_ref, modc_ref, x_ref, o_ref,
                     *, tm, t):
    row = pl.program_id(1) * tm + lax.broadcasted_iota(jnp.int32, (tm, 1), 0)
    loc = lax.broadcasted_iota(jnp.int32, (tm, 1), 0)
    g = gt_ref[0].astype(F32)
    gp = jnp.where(loc == 0, prev_ref[0, FFN_HALO - 1:FFN_HALO, :].astype(F32), pltpu.roll(g, 1, 0))
    gn = jnp.where(loc == tm - 1, next_ref[0, 0:1, :].astype(F32), pltpu.roll(g, tm - 1, 0))
    gp = jnp.where((row == 0) | (row == CTX_LEN), 0.0, gp)
    gn = jnp.where((row == CTX_LEN - 1) | (row == t - 1), 0.0, gn)
    conv = gp * cw_ref[0:1, :] + g * cw_ref[1:2, :] + gn * cw_ref[2:3, :]
    a = (_gelu(conv) * vl_ref[0].astype(F32)).astype(BF16)
    acc = _dot(a, wd_ref[...])
    gate = jnp.where(row < CTX_LEN, modc_ref[0, 5:6, :], modl_ref[0, 5:6, :])
    o_ref[0] = x_ref[0] + gate * (_rms(acc) * g_ref[...])


def _ffn_down(gv, conv_w, w_down, g_post, modl, modc, x, *, tm):
    b, t, d = x.shape
    nbh = tm // FFN_HALO
    lasth = t // FFN_HALO - 1
    return pl.pallas_call(
        functools.partial(_ffn_down_kernel, tm=tm, t=t),
        grid=(b, t // tm),
        in_specs=[
            pl.BlockSpec((1, tm, D_FF), lambda i, j: (i, j, 0)),
            pl.BlockSpec((1, tm, D_FF), lambda i, j: (i, j, 1)),
            pl.BlockSpec((1, FFN_HALO, D_FF), lambda i, j: (i, jnp.maximum(j * nbh - 1, 0), 0)),
            pl.BlockSpec((1, FFN_HALO, D_FF), lambda i, j: (i, jnp.minimum((j + 1) * nbh, lasth), 0)),
            pl.BlockSpec((3, D_FF), lambda i, j: (0, 0)),
            pl.BlockSpec((D_FF, d), lambda i, j: (0, 0)),
            pl.BlockSpec((1, d), lambda i, j: (0, 0)),
            pl.BlockSpec((1, 6, d), lambda i, j: (i, 0, 0)),
            pl.BlockSpec((1, 6, d), lambda i, j: (0, 0, 0)),
            pl.BlockSpec((1, tm, d), lambda i, j: (i, j, 0)),
        ],
        out_specs=pl.BlockSpec((1, tm, d), lambda i, j: (i, j, 0)),
        out_shape=jax.ShapeDtypeStruct((b, t, d), F32),
        compiler_params=_params(("arbitrary", "arbitrary")),
        name="ffn_down",
    )(gv, gv, gv, gv, conv_w, w_down, g_post, modl, modc, x)


def _reorder_w_in(w_in):
    sizes = (256, 256, 256, 256, 256, 256, 256, 16, 16, 256, 256, 128, 128)
    offs = np.concatenate([[0], np.cumsum(sizes)])
    seg = lambda i: w_in[..., offs[i]:offs[i + 1]]
    pad = jnp.zeros(w_in.shape[:-1] + (P_WIDTH - COL_GG - 2 * GLA_RANK,), w_in.dtype)
    return jnp.concatenate([seg(0), seg(1), seg(2), seg(3), seg(4), seg(5), seg(6), seg(9),
                            seg(10), seg(11), seg(12), seg(7), seg(8), pad], axis=-1)


def kernel(x, c, ctx, c_ctx, w_mod, b_mod, g_pre_mix, g_post_mix, g_pre_ffn, g_post_ffn, w_in, w_out, s5_lam_re, s5_lam_im, s5_b_re, s5_b_im, s5_c_re, s5_c_im, s5_log_step, s5_d, s5_w_glu, na_rpb, gla_w_gate2, gla_b_gate, gla_g_norm, swa_sink, ffn_w_up, ffn_conv, ffn_w_down):
    bsz, seq, d = x.shape
    depth = w_mod.shape[0]
    rows = seq // GRID_W
    t = CTX_LEN + seq
    tm = 768

    cvec = jnp.concatenate([c, c_ctx[None, :]], axis=0)
    cvec = jnp.pad(cvec, ((0, (-cvec.shape[0]) % 8), (0, 0)))
    mod = _modulation(cvec, w_mod, b_mod).reshape(depth, cvec.shape[0], 6, d)

    w_in_r = _reorder_w_in(w_in).astype(BF16)
    w_out_b = w_out.astype(BF16)
    w_glu_b = s5_w_glu.astype(BF16)
    w_up_b = ffn_w_up.astype(BF16)
    w_down_b = ffn_w_down.astype(BF16)
    w2 = jnp.zeros((depth, 2, 128, GLA_HEADS * GLA_DK), F32)
    w2 = w2.at[:, 0, 0:GLA_RANK].set(gla_w_gate2[:, 0]).at[:, 1, GLA_RANK:2 * GLA_RANK].set(gla_w_gate2[:, 1])
    w2 = w2.astype(BF16)
    cos, sin = _rope_tables(seq)

    xc = jnp.concatenate([ctx, x], axis=1)
    for l in range(depth):
        modl = mod[l, :bsz]
        modc = mod[l, bsz:bsz + 1]
        p = _nm_matmul(xc, g_pre_mix[l][None], modl, modc, w_in_r[l], k_shift=0, k_scale=1, tm=tm, tn=P_WIDTH)
        tabs = _s5_tables(s5_lam_re[l], s5_lam_im[l], s5_b_re[l], s5_b_im[l], s5_c_re[l], s5_c_im[l], s5_log_step[l])
        yf, yr = _s5_scan(p[:, :, COL_S5:COL_S5 + S5_CH], *tabs)
        y_na = _na_attention(p, _na_bias_tables(na_rpb[l], rows))
        y_gla = _gla_attention(p, w2[l], gla_b_gate[l], gla_g_norm[l][None])
        y_swa = _swa_attention(p, cos, sin, swa_sink[l])
        xc = _out_proj(p, yf, yr, s5_d[l][None], w_glu_b[l], y_na, y_gla, y_swa, w_out_b[l],
                       g_post_mix[l][None], modl, modc, xc, tm=tm)
        gv = _nm_matmul(xc, g_pre_ffn[l][None], modl, modc, w_up_b[l], k_shift=3, k_scale=4, tm=tm, tn=D_FF)
        xc = _ffn_down(gv, ffn_conv[l], w_down_b[l], g_post_ffn[l][None], modl, modc, xc, tm=256)
    return xc[:, CTX_LEN:]
```

```python
import functools
import math

import numpy as np
import jax
import jax.numpy as jnp
from jax import lax
from jax.experimental import pallas as pl
from jax.experimental.pallas import tpu as pltpu

F32 = jnp.float32
BF16 = jnp.bfloat16

D_MODEL = 1024
GRID_W = 64
CTX_LEN = 256
HEAD_DIM = 64
EPS = 1e-6
NEG_INF = -1e30

S5_CH = 256
S5_GROUP_CH = 16
S5_GROUPS = S5_CH // S5_GROUP_CH
S5_STATE = 64
S5_TC = 128
S5_K = S5_TC * S5_GROUP_CH
S5_NC = 6

NA_HEADS = 4
NA_KH = 8
NA_KW = 16
NA_QROWS = 4
NA_WROWS = NA_QROWS + NA_KH

GLA_HEADS = 4
GLA_DK = 64
GLA_RANK = 16
GLA_TAU = 16.0
GLA_CHUNK = 64
GLA_UNROLL = 4

SWA_HEADS = 4
SWA_KV_HEADS = 2
SWA_WINDOW = 128
SWA_BLOCK = 128
ROPE_BASE = 10000.0

D_FF = 2816
FFN_HALO = 16

COL_S5, COL_NAQ, COL_NAK, COL_NAV = 0, 256, 512, 768
COL_GQ, COL_GK, COL_GV, COL_GR = 1024, 1280, 1536, 1792
COL_SQ, COL_SK, COL_SV, COL_GG = 2048, 2304, 2432, 2560
P_WIDTH = 2688

VMEM_LIMIT = 56 * 1024 * 1024


def _dot(a, b):
    return jnp.dot(a, b, preferred_element_type=F32)


def _dot_exact(a, b):
    return jnp.dot(a, b, preferred_element_type=F32, precision=lax.Precision.HIGHEST)


def _dot_nt(a, b):
    return lax.dot_general(a, b, (((1,), (1,)), ((), ())), preferred_element_type=F32)


def _dot_tn(a, b):
    return lax.dot_general(a, b, (((0,), (0,)), ((), ())), preferred_element_type=F32)


def _gelu(x):
    return 0.5 * x * (1.0 + jnp.tanh(math.sqrt(2.0 / math.pi) * (x + 0.044715 * (x * x * x))))


def _sigmoid(x):
    return 1.0 / (1.0 + jnp.exp(-x))


def _rms(x):
    return x * lax.rsqrt(jnp.mean(x * x, axis=-1, keepdims=True) + EPS)


def _params(sem):
    return pltpu.CompilerParams(dimension_semantics=sem, vmem_limit_bytes=VMEM_LIMIT)


def _mod_kernel(c_ref, w_ref, b_ref, o_ref):
    c = c_ref[...]
    s = (c * _sigmoid(c)).astype(BF16)
    o_ref[0] = _dot(s, w_ref[0].astype(BF16)) + b_ref[0]


def _modulation(cvec, w_mod, b_mod):
    depth, d, n = w_mod.shape
    r = cvec.shape[0]
    tn = 1536
    return pl.pallas_call(
        _mod_kernel,
        grid=(depth, n // tn),
        in_specs=[
            pl.BlockSpec((r, d), lambda l, j: (0, 0)),
            pl.BlockSpec((1, d, tn), lambda l, j: (l, 0, j)),
            pl.BlockSpec((1, 1, tn), lambda l, j: (l, 0, j)),
        ],
        out_specs=pl.BlockSpec((1, r, tn), lambda l, j: (l, 0, j)),
        out_shape=jax.ShapeDtypeStruct((depth, r, n), F32),
        compiler_params=_params(("arbitrary", "arbitrary")),
        name="modulation",
    )(cvec, w_mod, b_mod.reshape(depth, 1, n))


def _nm_matmul_kernel(x_ref, g_ref, modl_ref, modc_ref, w_ref, o_ref, h_ref, *, tm, k_shift, k_scale):
    @pl.when(pl.program_id(2) == 0)
    def _():
        y = _rms(x_ref[0]) * g_ref[...]
        row = pl.program_id(1) * tm + lax.broadcasted_iota(jnp.int32, (tm, 1), 0)
        is_ctx = row < CTX_LEN
        scale = jnp.where(is_ctx, modc_ref[0, k_scale:k_scale + 1, :], modl_ref[0, k_scale:k_scale + 1, :])
        shift = jnp.where(is_ctx, modc_ref[0, k_shift:k_shift + 1, :], modl_ref[0, k_shift:k_shift + 1, :])
        h_ref[...] = (y * (1.0 + scale) + shift).astype(BF16)

    o_ref[0] = _dot(h_ref[...], w_ref[...]).astype(o_ref.dtype)


def _nm_matmul(x, g, modl, modc, w, *, k_shift, k_scale, tm, tn):
    b, t, d = x.shape
    n = w.shape[1]
    return pl.pallas_call(
        functools.partial(_nm_matmul_kernel, tm=tm, k_shift=k_shift, k_scale=k_scale),
        grid=(b, t // tm, n // tn),
        in_specs=[
            pl.BlockSpec((1, tm, d), lambda i, j, k: (i, j, 0)),
            pl.BlockSpec((1, d), lambda i, j, k: (0, 0)),
            pl.BlockSpec((1, 6, d), lambda i, j, k: (i, 0, 0)),
            pl.BlockSpec((1, 6, d), lambda i, j, k: (0, 0, 0)),
            pl.BlockSpec((d, tn), lambda i, j, k: (0, k)),
        ],
        out_specs=pl.BlockSpec((1, tm, tn), lambda i, j, k: (i, j, k)),
        out_shape=jax.ShapeDtypeStruct((b, t, n), BF16),
        scratch_shapes=[pltpu.VMEM((tm, d), BF16)],
        compiler_params=_params(("arbitrary", "arbitrary", "arbitrary")),
        name="norm_mod_matmul",
    )(x, g, modl, modc, w)


def _s5_prep_kernel(lr_ref, lc_ref, ls_ref, bt_ref, ct_ref, t_ref, s_ref, c_ref, a_ref, kap_ref):
    tc = S5_TC
    p = S5_STATE
    lane = lax.broadcasted_iota(jnp.int32, (1, tc), 1).astype(F32)
    sub = lax.broadcasted_iota(jnp.int32, (tc, 1), 0).astype(F32)
    s_idx = lax.broadcasted_iota(jnp.int32, (tc, tc), 0)
    t_idx = lax.broadcasted_iota(jnp.int32, (tc, tc), 1)
    for d in range(2):
        step = jnp.exp(ls_ref[0, d:d + 1, :])
        lre = lr_ref[0, d, 0:1, :]
        lim = lr_ref[0, d, 1:2, :]
        zr = lre * step
        zi = lim * step
        mag = jnp.exp(zr)
        nr = mag * jnp.cos(zi) - 1.0
        ni = mag * jnp.sin(zi)
        den = lre * lre + lim * lim
        cr = (nr * lre + ni * lim) / den
        ci = (ni * lre - nr * lim) / den
        btr = bt_ref[0, d, 0]
        bti = bt_ref[0, d, 1]
        bbr = cr * btr - ci * bti
        bbi = cr * bti + ci * btr
        ma = jnp.exp(zr * tc)
        ar = ma * jnp.cos(zi * tc)
        ai = ma * jnp.sin(zi * tc)
        a_ref[0, 0:1, d * 2 * p:d * 2 * p + p] = ar
        a_ref[0, 0:1, d * 2 * p + p:(d + 1) * 2 * p] = ar
        a_ref[0, 1:2, d * 2 * p:d * 2 * p + p] = -ai
        a_ref[0, 1:2, d * 2 * p + p:(d + 1) * 2 * p] = ai
        zrc = lc_ref[0, d, :, 0:1] * step
        zic = lc_ref[0, d, :, 1:2] * step

        def powers(tau):
            m = jnp.exp(zrc * tau)
            ang = zic * tau
            return m * jnp.cos(ang), m * jnp.sin(ang)

        if d == 0:
            pkr, pki = powers(lane)
            pcr, pci = powers(lane + 1.0)
            es = (tc - 1.0) - sub
        else:
            pkr, pki = powers(jnp.where(lane == 0.0, 0.0, tc - lane))
            pcr, pci = powers(tc - lane)
            es = sub
        ctr = ct_ref[0, d, 0]
        cti = ct_ref[0, d, 1]
        for co in range(S5_GROUP_CH):
            ccr = ctr[:, co:co + 1]
            cci = cti[:, co:co + 1]
            kap_ref[d, co] = (_dot_exact(bbr, ccr * pkr - cci * pki)
                              - _dot_exact(bbi, ccr * pki + cci * pkr))
            cols = slice(co * tc, (co + 1) * tc)
            c_ref[0, d * 2 * p:d * 2 * p + p, cols] = (ccr * pcr - cci * pci).astype(BF16)
            c_ref[0, d * 2 * p + p:(d + 1) * 2 * p, cols] = (-(ccr * pci + cci * pcr)).astype(BF16)
        me = jnp.exp(zr * es)
        er = me * jnp.cos(zi * es)
        ei = me * jnp.sin(zi * es)
        for cin in range(S5_GROUP_CH):
            br = bbr[cin:cin + 1, :]
            bi = bbi[cin:cin + 1, :]
            rows = slice(cin * tc, (cin + 1) * tc)
            s_ref[0, rows, d * 2 * p:d * 2 * p + p] = (er * br - ei * bi).astype(BF16)
            s_ref[0, rows, d * 2 * p + p:(d + 1) * 2 * p] = (er * bi + ei * br).astype(BF16)

    for co in range(S5_GROUP_CH):
        cols = slice(co * tc, (co + 1) * tc)

        def toeplitz(cin, carry):
            kf = jnp.broadcast_to(kap_ref[0, co, pl.ds(cin, 1), :], (tc, tc))
            kr = jnp.broadcast_to(kap_ref[1, co, pl.ds(cin, 1), :], (tc, tc))
            f = pltpu.roll(kf, 0, 1, stride=1, stride_axis=0)
            r = pltpu.roll(kr, 0, 1, stride=1, stride_axis=0)
            blk = jnp.where(t_idx >= s_idx, f, 0.0) + jnp.where(t_idx <= s_idx, r, 0.0)
            t_ref[0, pl.ds(pl.multiple_of(cin * tc, tc), tc), cols] = blk.astype(BF16)
            return carry

        lax.fori_loop(0, S5_GROUP_CH, toeplitz, 0)


def _s5_prep(lam_re, lam_im, b_re, b_im, c_re, c_im, log_step):
    g = S5_GROUPS
    lam = jnp.stack([lam_re, lam_im], axis=-2).astype(F32).transpose(1, 0, 2, 3)
    lam_c = lam.transpose(0, 1, 3, 2)
    ls = log_step.astype(F32).T[:, :, None]
    bt = jnp.stack([b_re, b_im], axis=2).astype(F32).transpose(1, 0, 2, 4, 3)
    ct = jnp.stack([c_re, c_im], axis=2).astype(F32).transpose(1, 0, 2, 4, 3)
    blk = lambda a: pl.BlockSpec((1,) + a.shape[1:], lambda i: (i,) + (0,) * (a.ndim - 1))
    return pl.pallas_call(
        _s5_prep_kernel,
        grid=(g,),
        in_specs=[blk(lam), blk(lam_c), blk(ls), blk(bt), blk(ct)],
        out_specs=[
            pl.BlockSpec((1, S5_K, S5_K), lambda i: (i, 0, 0)),
            pl.BlockSpec((1, S5_K, 4 * S5_STATE), lambda i: (i, 0, 0)),
            pl.BlockSpec((1, 4 * S5_STATE, S5_K), lambda i: (i, 0, 0)),
            pl.BlockSpec((1, 2, 4 * S5_STATE), lambda i: (i, 0, 0)),
        ],
        out_shape=[
            jax.ShapeDtypeStruct((g, S5_K, S5_K), BF16),
            jax.ShapeDtypeStruct((g, S5_K, 4 * S5_STATE), BF16),
            jax.ShapeDtypeStruct((g, 4 * S5_STATE, S5_K), BF16),
            jax.ShapeDtypeStruct((g, 2, 4 * S5_STATE), F32),
        ],
        scratch_shapes=[pltpu.VMEM((2, S5_GROUP_CH, S5_GROUP_CH, S5_TC), F32)],
        compiler_params=_params(("arbitrary",)),
        name="s5_prep",
    )(lam, lam_c, ls, bt, ct)


def _s5_in_kernel(p_ref, o_ref):
    for j in range(S5_NC):
        z = p_ref[0, j * S5_TC:(j + 1) * S5_TC, :].astype(F32).T.astype(BF16)
        for g in range(S5_GROUPS):
            o_ref[g, j, 0] = z[g * S5_GROUP_CH:(g + 1) * S5_GROUP_CH, :]


def _s5_out_kernel(y_ref, o_ref):
    for j in range(S5_NC):
        z = jnp.concatenate([y_ref[g, j, 0] for g in range(S5_GROUPS)], axis=0)
        o_ref[0, j * S5_TC:(j + 1) * S5_TC, :] = z.T


def _s5_scan_kernel(u_ref, t_ref, s_ref, c_ref, a_ref, y_ref, loc_ref, hin_ref, *, bsz, nch):
    w = 2 * S5_STATE
    u = u_ref[0]
    loc_ref[...] = _dot(u, s_ref[0])
    a1f, a2f = a_ref[0, 0:1, 0:w], a_ref[0, 1:2, 0:w]
    a1r, a2r = a_ref[0, 0:1, w:2 * w], a_ref[0, 1:2, w:2 * w]
    nctx = CTX_LEN // S5_TC

    def body(i, carry):
        hf, hr = carry
        kr = jnp.where(i < nctx, nctx - 1 - i, nch + nctx - 1 - i)
        rf = pl.ds(pl.multiple_of(i * bsz, bsz), bsz)
        rr = pl.ds(pl.multiple_of(kr * bsz, bsz), bsz)
        hin_ref[rf, 0:w] = hf
        hin_ref[rr, w:2 * w] = hr
        hf = a1f * hf + a2f * pltpu.roll(hf, S5_STATE, 1) + loc_ref[rf, 0:w]
        hr = a1r * hr + a2r * pltpu.roll(hr, S5_STATE, 1) + loc_ref[rr, w:2 * w]
        return hf, hr

    zero = jnp.zeros((bsz, w), F32)
    lax.fori_loop(0, nch, body, (zero, zero))
    y_ref[0] = _dot(u, t_ref[0]) + _dot(hin_ref[...].astype(BF16), c_ref[0])


def _s5_scan(p, tmat, smat, cmat, amat):
    b, t, _ = p.shape
    nch = t // S5_TC
    m = nch * b
    g = S5_GROUPS
    chunked = pl.BlockSpec((g, S5_NC, 1, S5_GROUP_CH, S5_TC), lambda i, j: (0, j, i, 0, 0))
    tokens = pl.BlockSpec((1, S5_NC * S5_TC, S5_CH), lambda i, j: (i, j, 0))
    ug = pl.pallas_call(
        _s5_in_kernel,
        grid=(b, nch // S5_NC),
        in_specs=[tokens],
        out_specs=chunked,
        out_shape=jax.ShapeDtypeStruct((g, nch, b, S5_GROUP_CH, S5_TC), BF16),
        compiler_params=_params(("arbitrary", "arbitrary")),
        name="s5_to_chunks",
    )(p)
    yg = pl.pallas_call(
        functools.partial(_s5_scan_kernel, bsz=b, nch=nch),
        grid=(g,),
        in_specs=[
            pl.BlockSpec((1, m, S5_K), lambda i: (i, 0, 0)),
            pl.BlockSpec((1, S5_K, S5_K), lambda i: (i, 0, 0)),
            pl.BlockSpec((1, S5_K, 4 * S5_STATE), lambda i: (i, 0, 0)),
            pl.BlockSpec((1, 4 * S5_STATE, S5_K), lambda i: (i, 0, 0)),
            pl.BlockSpec((1, 2, 4 * S5_STATE), lambda i: (i, 0, 0)),
        ],
        out_specs=pl.BlockSpec((1, m, S5_K), lambda i: (i, 0, 0)),
        out_shape=jax.ShapeDtypeStruct((g, m, S5_K), F32),
        scratch_shapes=[pltpu.VMEM((m, 4 * S5_STATE), F32), pltpu.VMEM((m, 4 * S5_STATE), F32)],
        compiler_params=_params(("arbitrary",)),
        name="s5_scan",
    )(ug.reshape(g, m, S5_K), tmat, smat, cmat, amat)
    return pl.pallas_call(
        _s5_out_kernel,
        grid=(b, nch // S5_NC),
        in_specs=[chunked],
        out_specs=tokens,
        out_shape=jax.ShapeDtypeStruct((b, t, S5_CH), F32),
        compiler_params=_params(("arbitrary", "arbitrary")),
        name="s5_from_chunks",
    )(yg.reshape(g, nch, b, S5_GROUP_CH, S5_TC))


def _na_block_geometry(rows):
    geo = []
    for r0 in (0, NA_QROWS, rows - NA_QROWS):
        geo.append((r0, min(max(r0 - NA_KH // 2, 0), rows - NA_WROWS)))
    return geo


def _na_fill_bias(rpb_ref, bt_ref, rows):
    w = GRID_W
    qc = lax.broadcasted_iota(jnp.int32, (w, w), 0)
    kc = lax.broadcasted_iota(jnp.int32, (w, w), 1)
    ws = jnp.clip(qc - NA_KW // 2, 0, w - NA_KW)
    in_win = (kc >= ws) & (kc < ws + NA_KW)
    neg = jnp.full((w, w), NEG_INF, F32)
    for h in range(NA_HEADS):
        cm = []
        for dr in range(2 * NA_KH - 1):
            v = jnp.broadcast_to(rpb_ref[h, dr:dr + 1, :], (w, 128))
            v = pltpu.roll(v, 128 - (NA_KW - 1), 1, stride=1, stride_axis=0)
            cm.append(jnp.where(in_win, v[:, :w], NEG_INF))
        for kind, (r0, w0) in enumerate(_na_block_geometry(rows)):
            for qr in range(NA_QROWS):
                start = min(max(r0 + qr - NA_KH // 2, 0), rows - NA_KH)
                for kr in range(0, NA_WROWS, 2):
                    pair = []
                    for k in (kr, kr + 1):
                        inside = start <= w0 + k < start + NA_KH
                        pair.append(cm[(w0 + k) - (r0 + qr) + NA_KH - 1] if inside else neg)
                    bt_ref[kind, h, qr * w:(qr + 1) * w, kr * w:(kr + 2) * w] = jnp.concatenate(pair, axis=1)


def _na_kernel(q_ref, k_ref, v_ref, rpb_ref, o_ref, bt_ref, *, rows):
    qn = NA_QROWS * GRID_W
    kn = NA_WROWS * GRID_W
    nblk = rows // NA_QROWS
    scale = HEAD_DIM ** -0.5

    @pl.when(pl.program_id(0) == 0)
    def _():
        _na_fill_bias(rpb_ref, bt_ref, rows)

    for h in range(NA_HEADS):
        hs = slice(h * HEAD_DIM, (h + 1) * HEAD_DIM)
        kc = k_ref[0, 0:CTX_LEN, hs]
        vc = v_ref[0, 0:CTX_LEN, hs]
        s = _dot_nt(q_ref[0, 0:CTX_LEN, hs] * scale, kc)
        p = jnp.exp(s - jnp.max(s, axis=-1, keepdims=True))
        o = _dot(p.astype(BF16), vc) / jnp.sum(p, axis=-1, keepdims=True)
        o_ref[0, 0:CTX_LEN, hs] = o.astype(o_ref.dtype)

        def block(i, carry):
            w0 = jnp.clip(i * NA_QROWS - NA_KH // 2, 0, rows - NA_WROWS)
            kind = (i > 0).astype(jnp.int32) + (i == nblk - 1).astype(jnp.int32)
            qrows = pl.ds(pl.multiple_of(CTX_LEN + i * qn, qn), qn)
            krows = pl.ds(pl.multiple_of(CTX_LEN + w0 * GRID_W, GRID_W), kn)
            qb = q_ref[0, qrows, hs] * scale
            s_nb = _dot_nt(qb, k_ref[0, krows, hs]) + bt_ref[kind, h]
            s_cx = _dot_nt(qb, kc)
            mx = jnp.maximum(jnp.max(s_nb, axis=-1, keepdims=True), jnp.max(s_cx, axis=-1, keepdims=True))
            p_nb = jnp.exp(s_nb - mx)
            p_cx = jnp.exp(s_cx - mx)
            den = jnp.sum(p_nb, axis=-1, keepdims=True) + jnp.sum(p_cx, axis=-1, keepdims=True)
            ob = (_dot(p_nb.astype(BF16), v_ref[0, krows, hs]) + _dot(p_cx.astype(BF16), vc)) / den
            o_ref[0, qrows, hs] = ob.astype(o_ref.dtype)
            return carry

        lax.fori_loop(0, nblk, block, 0, unroll=2)


def _na_attention(p, rpb):
    b, t, _ = p.shape
    rows = (t - CTX_LEN) // GRID_W
    rpb_p = jnp.pad(rpb.astype(F32), ((0, 0), (0, 1), (0, 128 - rpb.shape[2])))
    cb = lambda c: (lambda i: (i, 0, c // 256))
    return pl.pallas_call(
        functools.partial(_na_kernel, rows=rows),
        grid=(b,),
        in_specs=[
            pl.BlockSpec((1, t, 256), cb(COL_NAQ)),
            pl.BlockSpec((1, t, 256), cb(COL_NAK)),
            pl.BlockSpec((1, t, 256), cb(COL_NAV)),
            pl.BlockSpec(rpb_p.shape, lambda i: (0, 0, 0)),
        ],
        out_specs=pl.BlockSpec((1, t, 256), lambda i: (i, 0, 0)),
        out_shape=jax.ShapeDtypeStruct((b, t, 256), BF16),
        scratch_shapes=[pltpu.VMEM((3, NA_HEADS, NA_QROWS * GRID_W, NA_WROWS * GRID_W), F32)],
        compiler_params=_params(("arbitrary",)),
        name="na_attention",
    )(p, p, p, rpb_p)


def _log_sigmoid(x):
    return jnp.minimum(x, 0.0) - jnp.log(1.0 + jnp.exp(-jnp.abs(x)))


def _gla_kernel(q_ref, k_ref, v_ref, r_ref, gg_ref, w2_ref, bg_ref, gn_ref, o_ref,
                la_ref, acc_ref, qd_ref, ke_ref, el_ref, st_ref, *, t):
    nch = t // GLA_CHUNK
    nctx = CTX_LEN // GLA_CHUNK
    c = GLA_CHUNK
    hd = GLA_HEADS * GLA_DK
    scale = GLA_DK ** -0.5
    ri = lax.broadcasted_iota(jnp.int32, (c, c), 0)
    ci = lax.broadcasted_iota(jnp.int32, (c, c), 1)
    row_i = lax.broadcasted_iota(jnp.int32, (c, hd), 0)
    col_j = lax.broadcasted_iota(jnp.int32, (c, hd), 1) % c
    same_head = (lax.broadcasted_iota(jnp.int32, (hd, hd), 0) // GLA_DK
                 == lax.broadcasted_iota(jnp.int32, (hd, hd), 1) // GLA_DK)
    for d in range(2):
        x = _dot(gg_ref[0], w2_ref[d]) + bg_ref[d:d + 1, :]
        la_ref[...] = _log_sigmoid(x) * (1.0 / GLA_TAU)
        tri = ((ci <= ri) if d == 0 else (ci >= ri)).astype(BF16)
        keep = (col_j <= row_i) if d == 0 else (col_j >= row_i)

        def order(s):
            if d == 0:
                return s
            return jnp.where(s < nctx, nctx - 1 - s, nch + nctx - 1 - s)

        def intra(s, carry):
            rows = pl.ds(pl.multiple_of(s * c, c), c)
            la = la_ref[rows, :]
            la_hi = la.astype(BF16)
            la_lo = (la - la_hi.astype(F32)).astype(BF16)
            bc = _dot(tri, la_hi) + _dot(tri, la_lo)
            bl = bc[c - 1:c, :] if d == 0 else bc[0:1, :]
            qd = (q_ref[0, rows, :].astype(F32) * scale * jnp.exp(bc)).astype(BF16)
            kf = k_ref[0, rows, :].astype(F32)
            kd = (kf * jnp.exp(-bc)).astype(BF16)
            qd_ref[rows, :] = qd
            ke_ref[rows, :] = (kf * jnp.exp(bl - bc)).astype(BF16)
            el_ref[s] = jnp.exp(bl)
            kbd = jnp.where(same_head, jnp.concatenate([kd] * GLA_HEADS, axis=0), 0.0)
            vbd = jnp.where(same_head, jnp.concatenate([v_ref[0, rows, :]] * GLA_HEADS, axis=0), 0.0)
            a = jnp.where(keep, _dot_nt(qd, kbd), 0.0)
            o = _dot(a.astype(BF16), vbd)
            if d == 0:
                acc_ref[rows, :] = o
            else:
                acc_ref[rows, :] += o
            return carry

        lax.fori_loop(0, nch, intra, 0, unroll=GLA_UNROLL)
        st_ref[...] = jnp.zeros_like(st_ref)

        def inter(s, carry):
            cidx = order(s)
            rows = pl.ds(pl.multiple_of(cidx * c, c), c)
            st = st_ref[...]
            acc_ref[rows, :] += _dot_nt(qd_ref[rows, :], st.astype(BF16))
            ds = _dot_tn(v_ref[0, rows, :], ke_ref[rows, :])
            st_ref[...] = st * el_ref[cidx] + jnp.where(same_head, ds, 0.0)
            return carry

        lax.fori_loop(0, nch, inter, 0, unroll=GLA_UNROLL)

    gn = gn_ref[...]
    for h in range(GLA_HEADS):
        hs = slice(h * GLA_DK, (h + 1) * GLA_DK)
        r = r_ref[0, :, hs].astype(F32)
        o_ref[0, :, hs] = (_rms(acc_ref[:, hs]) * gn * (r * _sigmoid(r))).astype(o_ref.dtype)


def _gla_attention(p, w2cat, b_gate, g_norm):
    b, t, _ = p.shape
    hd = GLA_HEADS * GLA_DK
    cb = lambda c: (lambda i: (i, 0, c // 256))
    return pl.pallas_call(
        functools.partial(_gla_kernel, t=t),
        grid=(b,),
        in_specs=[
            pl.BlockSpec((1, t, 256), cb(COL_GQ)),
            pl.BlockSpec((1, t, 256), cb(COL_GK)),
            pl.BlockSpec((1, t, 256), cb(COL_GV)),
            pl.BlockSpec((1, t, 256), cb(COL_GR)),
            pl.BlockSpec((1, t, 128), lambda i: (i, 0, COL_GG // 128)),
            pl.BlockSpec((2, 128, 256), lambda i: (0, 0, 0)),
            pl.BlockSpec((2, 256), lambda i: (0, 0)),
            pl.BlockSpec((1, GLA_DK), lambda i: (0, 0)),
        ],
        out_specs=pl.BlockSpec((1, t, 256), lambda i: (i, 0, 0)),
        out_shape=jax.ShapeDtypeStruct((b, t, 256), BF16),
        scratch_shapes=[
            pltpu.VMEM((t, hd), F32),
            pltpu.VMEM((t, hd), F32),
            pltpu.VMEM((t, hd), BF16),
            pltpu.VMEM((t, hd), BF16),
            pltpu.VMEM((t // GLA_CHUNK, 1, hd), F32),
            pltpu.VMEM((hd, hd), F32),
        ],
        compiler_params=_params(("arbitrary",)),
        name="gla_attention",
    )(p, p, p, p, p, w2cat, b_gate, g_norm)


def _rope_tables(seq):
    pos = np.arange(seq)
    half = HEAD_DIM // 2
    inv_freq = ROPE_BASE ** (-np.arange(0, half, 2, dtype=np.float32) / half)
    cos_l, sin_l = [], []
    for p_axis in (pos // GRID_W, pos % GRID_W):
        ang = p_axis.astype(np.float32)[:, None] * inv_freq[None, :]
        cos_l += [np.cos(ang), np.cos(ang)]
        sin_l += [-np.sin(ang), np.sin(ang)]
    cos = np.concatenate(cos_l, axis=-1).astype(np.float32)
    sin = np.concatenate(sin_l, axis=-1).astype(np.float32)
    return jnp.asarray(np.tile(cos, (1, SWA_HEADS))), jnp.asarray(np.tile(sin, (1, SWA_HEADS)))


def _rope(x, cos, sin):
    n = x.shape[-1]
    lane = lax.broadcasted_iota(jnp.int32, x.shape, 1)
    partner = jnp.where((lane % 32) < 16, pltpu.roll(x, n - 16, 1), pltpu.roll(x, 16, 1))
    return x * cos + partner * sin


def _swa_kernel(q_ref, k_ref, v_ref, cos_ref, sin_ref, sink_ref, o_ref, qr_ref, kr_ref, *, t):
    seq = t - CTX_LEN
    nb = seq // SWA_BLOCK
    grp = SWA_HEADS // SWA_KV_HEADS
    kw = 3 * SWA_BLOCK
    scale = HEAD_DIM ** -0.5
    cos = cos_ref[...]
    sin = sin_ref[...]
    qr_ref[...] = (_rope(q_ref[0, CTX_LEN:, :].astype(F32), cos, sin) * scale).astype(BF16)
    kv_w = SWA_KV_HEADS * HEAD_DIM
    kr_ref[...] = _rope(k_ref[0, CTX_LEN:, :].astype(F32), cos[:, :kv_w], sin[:, :kv_w]).astype(BF16)
    row_c = lax.broadcasted_iota(jnp.int32, (grp * CTX_LEN, 1), 0)
    row_b = lax.broadcasted_iota(jnp.int32, (grp * SWA_BLOCK, 1), 0)
    col_b = lax.broadcasted_iota(jnp.int32, (1, kw), 1)
    for hk in range(SWA_KV_HEADS):
        ks = slice(hk * HEAD_DIM, (hk + 1) * HEAD_DIM)
        qs = [slice((hk * grp + g) * HEAD_DIM, (hk * grp + g + 1) * HEAD_DIM) for g in range(grp)]
        s0 = sink_ref[hk * grp]
        s1 = sink_ref[hk * grp + 1]
        kc = k_ref[0, 0:CTX_LEN, ks]
        vc = v_ref[0, 0:CTX_LEN, ks]
        q2 = jnp.concatenate([q_ref[0, 0:CTX_LEN, qs[g]] for g in range(grp)], axis=0) * scale
        s = _dot_nt(q2, kc)
        sk = jnp.where(row_c < CTX_LEN, s0, s1)
        mx = jnp.maximum(jnp.max(s, axis=-1, keepdims=True), sk)
        p = jnp.exp(s - mx)
        den = jnp.sum(p, axis=-1, keepdims=True) + jnp.exp(sk - mx)
        o = _dot(p.astype(BF16), vc) / den
        for g in range(grp):
            o_ref[0, 0:CTX_LEN, qs[g]] = o[g * CTX_LEN:(g + 1) * CTX_LEN].astype(o_ref.dtype)
        skb = jnp.where(row_b < SWA_BLOCK, s0, s1)

        def block(n, carry):
            ws = jnp.clip(n - 1, 0, nb - 3) * SWA_BLOCK
            qrows = pl.ds(pl.multiple_of(n * SWA_BLOCK, SWA_BLOCK), SWA_BLOCK)
            krows = pl.ds(pl.multiple_of(ws, SWA_BLOCK), kw)
            vrows = pl.ds(pl.multiple_of(CTX_LEN + ws, SWA_BLOCK), kw)
            qb = jnp.concatenate([qr_ref[qrows, qs[g]] for g in range(grp)], axis=0)
            qpos = n * SWA_BLOCK + (row_b % SWA_BLOCK)
            kpos = ws + col_b
            valid = jnp.abs(qpos - kpos) <= SWA_WINDOW
            s_b = jnp.where(valid, _dot_nt(qb, kr_ref[krows, ks]), NEG_INF)
            s_c = _dot_nt(qb, kc)
            mxb = jnp.maximum(jnp.maximum(jnp.max(s_b, axis=-1, keepdims=True),
                                          jnp.max(s_c, axis=-1, keepdims=True)), skb)
            p_b = jnp.exp(s_b - mxb)
            p_c = jnp.exp(s_c - mxb)
            denb = (jnp.sum(p_b, axis=-1, keepdims=True) + jnp.sum(p_c, axis=-1, keepdims=True)
                    + jnp.exp(skb - mxb))
            ob = (_dot(p_b.astype(BF16), v_ref[0, vrows, ks]) + _dot(p_c.astype(BF16), vc)) / denb
            orows = pl.ds(pl.multiple_of(CTX_LEN + n * SWA_BLOCK, SWA_BLOCK), SWA_BLOCK)
            for g in range(grp):
                o_ref[0, orows, qs[g]] = ob[g * SWA_BLOCK:(g + 1) * SWA_BLOCK].astype(o_ref.dtype)
            return carry

        lax.fori_loop(0, nb, block, 0, unroll=2)


def _swa_attention(p, cos, sin, sink):
    b, t, _ = p.shape
    seq = t - CTX_LEN
    return pl.pallas_call(
        functools.partial(_swa_kernel, t=t),
        grid=(b,),
        in_specs=[
            pl.BlockSpec((1, t, 256), lambda i: (i, 0, COL_SQ // 256)),
            pl.BlockSpec((1, t, 128), lambda i: (i, 0, COL_SK // 128)),
            pl.BlockSpec((1, t, 128), lambda i: (i, 0, COL_SV // 128)),
            pl.BlockSpec((seq, 256), lambda i: (0, 0)),
            pl.BlockSpec((seq, 256), lambda i: (0, 0)),
            pl.BlockSpec(memory_space=pltpu.SMEM),
        ],
        out_specs=pl.BlockSpec((1, t, 256), lambda i: (i, 0, 0)),
        out_shape=jax.ShapeDtypeStruct((b, t, 256), BF16),
        scratch_shapes=[pltpu.VMEM((seq, 256), BF16), pltpu.VMEM((seq, 128), BF16)],
        compiler_params=_params(("arbitrary",)),
        name="swa_attention",
    )(p, p, p, cos, sin, sink)


def _out_proj_kernel(u_ref, ys_ref, d_ref, wg_ref, na_ref, gla_ref, swa_ref, wo_ref, g_ref,
                     modl_ref, modc_ref, x_ref, o_ref, *, tm):
    y = d_ref[...] * u_ref[0].astype(F32) + ys_ref[0]
    z = _gelu(y)
    a = z * _sigmoid(_dot(z.astype(BF16), wg_ref[...]))
    acc = _dot(a.astype(BF16), wo_ref[0:256, :])
    acc += _dot(na_ref[0], wo_ref[256:512, :])
    acc += _dot(gla_ref[0], wo_ref[512:768, :])
    acc += _dot(swa_ref[0], wo_ref[768:1024, :])
    row = pl.program_id(1) * tm + lax.broadcasted_iota(jnp.int32, (tm, 1), 0)
    gate = jnp.where(row < CTX_LEN, modc_ref[0, 2:3, :], modl_ref[0, 2:3, :])
    o_ref[0] = x_ref[0] + gate * (_rms(acc) * g_ref[...])


def _out_proj(p, ys, s5_d, w_glu, y_na, y_gla, y_swa, w_out, g_post, modl, modc, x, *, tm):
    b, t, d = x.shape
    tok = lambda w: pl.BlockSpec((1, tm, w), lambda i, j: (i, j, 0))
    full = lambda a: pl.BlockSpec(a.shape, lambda i, j: (0,) * a.ndim)
    return pl.pallas_call(
        functools.partial(_out_proj_kernel, tm=tm),
        grid=(b, t // tm),
        in_specs=[
            tok(256), tok(256), full(s5_d), full(w_glu), tok(256), tok(256), tok(256),
            full(w_out), full(g_post),
            pl.BlockSpec((1, 6, d), lambda i, j: (i, 0, 0)),
            pl.BlockSpec((1, 6, d), lambda i, j: (0, 0, 0)),
            tok(d),
        ],
        out_specs=tok(d),
        out_shape=jax.ShapeDtypeStruct((b, t, d), F32),
        compiler_params=_params(("arbitrary", "arbitrary")),
        name="out_proj",
    )(p, ys, s5_d, w_glu, y_na, y_gla, y_swa, w_out, g_post, modl, modc, x)


def _ffn_down_kernel(gt_ref, vl_ref, prev_ref, next_ref, cw_ref, wd_ref, g_ref, modl_ref, modc_ref, x_ref, o_ref,
                     *, tm, t):
    row = pl.program_id(1) * tm + lax.broadcasted_iota(jnp.int32, (tm, 1), 0)
    loc = lax.broadcasted_iota(jnp.int32, (tm, 1), 0)
    g = gt_ref[0].astype(F32)
    gp = jnp.where(loc == 0, prev_ref[0, FFN_HALO - 1:FFN_HALO, :].astype(F32), pltpu.roll(g, 1, 0))
    gn = jnp.where(loc == tm - 1, next_ref[0, 0:1, :].astype(F32), pltpu.roll(g, tm - 1, 0))
    gp = jnp.where((row == 0) | (row == CTX_LEN), 0.0, gp)
    gn = jnp.where((row == CTX_LEN - 1) | (row == t - 1), 0.0, gn)
    conv = gp * cw_ref[0:1, :] + g * cw_ref[1:2, :] + gn * cw_ref[2:3, :]
    a = (_gelu(conv) * vl_ref[0].astype(F32)).astype(BF16)
    acc = _dot(a, wd_ref[...])
    gate = jnp.where(row < CTX_LEN, modc_ref[0, 5:6, :], modl_ref[0, 5:6, :])
    o_ref[0] = x_ref[0] + gate * (_rms(acc) * g_ref[...])


def _ffn_down(gv, conv_w, w_down, g_post, modl, modc, x, *, tm):
    b, t, d = x.shape
    nbh = tm // FFN_HALO
    lasth = t // FFN_HALO - 1
    return pl.pallas_call(
        functools.partial(_ffn_down_kernel, tm=tm, t=t),
        grid=(b, t // tm),
        in_specs=[
            pl.BlockSpec((1, tm, D_FF), lambda i, j: (i, j, 0)),
            pl.BlockSpec((1, tm, D_FF), lambda i, j: (i, j, 1)),
            pl.BlockSpec((1, FFN_HALO, D_FF), lambda i, j: (i, jnp.maximum(j * nbh - 1, 0), 0)),
            pl.BlockSpec((1, FFN_HALO, D_FF), lambda i, j: (i, jnp.minimum((j + 1) * nbh, lasth), 0)),
            pl.BlockSpec((3, D_FF), lambda i, j: (0, 0)),
            pl.BlockSpec((D_FF, d), lambda i, j: (0, 0)),
            pl.BlockSpec((1, d), lambda i, j: (0, 0)),
            pl.BlockSpec((1, 6, d), lambda i, j: (i, 0, 0)),
            pl.BlockSpec((1, 6, d), lambda i, j: (0, 0, 0)),
            pl.BlockSpec((1, tm, d), lambda i, j: (i, j, 0)),
        ],
        out_specs=pl.BlockSpec((1, tm, d), lambda i, j: (i, j, 0)),
        out_shape=jax.ShapeDtypeStruct((b, t, d), F32),
        compiler_params=_params(("arbitrary", "arbitrary")),
        name="ffn_down",
    )(gv, gv, gv, gv, conv_w, w_down, g_post, modl, modc, x)


def _reorder_w_in(w_in):
    sizes = (256, 256, 256, 256, 256, 256, 256, 16, 16, 256, 256, 128, 128)
    offs = np.concatenate([[0], np.cumsum(sizes)])
    seg = lambda i: w_in[..., offs[i]:offs[i + 1]]
    pad = jnp.zeros(w_in.shape[:-1] + (P_WIDTH - COL_GG - 2 * GLA_RANK,), w_in.dtype)
    return jnp.concatenate([seg(0), seg(1), seg(2), seg(3), seg(4), seg(5), seg(6), seg(9),
                            seg(10), seg(11), seg(12), seg(7), seg(8), pad], axis=-1)


def kernel(x, c, ctx, c_ctx, w_mod, b_mod, g_pre_mix, g_post_mix, g_pre_ffn, g_post_ffn, w_in, w_out, s5_lam_re, s5_lam_im, s5_b_re, s5_b_im, s5_c_re, s5_c_im, s5_log_step, s5_d, s5_w_glu, na_rpb, gla_w_gate2, gla_b_gate, gla_g_norm, swa_sink, ffn_w_up, ffn_conv, ffn_w_down):
    bsz, seq, d = x.shape
    depth = w_mod.shape[0]
    t = CTX_LEN + seq
    tm = 768

    cvec = jnp.concatenate([c, c_ctx[None, :]], axis=0)
    cvec = jnp.pad(cvec, ((0, (-cvec.shape[0]) % 8), (0, 0)))
    mod = _modulation(cvec, w_mod, b_mod).reshape(depth, cvec.shape[0], 6, d)

    w_in_r = _reorder_w_in(w_in).astype(BF16)
    w_out_b = w_out.astype(BF16)
    w_glu_b = s5_w_glu.astype(BF16)
    w_up_b = ffn_w_up.astype(BF16)
    w_down_b = ffn_w_down.astype(BF16)
    w2 = jnp.zeros((depth, 2, 128, GLA_HEADS * GLA_DK), F32)
    w2 = w2.at[:, 0, 0:GLA_RANK].set(gla_w_gate2[:, 0]).at[:, 1, GLA_RANK:2 * GLA_RANK].set(gla_w_gate2[:, 1])
    w2 = w2.astype(BF16)
    cos, sin = _rope_tables(seq)

    xc = jnp.concatenate([ctx, x], axis=1)
    for l in range(depth):
        modl = mod[l, :bsz]
        modc = mod[l, bsz:bsz + 1]
        p = _nm_matmul(xc, g_pre_mix[l][None], modl, modc, w_in_r[l], k_shift=0, k_scale=1, tm=tm, tn=P_WIDTH)
        tabs = _s5_prep(s5_lam_re[l], s5_lam_im[l], s5_b_re[l], s5_b_im[l], s5_c_re[l], s5_c_im[l], s5_log_step[l])
        ys = _s5_scan(p, *tabs)
        y_na = _na_attention(p, na_rpb[l])
        y_gla = _gla_attention(p, w2[l], gla_b_gate[l], gla_g_norm[l][None])
        y_swa = _swa_attention(p, cos, sin, swa_sink[l])
        xc = _out_proj(p, ys, s5_d[l][None], w_glu_b[l], y_na, y_gla, y_swa, w_out_b[l],
                       g_post_mix[l][None], modl, modc, xc, tm=tm)
        gv = _nm_matmul(xc, g_pre_ffn[l][None], modl, modc, w_up_b[l], k_shift=3, k_scale=4, tm=tm, tn=D_FF)
        xc = _ffn_down(gv, ffn_conv[l], w_down_b[l], g_post_ffn[l][None], modl, modc, xc, tm=256)
    return xc[:, CTX_LEN:]
```

```python
import functools
import math

import numpy as np
import jax
import jax.numpy as jnp
from jax import lax
from jax.experimental import pallas as pl
from jax.experimental.pallas import tpu as pltpu

F32 = jnp.float32
BF16 = jnp.bfloat16

D_MODEL = 1024
GRID_W = 64
CTX_LEN = 256
HEAD_DIM = 64
EPS = 1e-6
NEG_INF = -1e30

S5_CH = 256
S5_GROUP_CH = 16
S5_GROUPS = S5_CH // S5_GROUP_CH
S5_STATE = 64
S5_TC = 128
S5_K = S5_TC * S5_GROUP_CH
S5_NC = 6

NA_HEADS = 4
NA_KH = 8
NA_KW = 16
NA_QROWS = 4
NA_WROWS = NA_QROWS + NA_KH

GLA_HEADS = 4
GLA_DK = 64
GLA_RANK = 16
GLA_TAU = 16.0
GLA_CHUNK = 64
GLA_GROUP = CTX_LEN // GLA_CHUNK

SWA_HEADS = 4
SWA_KV_HEADS = 2
SWA_WINDOW = 128
SWA_BLOCK = 128
ROPE_BASE = 10000.0

D_FF = 2816
FFN_HALO = 16

COL_S5, COL_NAQ, COL_NAK, COL_NAV = 0, 256, 512, 768
COL_GQ, COL_GK, COL_GV, COL_GR = 1024, 1280, 1536, 1792
COL_SQ, COL_SK, COL_SV, COL_GG = 2048, 2304, 2432, 2560
P_WIDTH = 2688

VMEM_LIMIT = 56 * 1024 * 1024


def _dot(a, b):
    return jnp.dot(a, b, preferred_element_type=F32)


def _dot_exact(a, b):
    return jnp.dot(a, b, preferred_element_type=F32, precision=lax.Precision.HIGHEST)


def _dot_nt(a, b):
    return lax.dot_general(a, b, (((1,), (1,)), ((), ())), preferred_element_type=F32)


def _dot_tn(a, b):
    return lax.dot_general(a, b, (((0,), (0,)), ((), ())), preferred_element_type=F32)


def _gelu(x):
    k = math.sqrt(2.0 / math.pi)
    return x * (0.5 + 0.5 * jnp.tanh(x * (k + (k * 0.044715) * (x * x))))


def _sigmoid(x):
    return 1.0 / (1.0 + jnp.exp(-x))


def _rms(x):
    return x * lax.rsqrt(jnp.mean(x * x, axis=-1, keepdims=True) + EPS)


def _params(sem):
    return pltpu.CompilerParams(dimension_semantics=sem, vmem_limit_bytes=VMEM_LIMIT)


def _mod_kernel(c_ref, w_ref, b_ref, o_ref):
    c = c_ref[...]
    s = (c * _sigmoid(c)).astype(BF16)
    o_ref[0] = _dot(s, w_ref[0].astype(BF16)) + b_ref[0]


def _modulation(cvec, w_mod, b_mod):
    depth, d, n = w_mod.shape
    r = cvec.shape[0]
    tn = 1536
    return pl.pallas_call(
        _mod_kernel,
        grid=(depth, n // tn),
        in_specs=[
            pl.BlockSpec((r, d), lambda l, j: (0, 0)),
            pl.BlockSpec((1, d, tn), lambda l, j: (l, 0, j)),
            pl.BlockSpec((1, 1, tn), lambda l, j: (l, 0, j)),
        ],
        out_specs=pl.BlockSpec((1, r, tn), lambda l, j: (l, 0, j)),
        out_shape=jax.ShapeDtypeStruct((depth, r, n), F32),
        compiler_params=_params(("arbitrary", "arbitrary")),
        name="modulation",
    )(cvec, w_mod, b_mod.reshape(depth, 1, n))


def _nm_matmul_kernel(x_ref, g_ref, modl_ref, modc_ref, w_ref, o_ref, h_ref, *, tm, k_shift, k_scale):
    @pl.when(pl.program_id(2) == 0)
    def _():
        y = _rms(x_ref[0]) * g_ref[...]
        row = pl.program_id(1) * tm + lax.broadcasted_iota(jnp.int32, (tm, 1), 0)
        is_ctx = row < CTX_LEN
        scale = jnp.where(is_ctx, modc_ref[0, k_scale:k_scale + 1, :], modl_ref[0, k_scale:k_scale + 1, :])
        shift = jnp.where(is_ctx, modc_ref[0, k_shift:k_shift + 1, :], modl_ref[0, k_shift:k_shift + 1, :])
        h_ref[...] = (y * (1.0 + scale) + shift).astype(BF16)

    o_ref[0] = _dot(h_ref[...], w_ref[...]).astype(o_ref.dtype)


def _nm_matmul(x, g, modl, modc, w, *, k_shift, k_scale, tm, tn):
    b, t, d = x.shape
    n = w.shape[1]
    return pl.pallas_call(
        functools.partial(_nm_matmul_kernel, tm=tm, k_shift=k_shift, k_scale=k_scale),
        grid=(b, t // tm, n // tn),
        in_specs=[
            pl.BlockSpec((1, tm, d), lambda i, j, k: (i, j, 0)),
            pl.BlockSpec((1, d), lambda i, j, k: (0, 0)),
            pl.BlockSpec((1, 6, d), lambda i, j, k: (i, 0, 0)),
            pl.BlockSpec((1, 6, d), lambda i, j, k: (0, 0, 0)),
            pl.BlockSpec((d, tn), lambda i, j, k: (0, k)),
        ],
        out_specs=pl.BlockSpec((1, tm, tn), lambda i, j, k: (i, j, k)),
        out_shape=jax.ShapeDtypeStruct((b, t, n), BF16),
        scratch_shapes=[pltpu.VMEM((tm, d), BF16)],
        compiler_params=_params(("arbitrary", "arbitrary", "arbitrary")),
        name="norm_mod_matmul",
    )(x, g, modl, modc, w)


def _s5_prep_kernel(lr_ref, lc_ref, ls_ref, bt_ref, ct_ref, t_ref, s_ref, c_ref, a_ref, kap_ref):
    tc = S5_TC
    p = S5_STATE
    lane = lax.broadcasted_iota(jnp.int32, (1, tc), 1).astype(F32)
    sub = lax.broadcasted_iota(jnp.int32, (tc, 1), 0).astype(F32)
    s_idx = lax.broadcasted_iota(jnp.int32, (tc, tc), 0)
    t_idx = lax.broadcasted_iota(jnp.int32, (tc, tc), 1)
    for d in range(2):
        step = jnp.exp(ls_ref[0, d:d + 1, :])
        lre = lr_ref[0, d, 0:1, :]
        lim = lr_ref[0, d, 1:2, :]
        zr = lre * step
        zi = lim * step
        mag = jnp.exp(zr)
        nr = mag * jnp.cos(zi) - 1.0
        ni = mag * jnp.sin(zi)
        den = lre * lre + lim * lim
        cr = (nr * lre + ni * lim) / den
        ci = (ni * lre - nr * lim) / den
        btr = bt_ref[0, d, 0]
        bti = bt_ref[0, d, 1]
        bbr = cr * btr - ci * bti
        bbi = cr * bti + ci * btr
        ma = jnp.exp(zr * tc)
        ar = ma * jnp.cos(zi * tc)
        ai = ma * jnp.sin(zi * tc)
        a_ref[0, 0:1, d * 2 * p:d * 2 * p + p] = ar
        a_ref[0, 0:1, d * 2 * p + p:(d + 1) * 2 * p] = ar
        a_ref[0, 1:2, d * 2 * p:d * 2 * p + p] = -ai
        a_ref[0, 1:2, d * 2 * p + p:(d + 1) * 2 * p] = ai
        zrc = lc_ref[0, d, :, 0:1] * step
        zic = lc_ref[0, d, :, 1:2] * step

        def powers(tau):
            m = jnp.exp(zrc * tau)
            ang = zic * tau
            return m * jnp.cos(ang), m * jnp.sin(ang)

        if d == 0:
            pkr, pki = powers(lane)
            pcr, pci = powers(lane + 1.0)
            es = (tc - 1.0) - sub
        else:
            pkr, pki = powers(jnp.where(lane == 0.0, 0.0, tc - lane))
            pcr, pci = powers(tc - lane)
            es = sub
        ctr = ct_ref[0, d, 0]
        cti = ct_ref[0, d, 1]
        for co in range(S5_GROUP_CH):
            ccr = ctr[:, co:co + 1]
            cci = cti[:, co:co + 1]
            kap_ref[d, co] = (_dot_exact(bbr, ccr * pkr - cci * pki)
                              - _dot_exact(bbi, ccr * pki + cci * pkr))
            cols = slice(co * tc, (co + 1) * tc)
            c_ref[0, d * 2 * p:d * 2 * p + p, cols] = (ccr * pcr - cci * pci).astype(BF16)
            c_ref[0, d * 2 * p + p:(d + 1) * 2 * p, cols] = (-(ccr * pci + cci * pcr)).astype(BF16)
        me = jnp.exp(zr * es)
        er = me * jnp.cos(zi * es)
        ei = me * jnp.sin(zi * es)
        for cin in range(S5_GROUP_CH):
            br = bbr[cin:cin + 1, :]
            bi = bbi[cin:cin + 1, :]
            rows = slice(cin * tc, (cin + 1) * tc)
            s_ref[0, rows, d * 2 * p:d * 2 * p + p] = (er * br - ei * bi).astype(BF16)
            s_ref[0, rows, d * 2 * p + p:(d + 1) * 2 * p] = (er * bi + ei * br).astype(BF16)

    for co in range(S5_GROUP_CH):
        cols = slice(co * tc, (co + 1) * tc)

        def toeplitz(i, carry):
            cins = [i * 4 + j for j in range(4)]
            kr = [kap_ref[1, co, pl.ds(cin, 1), :] for cin in cins]
            kf = [kap_ref[0, co, pl.ds(cin, 1), :] + jnp.where(lane == 0.0, r, 0.0)
                  for cin, r in zip(cins, kr)]
            mixed = [jnp.where(t_idx + s_idx < tc, jnp.broadcast_to(f, (tc, tc)), jnp.broadcast_to(r, (tc, tc)))
                     for f, r in zip(kf, kr)]
            blk = [pltpu.roll(m, 0, 1, stride=1, stride_axis=0).astype(BF16) for m in mixed]
            for cin, b in zip(cins, blk):
                t_ref[0, pl.ds(pl.multiple_of(cin * tc, tc), tc), cols] = b
            return carry

        lax.fori_loop(0, S5_GROUP_CH // 4, toeplitz, 0)


def _s5_prep(lam_re, lam_im, b_re, b_im, c_re, c_im, log_step):
    g = S5_GROUPS
    lam = jnp.stack([lam_re, lam_im], axis=-2).astype(F32).transpose(1, 0, 2, 3)
    lam_c = lam.transpose(0, 1, 3, 2)
    ls = log_step.astype(F32).T[:, :, None]
    bt = jnp.stack([b_re, b_im], axis=2).astype(F32).transpose(1, 0, 2, 4, 3)
    ct = jnp.stack([c_re, c_im], axis=2).astype(F32).transpose(1, 0, 2, 4, 3)
    blk = lambda a: pl.BlockSpec((1,) + a.shape[1:], lambda i: (i,) + (0,) * (a.ndim - 1))
    return pl.pallas_call(
        _s5_prep_kernel,
        grid=(g,),
        in_specs=[blk(lam), blk(lam_c), blk(ls), blk(bt), blk(ct)],
        out_specs=[
            pl.BlockSpec((1, S5_K, S5_K), lambda i: (i, 0, 0)),
            pl.BlockSpec((1, S5_K, 4 * S5_STATE), lambda i: (i, 0, 0)),
            pl.BlockSpec((1, 4 * S5_STATE, S5_K), lambda i: (i, 0, 0)),
            pl.BlockSpec((1, 2, 4 * S5_STATE), lambda i: (i, 0, 0)),
        ],
        out_shape=[
            jax.ShapeDtypeStruct((g, S5_K, S5_K), BF16),
            jax.ShapeDtypeStruct((g, S5_K, 4 * S5_STATE), BF16),
            jax.ShapeDtypeStruct((g, 4 * S5_STATE, S5_K), BF16),
            jax.ShapeDtypeStruct((g, 2, 4 * S5_STATE), F32),
        ],
        scratch_shapes=[pltpu.VMEM((2, S5_GROUP_CH, S5_GROUP_CH, S5_TC), F32)],
        compiler_params=_params(("arbitrary",)),
        name="s5_prep",
    )(lam, lam_c, ls, bt, ct)


def _s5_in_kernel(p_ref, o_ref):
    for j in range(S5_NC):
        z = p_ref[0, j * S5_TC:(j + 1) * S5_TC, :].astype(F32).T.astype(BF16)
        for g in range(S5_GROUPS):
            o_ref[g, j, 0] = z[g * S5_GROUP_CH:(g + 1) * S5_GROUP_CH, :]


def _s5_out_kernel(y_ref, o_ref):
    for j in range(S5_NC):
        z = jnp.concatenate([y_ref[g, j, 0] for g in range(S5_GROUPS)], axis=0)
        o_ref[0, j * S5_TC:(j + 1) * S5_TC, :] = z.T


def _s5_scan_kernel(u_ref, t_ref, s_ref, c_ref, a_ref, y_ref, loc_ref, hin_ref, *, bsz, nch):
    w = 2 * S5_STATE
    u = u_ref[0]
    loc_ref[...] = _dot(u, s_ref[0])
    a1f, a2f = a_ref[0, 0:1, 0:w], a_ref[0, 1:2, 0:w]
    a1r, a2r = a_ref[0, 0:1, w:2 * w], a_ref[0, 1:2, w:2 * w]
    nctx = CTX_LEN // S5_TC

    def body(i, carry):
        hf, hr = carry
        kr = jnp.where(i < nctx, nctx - 1 - i, nch + nctx - 1 - i)
        rf = pl.ds(pl.multiple_of(i * bsz, bsz), bsz)
        rr = pl.ds(pl.multiple_of(kr * bsz, bsz), bsz)
        hin_ref[rf, 0:w] = hf
        hin_ref[rr, w:2 * w] = hr
        hf = a1f * hf + a2f * pltpu.roll(hf, S5_STATE, 1) + loc_ref[rf, 0:w]
        hr = a1r * hr + a2r * pltpu.roll(hr, S5_STATE, 1) + loc_ref[rr, w:2 * w]
        return hf, hr

    zero = jnp.zeros((bsz, w), F32)
    lax.fori_loop(0, nch, body, (zero, zero))
    y_ref[0] = _dot(u, t_ref[0]) + _dot(hin_ref[...].astype(BF16), c_ref[0])


def _s5_scan(p, tmat, smat, cmat, amat):
    b, t, _ = p.shape
    nch = t // S5_TC
    m = nch * b
    g = S5_GROUPS
    chunked = pl.BlockSpec((g, S5_NC, 1, S5_GROUP_CH, S5_TC), lambda i, j: (0, j, i, 0, 0))
    tokens = pl.BlockSpec((1, S5_NC * S5_TC, S5_CH), lambda i, j: (i, j, 0))
    ug = pl.pallas_call(
        _s5_in_kernel,
        grid=(b, nch // S5_NC),
        in_specs=[tokens],
        out_specs=chunked,
        out_shape=jax.ShapeDtypeStruct((g, nch, b, S5_GROUP_CH, S5_TC), BF16),
        compiler_params=_params(("arbitrary", "arbitrary")),
        name="s5_to_chunks",
    )(p)
    yg = pl.pallas_call(
        functools.partial(_s5_scan_kernel, bsz=b, nch=nch),
        grid=(g,),
        in_specs=[
            pl.BlockSpec((1, m, S5_K), lambda i: (i, 0, 0)),
            pl.BlockSpec((1, S5_K, S5_K), lambda i: (i, 0, 0)),
            pl.BlockSpec((1, S5_K, 4 * S5_STATE), lambda i: (i, 0, 0)),
            pl.BlockSpec((1, 4 * S5_STATE, S5_K), lambda i: (i, 0, 0)),
            pl.BlockSpec((1, 2, 4 * S5_STATE), lambda i: (i, 0, 0)),
        ],
        out_specs=pl.BlockSpec((1, m, S5_K), lambda i: (i, 0, 0)),
        out_shape=jax.ShapeDtypeStruct((g, m, S5_K), F32),
        scratch_shapes=[pltpu.VMEM((m, 4 * S5_STATE), F32), pltpu.VMEM((m, 4 * S5_STATE), F32)],
        compiler_params=_params(("arbitrary",)),
        name="s5_scan",
    )(ug.reshape(g, m, S5_K), tmat, smat, cmat, amat)
    return pl.pallas_call(
        _s5_out_kernel,
        grid=(b, nch // S5_NC),
        in_specs=[chunked],
        out_specs=tokens,
        out_shape=jax.ShapeDtypeStruct((b, t, S5_CH), F32),
        compiler_params=_params(("arbitrary", "arbitrary")),
        name="s5_from_chunks",
    )(yg.reshape(g, nch, b, S5_GROUP_CH, S5_TC))


def _na_block_geometry(rows):
    geo = []
    for r0 in (0, NA_QROWS, rows - NA_QROWS):
        geo.append((r0, min(max(r0 - NA_KH // 2, 0), rows - NA_WROWS)))
    return geo


def _na_fill_bias(rpb_ref, bt_ref, rows):
    w = GRID_W
    qc = lax.broadcasted_iota(jnp.int32, (w, w), 0)
    kc = lax.broadcasted_iota(jnp.int32, (w, w), 1)
    ws = jnp.clip(qc - NA_KW // 2, 0, w - NA_KW)
    in_win = (kc >= ws) & (kc < ws + NA_KW)
    neg = jnp.full((w, w), NEG_INF, F32)
    for h in range(NA_HEADS):
        cm = []
        for dr in range(2 * NA_KH - 1):
            v = jnp.broadcast_to(rpb_ref[h, dr:dr + 1, :], (w, 128))
            v = pltpu.roll(v, 128 - (NA_KW - 1), 1, stride=1, stride_axis=0)
            cm.append(jnp.where(in_win, v[:, :w], NEG_INF))
        for kind, (r0, w0) in enumerate(_na_block_geometry(rows)):
            for qr in range(NA_QROWS):
                start = min(max(r0 + qr - NA_KH // 2, 0), rows - NA_KH)
                for kr in range(0, NA_WROWS, 2):
                    pair = []
                    for k in (kr, kr + 1):
                        inside = start <= w0 + k < start + NA_KH
                        pair.append(cm[(w0 + k) - (r0 + qr) + NA_KH - 1] if inside else neg)
                    bt_ref[kind, h, qr * w:(qr + 1) * w, kr * w:(kr + 2) * w] = jnp.concatenate(pair, axis=1)


def _na_kernel(q_ref, k_ref, v_ref, rpb_ref, o_ref, bt_ref, *, rows):
    qn = NA_QROWS * GRID_W
    kn = NA_WROWS * GRID_W
    nblk = rows // NA_QROWS
    scale = HEAD_DIM ** -0.5

    @pl.when(pl.program_id(0) == 0)
    def _():
        _na_fill_bias(rpb_ref, bt_ref, rows)

    for h in range(NA_HEADS):
        hs = slice(h * HEAD_DIM, (h + 1) * HEAD_DIM)
        kc = k_ref[0, 0:CTX_LEN, hs]
        vc = v_ref[0, 0:CTX_LEN, hs]
        s = _dot_nt(q_ref[0, 0:CTX_LEN, hs] * scale, kc)
        p = jnp.exp(s - jnp.max(s, axis=-1, keepdims=True))
        o = _dot(p.astype(BF16), vc) / jnp.sum(p, axis=-1, keepdims=True)
        o_ref[0, 0:CTX_LEN, hs] = o.astype(o_ref.dtype)

        def block(i, carry):
            w0 = jnp.clip(i * NA_QROWS - NA_KH // 2, 0, rows - NA_WROWS)
            kind = jnp.where(i > 0, 1, 0) + jnp.where(i == nblk - 1, 1, 0)
            qrows = pl.ds(pl.multiple_of(CTX_LEN + i * qn, qn), qn)
            krows = pl.ds(pl.multiple_of(CTX_LEN + w0 * GRID_W, GRID_W), kn)
            qb = q_ref[0, qrows, hs] * scale
            s_nb = _dot_nt(qb, k_ref[0, krows, hs]) + bt_ref[kind, h]
            s_cx = _dot_nt(qb, kc)
            mx = jnp.maximum(jnp.max(s_nb, axis=-1, keepdims=True), jnp.max(s_cx, axis=-1, keepdims=True))
            p_nb = jnp.exp(s_nb - mx)
            p_cx = jnp.exp(s_cx - mx)
            den = jnp.sum(p_nb, axis=-1, keepdims=True) + jnp.sum(p_cx, axis=-1, keepdims=True)
            ob = (_dot(p_nb.astype(BF16), v_ref[0, krows, hs]) + _dot(p_cx.astype(BF16), vc)) / den
            o_ref[0, qrows, hs] = ob.astype(o_ref.dtype)
            return carry

        lax.fori_loop(0, nblk, block, 0, unroll=2)


def _na_attention(p, rpb):
    b, t, _ = p.shape
    rows = (t - CTX_LEN) // GRID_W
    rpb_p = jnp.pad(rpb.astype(F32), ((0, 0), (0, 1), (0, 128 - rpb.shape[2])))
    cb = lambda c: (lambda i: (i, 0, c // 256))
    return pl.pallas_call(
        functools.partial(_na_kernel, rows=rows),
        grid=(b,),
        in_specs=[
            pl.BlockSpec((1, t, 256), cb(COL_NAQ)),
            pl.BlockSpec((1, t, 256), cb(COL_NAK)),
            pl.BlockSpec((1, t, 256), cb(COL_NAV)),
            pl.BlockSpec(rpb_p.shape, lambda i: (0, 0, 0)),
        ],
        out_specs=pl.BlockSpec((1, t, 256), lambda i: (i, 0, 0)),
        out_shape=jax.ShapeDtypeStruct((b, t, 256), BF16),
        scratch_shapes=[pltpu.VMEM((3, NA_HEADS, NA_QROWS * GRID_W, NA_WROWS * GRID_W), F32)],
        compiler_params=_params(("arbitrary",)),
        name="na_attention",
    )(p, p, p, rpb_p)


def _log_sigmoid(x):
    return jnp.minimum(x, 0.0) - jnp.log(1.0 + jnp.exp(-jnp.abs(x)))


def _gla_kernel(q_ref, k_ref, v_ref, r_ref, gg_ref, w2_ref, bg_ref, gn_ref, o_ref,
                la_ref, acc_ref, qd_ref, ke_ref, el_ref, st_ref, *, t):
    c = GLA_CHUNK
    ng = GLA_GROUP
    gc = ng * c
    hd = GLA_HEADS * GLA_DK
    ngrp = t // gc
    scale = GLA_DK ** -0.5
    gi = lax.broadcasted_iota(jnp.int32, (gc, gc), 0)
    gj = lax.broadcasted_iota(jnp.int32, (gc, gc), 1)
    same_chunk = (gi // c) == (gj // c)
    same_head = (lax.broadcasted_iota(jnp.int32, (hd, hd), 0) // GLA_DK
                 == lax.broadcasted_iota(jnp.int32, (hd, hd), 1) // GLA_DK)
    row_i = lax.broadcasted_iota(jnp.int32, (c, hd), 0)
    col_j = lax.broadcasted_iota(jnp.int32, (c, hd), 1) % c
    ones4 = same_chunk.astype(BF16)
    for d in range(2):
        x = _dot(gg_ref[0], w2_ref[d]) + bg_ref[d:d + 1, :]
        la_ref[...] = _log_sigmoid(x) * (1.0 / GLA_TAU)
        if d == 0:
            tri4 = (same_chunk & ((gj % c) <= (gi % c))).astype(BF16)
            keep = col_j <= row_i
        else:
            tri4 = (same_chunk & ((gj % c) >= (gi % c))).astype(BF16)
            keep = col_j >= row_i

        def intra(sg, carry):
            rows4 = pl.ds(pl.multiple_of(sg * gc, gc), gc)
            la = la_ref[rows4, :]
            hi = la.astype(BF16)
            lo = (la - hi.astype(F32)).astype(BF16)
            bc = _dot(tri4, hi) + _dot(tri4, lo)
            bl = _dot(ones4, hi) + _dot(ones4, lo)
            qd = (q_ref[0, rows4, :].astype(F32) * scale * jnp.exp(bc)).astype(BF16)
            kf = k_ref[0, rows4, :].astype(F32)
            kd = (kf * jnp.exp(-bc)).astype(BF16)
            vv = v_ref[0, rows4, :]
            qd_ref[rows4, :] = qd
            ke_ref[rows4, :] = (kf * jnp.exp(bl - bc)).astype(BF16)
            el = jnp.exp(bl)
            for g in range(ng):
                el_ref[sg * ng + g] = el[g * c:g * c + 1, :]
            kbd = [jnp.where(same_head, jnp.concatenate([kd[g * c:(g + 1) * c]] * GLA_HEADS, axis=0), 0.0)
                   for g in range(ng)]
            sc = [_dot_nt(qd[g * c:(g + 1) * c], kbd[g]) for g in range(ng)]
            a = [jnp.where(keep, s, 0.0).astype(BF16) for s in sc]
            vbd = [jnp.where(same_head, jnp.concatenate([vv[g * c:(g + 1) * c]] * GLA_HEADS, axis=0), 0.0)
                   for g in range(ng)]
            o = jnp.concatenate([_dot(a[g], vbd[g]) for g in range(ng)], axis=0)
            if d == 0:
                acc_ref[rows4, :] = o
            else:
                acc_ref[rows4, :] += o
            return carry

        lax.fori_loop(0, ngrp, intra, 0)
        st_ref[...] = jnp.zeros_like(st_ref)

        def inter(sg, carry):
            if d == 0:
                grp, visit = sg, list(range(ng))
            else:
                grp, visit = jnp.where(sg == 0, 0, ngrp - sg), list(range(ng - 1, -1, -1))
            rows = [pl.ds(pl.multiple_of((grp * ng + g) * c, c), c) for g in visit]
            ds = [_dot_tn(v_ref[0, r, :], ke_ref[r, :]) for r in rows]
            st = st_ref[...]
            sprev = []
            for i, g in enumerate(visit):
                sprev.append(st.astype(BF16))
                st = st * el_ref[grp * ng + g] + jnp.where(same_head, ds[i], 0.0)
            st_ref[...] = st
            oi = [_dot_nt(qd_ref[r, :], sprev[i]) for i, r in enumerate(rows)]
            for i, r in enumerate(rows):
                acc_ref[r, :] += oi[i]
            return carry

        lax.fori_loop(0, ngrp, inter, 0)

    gn = gn_ref[...]
    for h in range(GLA_HEADS):
        hs = slice(h * GLA_DK, (h + 1) * GLA_DK)
        r = r_ref[0, :, hs].astype(F32)
        o_ref[0, :, hs] = (_rms(acc_ref[:, hs]) * gn * (r * _sigmoid(r))).astype(o_ref.dtype)


def _gla_attention(p, w2cat, b_gate, g_norm):
    b, t, _ = p.shape
    hd = GLA_HEADS * GLA_DK
    cb = lambda c: (lambda i: (i, 0, c // 256))
    return pl.pallas_call(
        functools.partial(_gla_kernel, t=t),
        grid=(b,),
        in_specs=[
            pl.BlockSpec((1, t, 256), cb(COL_GQ)),
            pl.BlockSpec((1, t, 256), cb(COL_GK)),
            pl.BlockSpec((1, t, 256), cb(COL_GV)),
            pl.BlockSpec((1, t, 256), cb(COL_GR)),
            pl.BlockSpec((1, t, 128), lambda i: (i, 0, COL_GG // 128)),
            pl.BlockSpec((2, 128, 256), lambda i: (0, 0, 0)),
            pl.BlockSpec((2, 256), lambda i: (0, 0)),
            pl.BlockSpec((1, GLA_DK), lambda i: (0, 0)),
        ],
        out_specs=pl.BlockSpec((1, t, 256), lambda i: (i, 0, 0)),
        out_shape=jax.ShapeDtypeStruct((b, t, 256), BF16),
        scratch_shapes=[
            pltpu.VMEM((t, hd), F32),
            pltpu.VMEM((t, hd), F32),
            pltpu.VMEM((t, hd), BF16),
            pltpu.VMEM((t, hd), BF16),
            pltpu.VMEM((t // GLA_CHUNK, 1, hd), F32),
            pltpu.VMEM((hd, hd), F32),
        ],
        compiler_params=_params(("arbitrary",)),
        name="gla_attention",
    )(p, p, p, p, p, w2cat, b_gate, g_norm)


def _rope_tables(seq):
    pos = np.arange(seq)
    half = HEAD_DIM // 2
    inv_freq = ROPE_BASE ** (-np.arange(0, half, 2, dtype=np.float32) / half)
    cos_l, sin_l = [], []
    for p_axis in (pos // GRID_W, pos % GRID_W):
        ang = p_axis.astype(np.float32)[:, None] * inv_freq[None, :]
        cos_l += [np.cos(ang), np.cos(ang)]
        sin_l += [-np.sin(ang), np.sin(ang)]
    cos = np.concatenate(cos_l, axis=-1).astype(np.float32)
    sin = np.concatenate(sin_l, axis=-1).astype(np.float32)
    return jnp.asarray(np.tile(cos, (1, SWA_HEADS))), jnp.asarray(np.tile(sin, (1, SWA_HEADS)))


def _rope(x, cos, sin):
    n = x.shape[-1]
    lane = lax.broadcasted_iota(jnp.int32, x.shape, 1)
    partner = jnp.where((lane % 32) < 16, pltpu.roll(x, n - 16, 1), pltpu.roll(x, 16, 1))
    return x * cos + partner * sin


def _swa_kernel(q_ref, k_ref, v_ref, cos_ref, sin_ref, sink_ref, o_ref, qr_ref, kr_ref, *, t):
    seq = t - CTX_LEN
    nb = seq // SWA_BLOCK
    grp = SWA_HEADS // SWA_KV_HEADS
    kw = 3 * SWA_BLOCK
    scale = HEAD_DIM ** -0.5
    cos = cos_ref[...]
    sin = sin_ref[...]
    qr_ref[...] = (_rope(q_ref[0, CTX_LEN:, :].astype(F32), cos, sin) * scale).astype(BF16)
    kv_w = SWA_KV_HEADS * HEAD_DIM
    kr_ref[...] = _rope(k_ref[0, CTX_LEN:, :].astype(F32), cos[:, :kv_w], sin[:, :kv_w]).astype(BF16)
    row_c = lax.broadcasted_iota(jnp.int32, (grp * CTX_LEN, 1), 0)
    row_b = lax.broadcasted_iota(jnp.int32, (grp * SWA_BLOCK, 1), 0)
    col_b = lax.broadcasted_iota(jnp.int32, (1, kw), 1)
    for hk in range(SWA_KV_HEADS):
        ks = slice(hk * HEAD_DIM, (hk + 1) * HEAD_DIM)
        qs = [slice((hk * grp + g) * HEAD_DIM, (hk * grp + g + 1) * HEAD_DIM) for g in range(grp)]
        s0 = sink_ref[hk * grp]
        s1 = sink_ref[hk * grp + 1]
        kc = k_ref[0, 0:CTX_LEN, ks]
        vc = v_ref[0, 0:CTX_LEN, ks]
        q2 = jnp.concatenate([q_ref[0, 0:CTX_LEN, qs[g]] for g in range(grp)], axis=0) * scale
        s = _dot_nt(q2, kc)
        sk = jnp.where(row_c < CTX_LEN, s0, s1)
        mx = jnp.maximum(jnp.max(s, axis=-1, keepdims=True), sk)
        p = jnp.exp(s - mx)
        den = jnp.sum(p, axis=-1, keepdims=True) + jnp.exp(sk - mx)
        o = _dot(p.astype(BF16), vc) / den
        for g in range(grp):
            o_ref[0, 0:CTX_LEN, qs[g]] = o[g * CTX_LEN:(g + 1) * CTX_LEN].astype(o_ref.dtype)
        skb = jnp.where(row_b < SWA_BLOCK, s0, s1)

        def block(n, carry):
            ws = jnp.clip(n - 1, 0, nb - 3) * SWA_BLOCK
            qrows = pl.ds(pl.multiple_of(n * SWA_BLOCK, SWA_BLOCK), SWA_BLOCK)
            krows = pl.ds(pl.multiple_of(ws, SWA_BLOCK), kw)
            vrows = pl.ds(pl.multiple_of(CTX_LEN + ws, SWA_BLOCK), kw)
            qb = jnp.concatenate([qr_ref[qrows, qs[g]] for g in range(grp)], axis=0)
            qpos = n * SWA_BLOCK + (row_b % SWA_BLOCK)
            kpos = ws + col_b
            valid = jnp.abs(qpos - kpos) <= SWA_WINDOW
            s_b = jnp.where(valid, _dot_nt(qb, kr_ref[krows, ks]), NEG_INF)
            s_c = _dot_nt(qb, kc)
            mxb = jnp.maximum(jnp.maximum(jnp.max(s_b, axis=-1, keepdims=True),
                                          jnp.max(s_c, axis=-1, keepdims=True)), skb)
            p_b = jnp.exp(s_b - mxb)
            p_c = jnp.exp(s_c - mxb)
            denb = (jnp.sum(p_b, axis=-1, keepdims=True) + jnp.sum(p_c, axis=-1, keepdims=True)
                    + jnp.exp(skb - mxb))
            ob = (_dot(p_b.astype(BF16), v_ref[0, vrows, ks]) + _dot(p_c.astype(BF16), vc)) / denb
            orows = pl.ds(pl.multiple_of(CTX_LEN + n * SWA_BLOCK, SWA_BLOCK), SWA_BLOCK)
            for g in range(grp):
                o_ref[0, orows, qs[g]] = ob[g * SWA_BLOCK:(g + 1) * SWA_BLOCK].astype(o_ref.dtype)
            return carry

        lax.fori_loop(0, nb, block, 0, unroll=2)


def _swa_attention(p, cos, sin, sink):
    b, t, _ = p.shape
    seq = t - CTX_LEN
    return pl.pallas_call(
        functools.partial(_swa_kernel, t=t),
        grid=(b,),
        in_specs=[
            pl.BlockSpec((1, t, 256), lambda i: (i, 0, COL_SQ // 256)),
            pl.BlockSpec((1, t, 128), lambda i: (i, 0, COL_SK // 128)),
            pl.BlockSpec((1, t, 128), lambda i: (i, 0, COL_SV // 128)),
            pl.BlockSpec((seq, 256), lambda i: (0, 0)),
            pl.BlockSpec((seq, 256), lambda i: (0, 0)),
            pl.BlockSpec(memory_space=pltpu.SMEM),
        ],
        out_specs=pl.BlockSpec((1, t, 256), lambda i: (i, 0, 0)),
        out_shape=jax.ShapeDtypeStruct((b, t, 256), BF16),
        scratch_shapes=[pltpu.VMEM((seq, 256), BF16), pltpu.VMEM((seq, 128), BF16)],
        compiler_params=_params(("arbitrary",)),
        name="swa_attention",
    )(p, p, p, cos, sin, sink)


def _out_proj_kernel(u_ref, ys_ref, d_ref, wg_ref, na_ref, gla_ref, swa_ref, wo_ref, g_ref,
                     modl_ref, modc_ref, x_ref, o_ref, *, tm):
    y = d_ref[...] * u_ref[0].astype(F32) + ys_ref[0]
    z = _gelu(y)
    a = z * _sigmoid(_dot(z.astype(BF16), wg_ref[...]))
    acc = _dot(a.astype(BF16), wo_ref[0:256, :])
    acc += _dot(na_ref[0], wo_ref[256:512, :])
    acc += _dot(gla_ref[0], wo_ref[512:768, :])
    acc += _dot(swa_ref[0], wo_ref[768:1024, :])
    row = pl.program_id(1) * tm + lax.broadcasted_iota(jnp.int32, (tm, 1), 0)
    gate = jnp.where(row < CTX_LEN, modc_ref[0, 2:3, :], modl_ref[0, 2:3, :])
    o_ref[0] = x_ref[0] + gate * (_rms(acc) * g_ref[...])


def _out_proj(p, ys, s5_d, w_glu, y_na, y_gla, y_swa, w_out, g_post, modl, modc, x, *, tm):
    b, t, d = x.shape
    tok = lambda w: pl.BlockSpec((1, tm, w), lambda i, j: (i, j, 0))
    full = lambda a: pl.BlockSpec(a.shape, lambda i, j: (0,) * a.ndim)
    return pl.pallas_call(
        functools.partial(_out_proj_kernel, tm=tm),
        grid=(b, t // tm),
        in_specs=[
            tok(256), tok(256), full(s5_d), full(w_glu), tok(256), tok(256), tok(256),
            full(w_out), full(g_post),
            pl.BlockSpec((1, 6, d), lambda i, j: (i, 0, 0)),
            pl.BlockSpec((1, 6, d), lambda i, j: (0, 0, 0)),
            tok(d),
        ],
        out_specs=tok(d),
        out_shape=jax.ShapeDtypeStruct((b, t, d), F32),
        compiler_params=_params(("arbitrary", "arbitrary")),
        name="out_proj",
    )(p, ys, s5_d, w_glu, y_na, y_gla, y_swa, w_out, g_post, modl, modc, x)


def _ffn_down_kernel(gt_ref, vl_ref, prev_ref, next_ref, cw_ref, wd_ref, g_ref, modl_ref, modc_ref, x_ref, o_ref,
                     a_ref, *, tm, t):
    h = FFN_HALO
    row0 = pl.program_id(1) * tm
    w0, w1, w2 = cw_ref[0:1, :], cw_ref[1:2, :], cw_ref[2:3, :]
    g = gt_ref[0].astype(F32)
    gp = pltpu.roll(g, 1, 0)
    gn = pltpu.roll(g, tm - 1, 0)
    a_ref[...] = _gelu((gp * w0 + g * w1 + gn * w2).astype(BF16)) * vl_ref[0]
    loc = lax.broadcasted_iota(jnp.int32, (h, 1), 0)
    has_prev = jnp.where((row0 == 0) | (row0 == CTX_LEN), 0.0, 1.0)
    has_next = jnp.where((row0 + tm == CTX_LEN) | (row0 + tm == t), 0.0, 1.0)
    gp_top = jnp.where(loc == 0, prev_ref[0, h - 1:h, :].astype(F32) * has_prev, gp[0:h])
    top = gp_top * w0 + g[0:h] * w1 + gn[0:h] * w2
    a_ref[0:h, :] = _gelu(top.astype(BF16)) * vl_ref[0, 0:h, :]
    gn_bot = jnp.where(loc == h - 1, next_ref[0, 0:1, :].astype(F32) * has_next, gn[tm - h:tm])
    bot = gp[tm - h:tm] * w0 + g[tm - h:tm] * w1 + gn_bot * w2
    a_ref[tm - h:tm, :] = _gelu(bot.astype(BF16)) * vl_ref[0, tm - h:tm, :]
    acc = _dot(a_ref[...], wd_ref[...])
    gate = jnp.where(row0 < CTX_LEN, modc_ref[0, 5:6, :], modl_ref[0, 5:6, :])
    o_ref[0] = x_ref[0] + gate * (_rms(acc) * g_ref[...])


def _ffn_down(gv, conv_w, w_down, g_post, modl, modc, x, *, tm):
    b, t, d = x.shape
    assert CTX_LEN % tm == 0 and tm > 2 * FFN_HALO
    nbh = tm // FFN_HALO
    lasth = t // FFN_HALO - 1
    return pl.pallas_call(
        functools.partial(_ffn_down_kernel, tm=tm, t=t),
        grid=(b, t // tm),
        in_specs=[
            pl.BlockSpec((1, tm, D_FF), lambda i, j: (i, j, 0)),
            pl.BlockSpec((1, tm, D_FF), lambda i, j: (i, j, 1)),
            pl.BlockSpec((1, FFN_HALO, D_FF), lambda i, j: (i, jnp.maximum(j * nbh - 1, 0), 0)),
            pl.BlockSpec((1, FFN_HALO, D_FF), lambda i, j: (i, jnp.minimum((j + 1) * nbh, lasth), 0)),
            pl.BlockSpec((3, D_FF), lambda i, j: (0, 0)),
            pl.BlockSpec((D_FF, d), lambda i, j: (0, 0)),
            pl.BlockSpec((1, d), lambda i, j: (0, 0)),
            pl.BlockSpec((1, 6, d), lambda i, j: (i, 0, 0)),
            pl.BlockSpec((1, 6, d), lambda i, j: (0, 0, 0)),
            pl.BlockSpec((1, tm, d), lambda i, j: (i, j, 0)),
        ],
        out_specs=pl.BlockSpec((1, tm, d), lambda i, j: (i, j, 0)),
        out_shape=jax.ShapeDtypeStruct((b, t, d), F32),
        scratch_shapes=[pltpu.VMEM((tm, D_FF), BF16)],
        compiler_params=_params(("arbitrary", "arbitrary")),
        name="ffn_down",
    )(gv, gv, gv, gv, conv_w, w_down, g_post, modl, modc, x)


def _reorder_w_in(w_in):
    sizes = (256, 256, 256, 256, 256, 256, 256, 16, 16, 256, 256, 128, 128)
    offs = np.concatenate([[0], np.cumsum(sizes)])
    seg = lambda i: w_in[..., offs[i]:offs[i + 1]]
    pad = jnp.zeros(w_in.shape[:-1] + (P_WIDTH - COL_GG - 2 * GLA_RANK,), w_in.dtype)
    return jnp.concatenate([seg(0), seg(1), seg(2), seg(3), seg(4), seg(5), seg(6), seg(9),
                            seg(10), seg(11), seg(12), seg(7), seg(8), pad], axis=-1)


def kernel(x, c, ctx, c_ctx, w_mod, b_mod, g_pre_mix, g_post_mix, g_pre_ffn, g_post_ffn, w_in, w_out, s5_lam_re, s5_lam_im, s5_b_re, s5_b_im, s5_c_re, s5_c_im, s5_log_step, s5_d, s5_w_glu, na_rpb, gla_w_gate2, gla_b_gate, gla_g_norm, swa_sink, ffn_w_up, ffn_conv, ffn_w_down):
    bsz, seq, d = x.shape
    depth = w_mod.shape[0]
    t = CTX_LEN + seq
    tm = 768

    cvec = jnp.concatenate([c, c_ctx[None, :]], axis=0)
    cvec = jnp.pad(cvec, ((0, (-cvec.shape[0]) % 8), (0, 0)))
    mod = _modulation(cvec, w_mod, b_mod).reshape(depth, cvec.shape[0], 6, d)

    w_in_r = _reorder_w_in(w_in).astype(BF16)
    w_out_b = w_out.astype(BF16)
    w_glu_b = s5_w_glu.astype(BF16)
    w_up_b = ffn_w_up.astype(BF16)
    w_down_b = ffn_w_down.astype(BF16)
    w2 = jnp.zeros((depth, 2, 128, GLA_HEADS * GLA_DK), F32)
    w2 = w2.at[:, 0, 0:GLA_RANK].set(gla_w_gate2[:, 0]).at[:, 1, GLA_RANK:2 * GLA_RANK].set(gla_w_gate2[:, 1])
    w2 = w2.astype(BF16)
    cos, sin = _rope_tables(seq)

    xc = jnp.concatenate([ctx, x], axis=1)
    for l in range(depth):
        modl = mod[l, :bsz]
        modc = mod[l, bsz:bsz + 1]
        p = _nm_matmul(xc, g_pre_mix[l][None], modl, modc, w_in_r[l], k_shift=0, k_scale=1, tm=tm, tn=P_WIDTH)
        tabs = _s5_prep(s5_lam_re[l], s5_lam_im[l], s5_b_re[l], s5_b_im[l], s5_c_re[l], s5_c_im[l], s5_log_step[l])
        ys = _s5_scan(p, *tabs)
        y_na = _na_attention(p, na_rpb[l])
        y_gla = _gla_attention(p, w2[l], gla_b_gate[l], gla_g_norm[l][None])
        y_swa = _swa_attention(p, cos, sin, swa_sink[l])
        xc = _out_proj(p, ys, s5_d[l][None], w_glu_b[l], y_na, y_gla, y_swa, w_out_b[l],
                       g_post_mix[l][None], modl, modc, xc, tm=tm)
        gv = _nm_matmul(xc, g_pre_ffn[l][None], modl, modc, w_up_b[l], k_shift=3, k_scale=4, tm=tm, tn=D_FF)
        xc = _ffn_down(gv, ffn_conv[l], w_down_b[l], g_post_ffn[l][None], modl, modc, xc, tm=256)
    return xc[:, CTX_LEN:]
```

```python
import functools
import math

import numpy as np
import jax
import jax.numpy as jnp
from jax import lax
from jax.experimental import pallas as pl
from jax.experimental.pallas import tpu as pltpu

F32 = jnp.float32
BF16 = jnp.bfloat16

D_MODEL = 1024
GRID_W = 64
CTX_LEN = 256
HEAD_DIM = 64
EPS = 1e-6
NEG_INF = -1e30

S5_CH = 256
S5_GROUP_CH = 16
S5_GROUPS = S5_CH // S5_GROUP_CH
S5_STATE = 64
S5_TC = 128
S5_K = S5_TC * S5_GROUP_CH
S5_NC = 6

NA_HEADS = 4
NA_KH = 8
NA_KW = 16
NA_QROWS = 4
NA_WROWS = NA_QROWS + NA_KH

GLA_HEADS = 4
GLA_DK = 64
GLA_RANK = 16
GLA_TAU = 16.0
GLA_CHUNK = 64
GLA_GROUP = CTX_LEN // GLA_CHUNK

SWA_HEADS = 4
SWA_KV_HEADS = 2
SWA_WINDOW = 128
SWA_BLOCK = 128
ROPE_BASE = 10000.0

D_FF = 2816
FFN_HALO = 16

COL_S5, COL_NAQ, COL_NAK, COL_NAV = 0, 256, 512, 768
COL_GQ, COL_GK, COL_GV, COL_GR = 1024, 1280, 1536, 1792
COL_SQ, COL_SK, COL_SV, COL_GG = 2048, 2304, 2432, 2560
P_WIDTH = 2688

VMEM_LIMIT = 56 * 1024 * 1024


def _dot(a, b):
    return jnp.dot(a, b, preferred_element_type=F32)


def _dot_exact(a, b):
    return jnp.dot(a, b, preferred_element_type=F32, precision=lax.Precision.HIGHEST)


def _dot_nt(a, b):
    return lax.dot_general(a, b, (((1,), (1,)), ((), ())), preferred_element_type=F32)


def _dot_tn(a, b):
    return lax.dot_general(a, b, (((0,), (0,)), ((), ())), preferred_element_type=F32)


def _gelu(x):
    k = math.sqrt(2.0 / math.pi)
    return x * (0.5 + 0.5 * jnp.tanh(x * (k + (k * 0.044715) * (x * x))))


def _sigmoid(x):
    return 1.0 / (1.0 + jnp.exp(-x))


def _rms(x):
    return x * lax.rsqrt(jnp.mean(x * x, axis=-1, keepdims=True) + EPS)


def _params(sem):
    return pltpu.CompilerParams(dimension_semantics=sem, vmem_limit_bytes=VMEM_LIMIT)


def _mod_kernel(c_ref, w_ref, b_ref, o_ref):
    c = c_ref[...]
    s = (c * _sigmoid(c)).astype(BF16)
    o_ref[0] = _dot(s, w_ref[0].astype(BF16)) + b_ref[0]


def _modulation(cvec, w_mod, b_mod):
    depth, d, n = w_mod.shape
    r = cvec.shape[0]
    tn = 1536
    return pl.pallas_call(
        _mod_kernel,
        grid=(depth, n // tn),
        in_specs=[
            pl.BlockSpec((r, d), lambda l, j: (0, 0)),
            pl.BlockSpec((1, d, tn), lambda l, j: (l, 0, j)),
            pl.BlockSpec((1, 1, tn), lambda l, j: (l, 0, j)),
        ],
        out_specs=pl.BlockSpec((1, r, tn), lambda l, j: (l, 0, j)),
        out_shape=jax.ShapeDtypeStruct((depth, r, n), F32),
        compiler_params=_params(("arbitrary", "arbitrary")),
        name="modulation",
    )(cvec, w_mod, b_mod.reshape(depth, 1, n))


def _nm_matmul_kernel(x_ref, g_ref, modl_ref, modc_ref, w_ref, o_ref, h_ref, *, tm, k_shift, k_scale, ctx_rows):
    @pl.when(pl.program_id(2) == 0)
    def _():
        y = _rms(x_ref[0]) * g_ref[...]
        row = pl.program_id(1) * tm + lax.broadcasted_iota(jnp.int32, (tm, 1), 0)
        is_ctx = row < ctx_rows
        scale = jnp.where(is_ctx, modc_ref[0, k_scale:k_scale + 1, :], modl_ref[0, k_scale:k_scale + 1, :])
        shift = jnp.where(is_ctx, modc_ref[0, k_shift:k_shift + 1, :], modl_ref[0, k_shift:k_shift + 1, :])
        h_ref[...] = (y * (1.0 + scale) + shift).astype(BF16)

    o_ref[0] = _dot(h_ref[...], w_ref[...]).astype(o_ref.dtype)


def _nm_matmul(x, g, modl, modc, w, *, k_shift, k_scale, tm, tn, ctx_rows=CTX_LEN):
    b, t, d = x.shape
    n = w.shape[1]
    return pl.pallas_call(
        functools.partial(_nm_matmul_kernel, tm=tm, k_shift=k_shift, k_scale=k_scale, ctx_rows=ctx_rows),
        grid=(b, t // tm, n // tn),
        in_specs=[
            pl.BlockSpec((1, tm, d), lambda i, j, k: (i, j, 0)),
            pl.BlockSpec((1, d), lambda i, j, k: (0, 0)),
            pl.BlockSpec((1, 6, d), lambda i, j, k: (i, 0, 0)),
            pl.BlockSpec((1, 6, d), lambda i, j, k: (0, 0, 0)),
            pl.BlockSpec((d, tn), lambda i, j, k: (0, k)),
        ],
        out_specs=pl.BlockSpec((1, tm, tn), lambda i, j, k: (i, j, k)),
        out_shape=jax.ShapeDtypeStruct((b, t, n), BF16),
        scratch_shapes=[pltpu.VMEM((tm, d), BF16)],
        compiler_params=_params(("arbitrary", "arbitrary", "arbitrary")),
        name="norm_mod_matmul",
    )(x, g, modl, modc, w)


def _s5_prep_kernel(lr_ref, lc_ref, ls_ref, bt_ref, ct_ref, t_ref, s_ref, c_ref, a_ref, kap_ref):
    tc = S5_TC
    p = S5_STATE
    lane = lax.broadcasted_iota(jnp.int32, (1, tc), 1).astype(F32)
    sub = lax.broadcasted_iota(jnp.int32, (tc, 1), 0).astype(F32)
    s_idx = lax.broadcasted_iota(jnp.int32, (tc, tc), 0)
    t_idx = lax.broadcasted_iota(jnp.int32, (tc, tc), 1)
    for d in range(2):
        step = jnp.exp(ls_ref[0, d:d + 1, :])
        lre = lr_ref[0, d, 0:1, :]
        lim = lr_ref[0, d, 1:2, :]
        zr = lre * step
        zi = lim * step
        mag = jnp.exp(zr)
        nr = mag * jnp.cos(zi) - 1.0
        ni = mag * jnp.sin(zi)
        den = lre * lre + lim * lim
        cr = (nr * lre + ni * lim) / den
        ci = (ni * lre - nr * lim) / den
        btr = bt_ref[0, d, 0]
        bti = bt_ref[0, d, 1]
        bbr = cr * btr - ci * bti
        bbi = cr * bti + ci * btr
        ma = jnp.exp(zr * tc)
        ar = ma * jnp.cos(zi * tc)
        ai = ma * jnp.sin(zi * tc)
        a_ref[0, 0:1, d * 2 * p:d * 2 * p + p] = ar
        a_ref[0, 0:1, d * 2 * p + p:(d + 1) * 2 * p] = ar
        a_ref[0, 1:2, d * 2 * p:d * 2 * p + p] = -ai
        a_ref[0, 1:2, d * 2 * p + p:(d + 1) * 2 * p] = ai
        zrc = lc_ref[0, d, :, 0:1] * step
        zic = lc_ref[0, d, :, 1:2] * step

        def powers(tau):
            m = jnp.exp(zrc * tau)
            ang = zic * tau
            return m * jnp.cos(ang), m * jnp.sin(ang)

        if d == 0:
            pkr, pki = powers(lane)
            pcr, pci = powers(lane + 1.0)
            es = (tc - 1.0) - sub
        else:
            pkr, pki = powers(jnp.where(lane == 0.0, 0.0, tc - lane))
            pcr, pci = powers(tc - lane)
            es = sub
        ctr = ct_ref[0, d, 0]
        cti = ct_ref[0, d, 1]
        for co in range(S5_GROUP_CH):
            ccr = ctr[:, co:co + 1]
            cci = cti[:, co:co + 1]
            kap_ref[d, co] = (_dot_exact(bbr, ccr * pkr - cci * pki)
                              - _dot_exact(bbi, ccr * pki + cci * pkr))
            cols = slice(co * tc, (co + 1) * tc)
            c_ref[0, d * 2 * p:d * 2 * p + p, cols] = (ccr * pcr - cci * pci).astype(BF16)
            c_ref[0, d * 2 * p + p:(d + 1) * 2 * p, cols] = (-(ccr * pci + cci * pcr)).astype(BF16)
        me = jnp.exp(zr * es)
        er = me * jnp.cos(zi * es)
        ei = me * jnp.sin(zi * es)
        for cin in range(S5_GROUP_CH):
            br = bbr[cin:cin + 1, :]
            bi = bbi[cin:cin + 1, :]
            rows = slice(cin * tc, (cin + 1) * tc)
            s_ref[0, rows, d * 2 * p:d * 2 * p + p] = (er * br - ei * bi).astype(BF16)
            s_ref[0, rows, d * 2 * p + p:(d + 1) * 2 * p] = (er * bi + ei * br).astype(BF16)

    for co in range(S5_GROUP_CH):
        cols = slice(co * tc, (co + 1) * tc)

        def toeplitz(i, carry):
            cins = [i * 4 + j for j in range(4)]
            kr = [kap_ref[1, co, pl.ds(cin, 1), :] for cin in cins]
            kf = [kap_ref[0, co, pl.ds(cin, 1), :] + jnp.where(lane == 0.0, r, 0.0)
                  for cin, r in zip(cins, kr)]
            mixed = [jnp.where(t_idx + s_idx < tc, jnp.broadcast_to(f, (tc, tc)), jnp.broadcast_to(r, (tc, tc)))
                     for f, r in zip(kf, kr)]
            blk = [pltpu.roll(m, 0, 1, stride=1, stride_axis=0).astype(BF16) for m in mixed]
            for cin, b in zip(cins, blk):
                t_ref[0, pl.ds(pl.multiple_of(cin * tc, tc), tc), cols] = b
            return carry

        lax.fori_loop(0, S5_GROUP_CH // 4, toeplitz, 0)


def _s5_prep(lam_re, lam_im, b_re, b_im, c_re, c_im, log_step):
    g = S5_GROUPS
    lam = jnp.stack([lam_re, lam_im], axis=-2).astype(F32).transpose(1, 0, 2, 3)
    lam_c = lam.transpose(0, 1, 3, 2)
    ls = log_step.astype(F32).T[:, :, None]
    bt = jnp.stack([b_re, b_im], axis=2).astype(F32).transpose(1, 0, 2, 4, 3)
    ct = jnp.stack([c_re, c_im], axis=2).astype(F32).transpose(1, 0, 2, 4, 3)
    blk = lambda a: pl.BlockSpec((1,) + a.shape[1:], lambda i: (i,) + (0,) * (a.ndim - 1))
    return pl.pallas_call(
        _s5_prep_kernel,
        grid=(g,),
        in_specs=[blk(lam), blk(lam_c), blk(ls), blk(bt), blk(ct)],
        out_specs=[
            pl.BlockSpec((1, S5_K, S5_K), lambda i: (i, 0, 0)),
            pl.BlockSpec((1, S5_K, 4 * S5_STATE), lambda i: (i, 0, 0)),
            pl.BlockSpec((1, 4 * S5_STATE, S5_K), lambda i: (i, 0, 0)),
            pl.BlockSpec((1, 2, 4 * S5_STATE), lambda i: (i, 0, 0)),
        ],
        out_shape=[
            jax.ShapeDtypeStruct((g, S5_K, S5_K), BF16),
            jax.ShapeDtypeStruct((g, S5_K, 4 * S5_STATE), BF16),
            jax.ShapeDtypeStruct((g, 4 * S5_STATE, S5_K), BF16),
            jax.ShapeDtypeStruct((g, 2, 4 * S5_STATE), F32),
        ],
        scratch_shapes=[pltpu.VMEM((2, S5_GROUP_CH, S5_GROUP_CH, S5_TC), F32)],
        compiler_params=_params(("arbitrary",)),
        name="s5_prep",
    )(lam, lam_c, ls, bt, ct)


def _s5_in_kernel(p_ref, o_ref):
    for j in range(S5_NC):
        z = p_ref[0, j * S5_TC:(j + 1) * S5_TC, :].astype(F32).T.astype(BF16)
        for g in range(S5_GROUPS):
            o_ref[g, j, 0] = z[g * S5_GROUP_CH:(g + 1) * S5_GROUP_CH, :]


def _s5_out_kernel(y_ref, o_ref):
    for j in range(S5_NC):
        z = jnp.concatenate([y_ref[g, j, 0] for g in range(S5_GROUPS)], axis=0)
        o_ref[0, j * S5_TC:(j + 1) * S5_TC, :] = z.T


def _s5_scan_kernel(u_ref, t_ref, s_ref, c_ref, a_ref, y_ref, loc_ref, hin_ref, *, bsz, nch):
    w = 2 * S5_STATE
    u = u_ref[0]
    loc_ref[...] = _dot(u, s_ref[0])
    a1f, a2f = a_ref[0, 0:1, 0:w], a_ref[0, 1:2, 0:w]
    a1r, a2r = a_ref[0, 0:1, w:2 * w], a_ref[0, 1:2, w:2 * w]
    nctx = CTX_LEN // S5_TC

    def body(i, carry):
        hf, hr = carry
        kr = jnp.where(i < nctx, nctx - 1 - i, nch + nctx - 1 - i)
        rf = pl.ds(pl.multiple_of(i * bsz, bsz), bsz)
        rr = pl.ds(pl.multiple_of(kr * bsz, bsz), bsz)
        hin_ref[rf, 0:w] = hf
        hin_ref[rr, w:2 * w] = hr
        hf = a1f * hf + a2f * pltpu.roll(hf, S5_STATE, 1) + loc_ref[rf, 0:w]
        hr = a1r * hr + a2r * pltpu.roll(hr, S5_STATE, 1) + loc_ref[rr, w:2 * w]
        return hf, hr

    zero = jnp.zeros((bsz, w), F32)
    lax.fori_loop(0, nch, body, (zero, zero))
    y_ref[0] = _dot(u, t_ref[0]) + _dot(hin_ref[...].astype(BF16), c_ref[0])


def _s5_scan(p, tmat, smat, cmat, amat):
    b, t, _ = p.shape
    nch = t // S5_TC
    m = nch * b
    g = S5_GROUPS
    chunked = pl.BlockSpec((g, S5_NC, 1, S5_GROUP_CH, S5_TC), lambda i, j: (0, j, i, 0, 0))
    tokens = pl.BlockSpec((1, S5_NC * S5_TC, S5_CH), lambda i, j: (i, j, 0))
    ug = pl.pallas_call(
        _s5_in_kernel,
        grid=(b, nch // S5_NC),
        in_specs=[tokens],
        out_specs=chunked,
        out_shape=jax.ShapeDtypeStruct((g, nch, b, S5_GROUP_CH, S5_TC), BF16),
        compiler_params=_params(("arbitrary", "arbitrary")),
        name="s5_to_chunks",
    )(p)
    yg = pl.pallas_call(
        functools.partial(_s5_scan_kernel, bsz=b, nch=nch),
        grid=(g,),
        in_specs=[
            pl.BlockSpec((1, m, S5_K), lambda i: (i, 0, 0)),
            pl.BlockSpec((1, S5_K, S5_K), lambda i: (i, 0, 0)),
            pl.BlockSpec((1, S5_K, 4 * S5_STATE), lambda i: (i, 0, 0)),
            pl.BlockSpec((1, 4 * S5_STATE, S5_K), lambda i: (i, 0, 0)),
            pl.BlockSpec((1, 2, 4 * S5_STATE), lambda i: (i, 0, 0)),
        ],
        out_specs=pl.BlockSpec((1, m, S5_K), lambda i: (i, 0, 0)),
        out_shape=jax.ShapeDtypeStruct((g, m, S5_K), F32),
        scratch_shapes=[pltpu.VMEM((m, 4 * S5_STATE), F32), pltpu.VMEM((m, 4 * S5_STATE), F32)],
        compiler_params=_params(("arbitrary",)),
        name="s5_scan",
    )(ug.reshape(g, m, S5_K), tmat, smat, cmat, amat)
    return pl.pallas_call(
        _s5_out_kernel,
        grid=(b, nch // S5_NC),
        in_specs=[chunked],
        out_specs=tokens,
        out_shape=jax.ShapeDtypeStruct((b, t, S5_CH), F32),
        compiler_params=_params(("arbitrary", "arbitrary")),
        name="s5_from_chunks",
    )(yg.reshape(g, nch, b, S5_GROUP_CH, S5_TC))


def _na_block_geometry(rows):
    geo = []
    for r0 in (0, NA_QROWS, rows - NA_QROWS):
        geo.append((r0, min(max(r0 - NA_KH // 2, 0), rows - NA_WROWS)))
    return geo


def _na_fill_bias(rpb_ref, bt_ref, rows):
    w = GRID_W
    qc = lax.broadcasted_iota(jnp.int32, (w, w), 0)
    kc = lax.broadcasted_iota(jnp.int32, (w, w), 1)
    ws = jnp.clip(qc - NA_KW // 2, 0, w - NA_KW)
    in_win = (kc >= ws) & (kc < ws + NA_KW)
    neg = jnp.full((w, w), NEG_INF, F32)
    for h in range(NA_HEADS):
        cm = []
        for dr in range(2 * NA_KH - 1):
            v = jnp.broadcast_to(rpb_ref[h, dr:dr + 1, :], (w, 128))
            v = pltpu.roll(v, 128 - (NA_KW - 1), 1, stride=1, stride_axis=0)
            cm.append(jnp.where(in_win, v[:, :w], NEG_INF))
        for kind, (r0, w0) in enumerate(_na_block_geometry(rows)):
            for qr in range(NA_QROWS):
                start = min(max(r0 + qr - NA_KH // 2, 0), rows - NA_KH)
                for kr in range(0, NA_WROWS, 2):
                    pair = []
                    for k in (kr, kr + 1):
                        inside = start <= w0 + k < start + NA_KH
                        pair.append(cm[(w0 + k) - (r0 + qr) + NA_KH - 1] if inside else neg)
                    r_lo = (h * NA_QROWS + qr) * w
                    bt_ref[kind, r_lo:r_lo + w, kr * w:(kr + 2) * w] = jnp.concatenate(pair, axis=1)


def _na_kernel(q_ref, k_ref, v_ref, rpb_ref, o_ref, bt_ref, *, rows):
    qn = NA_QROWS * GRID_W
    kn = NA_WROWS * GRID_W
    nblk = rows // NA_QROWS
    scale = HEAD_DIM ** -0.5

    @pl.when(pl.program_id(0) == 0)
    def _():
        _na_fill_bias(rpb_ref, bt_ref, rows)

    lane_head = lax.broadcasted_iota(jnp.int32, (1, NA_HEADS * HEAD_DIM), 1) // HEAD_DIM

    def stack_heads(x):
        return jnp.concatenate([jnp.where(lane_head == h, x, 0.0) for h in range(NA_HEADS)], axis=0)

    def pick_heads(o4, n):
        out = o4[0:n]
        for h in range(1, NA_HEADS):
            out = jnp.where(lane_head == h, o4[h * n:(h + 1) * n], out)
        return out

    kc = k_ref[0, 0:CTX_LEN, :]
    vc = v_ref[0, 0:CTX_LEN, :]
    s = _dot_nt(stack_heads(q_ref[0, 0:CTX_LEN, :] * scale), kc)
    p = jnp.exp(s - jnp.max(s, axis=-1, keepdims=True))
    o = _dot(p.astype(BF16), vc) / jnp.sum(p, axis=-1, keepdims=True)
    o_ref[0, 0:CTX_LEN, :] = pick_heads(o, CTX_LEN).astype(o_ref.dtype)

    def logits(i):
        w0 = jnp.clip(i * NA_QROWS - NA_KH // 2, 0, rows - NA_WROWS)
        kind = jnp.where(i > 0, 1, 0) + jnp.where(i == nblk - 1, 1, 0)
        qrows = pl.ds(pl.multiple_of(CTX_LEN + i * qn, qn), qn)
        krows = pl.ds(pl.multiple_of(CTX_LEN + w0 * GRID_W, GRID_W), kn)
        q4 = stack_heads(q_ref[0, qrows, :] * scale)
        return _dot_nt(q4, k_ref[0, krows, :]), _dot_nt(q4, kc), kind, qrows, krows

    def attend(s_nb, s_cx, kind, qrows, krows):
        p_nb, p_cx, inv = [], [], []
        for h in range(NA_HEADS):
            r = slice(h * qn, (h + 1) * qn)
            a = s_nb[r] + bt_ref[kind, r, :]
            c = s_cx[r]
            mx = jnp.maximum(jnp.max(a, axis=-1, keepdims=True), jnp.max(c, axis=-1, keepdims=True))
            ea = jnp.exp(a - mx)
            ec = jnp.exp(c - mx)
            inv.append(1.0 / (jnp.sum(ea, axis=-1, keepdims=True) + jnp.sum(ec, axis=-1, keepdims=True)))
            p_nb.append(ea.astype(BF16))
            p_cx.append(ec.astype(BF16))
        o4 = (_dot(jnp.concatenate(p_nb, axis=0), v_ref[0, krows, :])
              + _dot(jnp.concatenate(p_cx, axis=0), vc)) * jnp.concatenate(inv, axis=0)
        o_ref[0, qrows, :] = pick_heads(o4, qn).astype(o_ref.dtype)

    def block_pair(j, carry):
        first = logits(2 * j)
        second = logits(2 * j + 1)
        attend(*first)
        attend(*second)
        return carry

    lax.fori_loop(0, nblk // 2, block_pair, 0)


def _na_attention(p, rpb):
    b, t, _ = p.shape
    rows = (t - CTX_LEN) // GRID_W
    rpb_p = jnp.pad(rpb.astype(F32), ((0, 0), (0, 1), (0, 128 - rpb.shape[2])))
    cb = lambda c: (lambda i: (i, 0, c // 256))
    return pl.pallas_call(
        functools.partial(_na_kernel, rows=rows),
        grid=(b,),
        in_specs=[
            pl.BlockSpec((1, t, 256), cb(COL_NAQ)),
            pl.BlockSpec((1, t, 256), cb(COL_NAK)),
            pl.BlockSpec((1, t, 256), cb(COL_NAV)),
            pl.BlockSpec(rpb_p.shape, lambda i: (0, 0, 0)),
        ],
        out_specs=pl.BlockSpec((1, t, 256), lambda i: (i, 0, 0)),
        out_shape=jax.ShapeDtypeStruct((b, t, 256), BF16),
        scratch_shapes=[pltpu.VMEM((3, NA_HEADS * NA_QROWS * GRID_W, NA_WROWS * GRID_W), F32)],
        compiler_params=_params(("arbitrary",)),
        name="na_attention",
    )(p, p, p, rpb_p)


def _log_sigmoid(x):
    return jnp.minimum(x, 0.0) - jnp.log(1.0 + jnp.exp(-jnp.abs(x)))


def _gla_kernel(q_ref, k_ref, v_ref, r_ref, gg_ref, w2_ref, bg_ref, gn_ref, o_ref,
                la_ref, acc_ref, qd_ref, ke_ref, el_ref, st_ref, *, t):
    c = GLA_CHUNK
    ng = GLA_GROUP
    gc = ng * c
    hd = GLA_HEADS * GLA_DK
    ngrp = t // gc
    scale = GLA_DK ** -0.5
    gi = lax.broadcasted_iota(jnp.int32, (gc, gc), 0)
    gj = lax.broadcasted_iota(jnp.int32, (gc, gc), 1)
    same_chunk = (gi // c) == (gj // c)
    same_head = (lax.broadcasted_iota(jnp.int32, (hd, hd), 0) // GLA_DK
                 == lax.broadcasted_iota(jnp.int32, (hd, hd), 1) // GLA_DK)
    row_i = lax.broadcasted_iota(jnp.int32, (c, hd), 0)
    col_j = lax.broadcasted_iota(jnp.int32, (c, hd), 1) % c
    ones4 = same_chunk.astype(BF16)
    for d in range(2):
        x = _dot(gg_ref[0], w2_ref[d]) + bg_ref[d:d + 1, :]
        la_ref[...] = _log_sigmoid(x) * (1.0 / GLA_TAU)
        if d == 0:
            tri4 = (same_chunk & ((gj % c) <= (gi % c))).astype(BF16)
            keep = col_j <= row_i
        else:
            tri4 = (same_chunk & ((gj % c) >= (gi % c))).astype(BF16)
            keep = col_j >= row_i

        def intra(sg, carry):
            rows4 = pl.ds(pl.multiple_of(sg * gc, gc), gc)
            la = la_ref[rows4, :]
            hi = la.astype(BF16)
            lo = (la - hi.astype(F32)).astype(BF16)
            bc = _dot(tri4, hi) + _dot(tri4, lo)
            bl = _dot(ones4, hi) + _dot(ones4, lo)
            qd = (q_ref[0, rows4, :].astype(F32) * scale * jnp.exp(bc)).astype(BF16)
            kf = k_ref[0, rows4, :].astype(F32)
            kd = (kf * jnp.exp(-bc)).astype(BF16)
            vv = v_ref[0, rows4, :]
            qd_ref[rows4, :] = qd
            ke_ref[rows4, :] = (kf * jnp.exp(bl - bc)).astype(BF16)
            el = jnp.exp(bl)
            for g in range(ng):
                el_ref[sg * ng + g] = el[g * c:g * c + 1, :]
            kbd = [jnp.where(same_head, jnp.concatenate([kd[g * c:(g + 1) * c]] * GLA_HEADS, axis=0), 0.0)
                   for g in range(ng)]
            sc = [_dot_nt(qd[g * c:(g + 1) * c], kbd[g]) for g in range(ng)]
            a = [jnp.where(keep, s, 0.0).astype(BF16) for s in sc]
            vbd = [jnp.where(same_head, jnp.concatenate([vv[g * c:(g + 1) * c]] * GLA_HEADS, axis=0), 0.0)
                   for g in range(ng)]
            o = jnp.concatenate([_dot(a[g], vbd[g]) for g in range(ng)], axis=0)
            if d == 0:
                acc_ref[rows4, :] = o
            else:
                acc_ref[rows4, :] += o
            return carry

        lax.fori_loop(0, ngrp, intra, 0)
        st_ref[...] = jnp.zeros_like(st_ref)

        def inter(sg, carry):
            if d == 0:
                grp, visit = sg, list(range(ng))
            else:
                grp, visit = jnp.where(sg == 0, 0, ngrp - sg), list(range(ng - 1, -1, -1))
            rows = [pl.ds(pl.multiple_of((grp * ng + g) * c, c), c) for g in visit]
            ds = [_dot_tn(v_ref[0, r, :], ke_ref[r, :]) for r in rows]
            st = st_ref[...]
            sprev = []
            for i, g in enumerate(visit):
                sprev.append(st.astype(BF16))
                st = st * el_ref[grp * ng + g] + jnp.where(same_head, ds[i], 0.0)
            st_ref[...] = st
            oi = [_dot_nt(qd_ref[r, :], sprev[i]) for i, r in enumerate(rows)]
            for i, r in enumerate(rows):
                acc_ref[r, :] += oi[i]
            return carry

        lax.fori_loop(0, ngrp, inter, 0)

    gn = gn_ref[...]
    for h in range(GLA_HEADS):
        hs = slice(h * GLA_DK, (h + 1) * GLA_DK)
        r = r_ref[0, :, hs].astype(F32)
        o_ref[0, :, hs] = (_rms(acc_ref[:, hs]) * gn * (r * _sigmoid(r))).astype(o_ref.dtype)


def _gla_attention(p, w2cat, b_gate, g_norm):
    b, t, _ = p.shape
    hd = GLA_HEADS * GLA_DK
    cb = lambda c: (lambda i: (i, 0, c // 256))
    return pl.pallas_call(
        functools.partial(_gla_kernel, t=t),
        grid=(b,),
        in_specs=[
            pl.BlockSpec((1, t, 256), cb(COL_GQ)),
            pl.BlockSpec((1, t, 256), cb(COL_GK)),
            pl.BlockSpec((1, t, 256), cb(COL_GV)),
            pl.BlockSpec((1, t, 256), cb(COL_GR)),
            pl.BlockSpec((1, t, 128), lambda i: (i, 0, COL_GG // 128)),
            pl.BlockSpec((2, 128, 256), lambda i: (0, 0, 0)),
            pl.BlockSpec((2, 256), lambda i: (0, 0)),
            pl.BlockSpec((1, GLA_DK), lambda i: (0, 0)),
        ],
        out_specs=pl.BlockSpec((1, t, 256), lambda i: (i, 0, 0)),
        out_shape=jax.ShapeDtypeStruct((b, t, 256), BF16),
        scratch_shapes=[
            pltpu.VMEM((t, hd), F32),
            pltpu.VMEM((t, hd), F32),
            pltpu.VMEM((t, hd), BF16),
            pltpu.VMEM((t, hd), BF16),
            pltpu.VMEM((t // GLA_CHUNK, 1, hd), F32),
            pltpu.VMEM((hd, hd), F32),
        ],
        compiler_params=_params(("arbitrary",)),
        name="gla_attention",
    )(p, p, p, p, p, w2cat, b_gate, g_norm)


def _rope_tables(seq):
    pos = np.arange(seq)
    half = HEAD_DIM // 2
    inv_freq = ROPE_BASE ** (-np.arange(0, half, 2, dtype=np.float32) / half)
    cos_l, sin_l = [], []
    for p_axis in (pos // GRID_W, pos % GRID_W):
        ang = p_axis.astype(np.float32)[:, None] * inv_freq[None, :]
        cos_l += [np.cos(ang), np.cos(ang)]
        sin_l += [-np.sin(ang), np.sin(ang)]
    cos = np.concatenate(cos_l, axis=-1).astype(np.float32)
    sin = np.concatenate(sin_l, axis=-1).astype(np.float32)
    return jnp.asarray(np.tile(cos, (1, SWA_HEADS))), jnp.asarray(np.tile(sin, (1, SWA_HEADS)))


def _rope(x, cos, sin):
    n = x.shape[-1]
    lane = lax.broadcasted_iota(jnp.int32, x.shape, 1)
    partner = jnp.where((lane % 32) < 16, pltpu.roll(x, n - 16, 1), pltpu.roll(x, 16, 1))
    return x * cos + partner * sin


def _swa_kernel(q_ref, k_ref, v_ref, cos_ref, sin_ref, sink_ref, o_ref, qr_ref, kr_ref, *, t):
    seq = t - CTX_LEN
    nb = seq // SWA_BLOCK
    grp = SWA_HEADS // SWA_KV_HEADS
    kw = 3 * SWA_BLOCK
    scale = HEAD_DIM ** -0.5
    kv_w = SWA_KV_HEADS * HEAD_DIM
    lane = lax.broadcasted_iota(jnp.int32, (1, kv_w), 1)

    def to_kv_lanes(x, h):
        slab = x[:, (h // grp) * kv_w:(h // grp + 1) * kv_w]
        if h % grp != h // grp:
            slab = pltpu.roll(slab, HEAD_DIM, 1)
        return jnp.where(lane // HEAD_DIM == h // grp, slab, 0.0)

    def from_kv_lanes(o4, n):
        slabs = []
        for pair in range(SWA_HEADS // grp):
            lo = o4[(2 * pair) * n:(2 * pair + 1) * n]
            hi = o4[(2 * pair + 1) * n:(2 * pair + 2) * n]
            if pair == 0:
                hi = pltpu.roll(hi, HEAD_DIM, 1)
            else:
                lo = pltpu.roll(lo, HEAD_DIM, 1)
            slabs.append(jnp.where(lane < HEAD_DIM, lo, hi))
        return jnp.concatenate(slabs, axis=1)

    def sink_rows(n):
        row = lax.broadcasted_iota(jnp.int32, (SWA_HEADS * n, 1), 0) // n
        sk = jnp.full((SWA_HEADS * n, 1), sink_ref[0], F32)
        for h in range(1, SWA_HEADS):
            sk = jnp.where(row == h, sink_ref[h], sk)
        return sk

    cos = cos_ref[...]
    sin = sin_ref[...]
    qrot = _rope(q_ref[0, CTX_LEN:, :].astype(F32), cos, sin) * scale
    for h in range(SWA_HEADS):
        qr_ref[h] = to_kv_lanes(qrot, h).astype(BF16)
    kr_ref[...] = _rope(k_ref[0, CTX_LEN:, :].astype(F32), cos[:, :kv_w], sin[:, :kv_w]).astype(BF16)
    kc = k_ref[0, 0:CTX_LEN, :]
    vc = v_ref[0, 0:CTX_LEN, :]
    qc = q_ref[0, 0:CTX_LEN, :].astype(F32) * scale
    q4 = jnp.concatenate([to_kv_lanes(qc, h) for h in range(SWA_HEADS)], axis=0).astype(BF16)
    s = _dot_nt(q4, kc)
    sk = sink_rows(CTX_LEN)
    mx = jnp.maximum(jnp.max(s, axis=-1, keepdims=True), sk)
    p = jnp.exp(s - mx)
    den = jnp.sum(p, axis=-1, keepdims=True) + jnp.exp(sk - mx)
    o_ref[0, 0:CTX_LEN, :] = from_kv_lanes(_dot(p.astype(BF16), vc) / den, CTX_LEN).astype(o_ref.dtype)

    skb = sink_rows(SWA_BLOCK)
    row_b = lax.broadcasted_iota(jnp.int32, (SWA_HEADS * SWA_BLOCK, 1), 0) % SWA_BLOCK
    col_b = lax.broadcasted_iota(jnp.int32, (1, kw), 1)

    def logits(n):
        ws = jnp.clip(n - 1, 0, nb - 3) * SWA_BLOCK
        qrows = pl.ds(pl.multiple_of(n * SWA_BLOCK, SWA_BLOCK), SWA_BLOCK)
        krows = pl.ds(pl.multiple_of(ws, SWA_BLOCK), kw)
        qb = jnp.concatenate([qr_ref[h, qrows, :] for h in range(SWA_HEADS)], axis=0)
        return _dot_nt(qb, kr_ref[krows, :]), _dot_nt(qb, kc), n, ws

    def attend(s_b, s_c, n, ws):
        valid = jnp.abs(n * SWA_BLOCK + row_b - (ws + col_b)) <= SWA_WINDOW
        s_b = jnp.where(valid, s_b, NEG_INF)
        mxb = jnp.maximum(jnp.maximum(jnp.max(s_b, axis=-1, keepdims=True),
                                      jnp.max(s_c, axis=-1, keepdims=True)), skb)
        p_b = jnp.exp(s_b - mxb)
        p_c = jnp.exp(s_c - mxb)
        denb = (jnp.sum(p_b, axis=-1, keepdims=True) + jnp.sum(p_c, axis=-1, keepdims=True)
                + jnp.exp(skb - mxb))
        vrows = pl.ds(pl.multiple_of(CTX_LEN + ws, SWA_BLOCK), kw)
        o4 = (_dot(p_b.astype(BF16), v_ref[0, vrows, :]) + _dot(p_c.astype(BF16), vc)) / denb
        orows = pl.ds(pl.multiple_of(CTX_LEN + n * SWA_BLOCK, SWA_BLOCK), SWA_BLOCK)
        o_ref[0, orows, :] = from_kv_lanes(o4, SWA_BLOCK).astype(o_ref.dtype)

    def block_pair(j, carry):
        first = logits(2 * j)
        second = logits(2 * j + 1)
        attend(*first)
        attend(*second)
        return carry

    lax.fori_loop(0, nb // 2, block_pair, 0)


def _swa_attention(p, cos, sin, sink):
    b, t, _ = p.shape
    seq = t - CTX_LEN
    return pl.pallas_call(
        functools.partial(_swa_kernel, t=t),
        grid=(b,),
        in_specs=[
            pl.BlockSpec((1, t, 256), lambda i: (i, 0, COL_SQ // 256)),
            pl.BlockSpec((1, t, 128), lambda i: (i, 0, COL_SK // 128)),
            pl.BlockSpec((1, t, 128), lambda i: (i, 0, COL_SV // 128)),
            pl.BlockSpec((seq, 256), lambda i: (0, 0)),
            pl.BlockSpec((seq, 256), lambda i: (0, 0)),
            pl.BlockSpec(memory_space=pltpu.SMEM),
        ],
        out_specs=pl.BlockSpec((1, t, 256), lambda i: (i, 0, 0)),
        out_shape=jax.ShapeDtypeStruct((b, t, 256), BF16),
        scratch_shapes=[pltpu.VMEM((SWA_HEADS, seq, 128), BF16), pltpu.VMEM((seq, 128), BF16)],
        compiler_params=_params(("arbitrary",)),
        name="swa_attention",
    )(p, p, p, cos, sin, sink)


def _out_proj_kernel(u_ref, ys_ref, d_ref, wg_ref, na_ref, gla_ref, swa_ref, wo_ref, g_ref,
                     modl_ref, modc_ref, x_ref, o_ref, *, tm, first):
    y = d_ref[...] * u_ref[0].astype(F32) + ys_ref[0]
    z = _gelu(y)
    a = z * _sigmoid(_dot(z.astype(BF16), wg_ref[...]))
    acc = _dot(a.astype(BF16), wo_ref[0:256, :])
    acc += _dot(na_ref[0], wo_ref[256:512, :])
    acc += _dot(gla_ref[0], wo_ref[512:768, :])
    acc += _dot(swa_ref[0], wo_ref[768:1024, :])
    row = (pl.program_id(1) + first) * tm + lax.broadcasted_iota(jnp.int32, (tm, 1), 0)
    gate = jnp.where(row < CTX_LEN, modc_ref[0, 2:3, :], modl_ref[0, 2:3, :])
    o_ref[0] = x_ref[0] + gate * (_rms(acc) * g_ref[...])


def _out_proj(p, ys, s5_d, w_glu, y_na, y_gla, y_swa, w_out, g_post, modl, modc, x, *, tm, skip_ctx=False):
    b, t_in, d = x.shape
    first = CTX_LEN // tm if skip_ctx else 0
    assert not skip_ctx or CTX_LEN % tm == 0
    t = t_in - first * tm
    tok = lambda w: pl.BlockSpec((1, tm, w), lambda i, j: (i, j + first, 0))
    full = lambda a: pl.BlockSpec(a.shape, lambda i, j: (0,) * a.ndim)
    return pl.pallas_call(
        functools.partial(_out_proj_kernel, tm=tm, first=first),
        grid=(b, t // tm),
        in_specs=[
            tok(256), tok(256), full(s5_d), full(w_glu), tok(256), tok(256), tok(256),
            full(w_out), full(g_post),
            pl.BlockSpec((1, 6, d), lambda i, j: (i, 0, 0)),
            pl.BlockSpec((1, 6, d), lambda i, j: (0, 0, 0)),
            tok(d),
        ],
        out_specs=pl.BlockSpec((1, tm, d), lambda i, j: (i, j, 0)),
        out_shape=jax.ShapeDtypeStruct((b, t, d), F32),
        compiler_params=_params(("arbitrary", "arbitrary")),
        name="out_proj",
    )(p, ys, s5_d, w_glu, y_na, y_gla, y_swa, w_out, g_post, modl, modc, x)


def _ffn_down_kernel(gt_ref, vl_ref, prev_ref, next_ref, cw_ref, wd_ref, g_ref, modl_ref, modc_ref, x_ref, o_ref,
                     a_ref, *, tm, t, ctx_rows):
    h = FFN_HALO
    row0 = pl.program_id(1) * tm
    w0, w1, w2 = cw_ref[0:1, :], cw_ref[1:2, :], cw_ref[2:3, :]
    g = gt_ref[0].astype(F32)
    gp = pltpu.roll(g, 1, 0)
    gn = pltpu.roll(g, tm - 1, 0)
    a_ref[...] = _gelu((gp * w0 + g * w1 + gn * w2).astype(BF16)) * vl_ref[0]
    loc = lax.broadcasted_iota(jnp.int32, (h, 1), 0)
    has_prev = jnp.where((row0 == 0) | (row0 == ctx_rows), 0.0, 1.0)
    has_next = jnp.where((row0 + tm == ctx_rows) | (row0 + tm == t), 0.0, 1.0)
    gp_top = jnp.where(loc == 0, prev_ref[0, h - 1:h, :].astype(F32) * has_prev, gp[0:h])
    top = gp_top * w0 + g[0:h] * w1 + gn[0:h] * w2
    a_ref[0:h, :] = _gelu(top.astype(BF16)) * vl_ref[0, 0:h, :]
    gn_bot = jnp.where(loc == h - 1, next_ref[0, 0:1, :].astype(F32) * has_next, gn[tm - h:tm])
    bot = gp[tm - h:tm] * w0 + g[tm - h:tm] * w1 + gn_bot * w2
    a_ref[tm - h:tm, :] = _gelu(bot.astype(BF16)) * vl_ref[0, tm - h:tm, :]
    acc = _dot(a_ref[...], wd_ref[...])
    gate = jnp.where(row0 < ctx_rows, modc_ref[0, 5:6, :], modl_ref[0, 5:6, :])
    o_ref[0] = x_ref[0] + gate * (_rms(acc) * g_ref[...])


def _ffn_down(gv, conv_w, w_down, g_post, modl, modc, x, *, tm, ctx_rows=CTX_LEN):
    b, t, d = x.shape
    assert ctx_rows % tm == 0 and t % tm == 0 and tm > 2 * FFN_HALO
    nbh = tm // FFN_HALO
    lasth = t // FFN_HALO - 1
    return pl.pallas_call(
        functools.partial(_ffn_down_kernel, tm=tm, t=t, ctx_rows=ctx_rows),
        grid=(b, t // tm),
        in_specs=[
            pl.BlockSpec((1, tm, D_FF), lambda i, j: (i, j, 0)),
            pl.BlockSpec((1, tm, D_FF), lambda i, j: (i, j, 1)),
            pl.BlockSpec((1, FFN_HALO, D_FF), lambda i, j: (i, jnp.maximum(j * nbh - 1, 0), 0)),
            pl.BlockSpec((1, FFN_HALO, D_FF), lambda i, j: (i, jnp.minimum((j + 1) * nbh, lasth), 0)),
            pl.BlockSpec((3, D_FF), lambda i, j: (0, 0)),
            pl.BlockSpec((D_FF, d), lambda i, j: (0, 0)),
            pl.BlockSpec((1, d), lambda i, j: (0, 0)),
            pl.BlockSpec((1, 6, d), lambda i, j: (i, 0, 0)),
            pl.BlockSpec((1, 6, d), lambda i, j: (0, 0, 0)),
            pl.BlockSpec((1, tm, d), lambda i, j: (i, j, 0)),
        ],
        out_specs=pl.BlockSpec((1, tm, d), lambda i, j: (i, j, 0)),
        out_shape=jax.ShapeDtypeStruct((b, t, d), F32),
        scratch_shapes=[pltpu.VMEM((tm, D_FF), BF16)],
        compiler_params=_params(("arbitrary", "arbitrary")),
        name="ffn_down",
    )(gv, gv, gv, gv, conv_w, w_down, g_post, modl, modc, x)


def _reorder_w_in(w_in):
    sizes = (256, 256, 256, 256, 256, 256, 256, 16, 16, 256, 256, 128, 128)
    offs = np.concatenate([[0], np.cumsum(sizes)])
    seg = lambda i: w_in[..., offs[i]:offs[i + 1]]
    pad = jnp.zeros(w_in.shape[:-1] + (P_WIDTH - COL_GG - 2 * GLA_RANK,), w_in.dtype)
    return jnp.concatenate([seg(0), seg(1), seg(2), seg(3), seg(4), seg(5), seg(6), seg(9),
                            seg(10), seg(11), seg(12), seg(7), seg(8), pad], axis=-1)


def kernel(x, c, ctx, c_ctx, w_mod, b_mod, g_pre_mix, g_post_mix, g_pre_ffn, g_post_ffn, w_in, w_out, s5_lam_re, s5_lam_im, s5_b_re, s5_b_im, s5_c_re, s5_c_im, s5_log_step, s5_d, s5_w_glu, na_rpb, gla_w_gate2, gla_b_gate, gla_g_norm, swa_sink, ffn_w_up, ffn_conv, ffn_w_down):
    bsz, seq, d = x.shape
    depth = w_mod.shape[0]
    t = CTX_LEN + seq
    tm = 768

    cvec = jnp.concatenate([c, c_ctx[None, :]], axis=0)
    cvec = jnp.pad(cvec, ((0, (-cvec.shape[0]) % 8), (0, 0)))
    mod = _modulation(cvec, w_mod, b_mod).reshape(depth, cvec.shape[0], 6, d)

    w_in_r = _reorder_w_in(w_in).astype(BF16)
    w_out_b = w_out.astype(BF16)
    w_glu_b = s5_w_glu.astype(BF16)
    w_up_b = ffn_w_up.astype(BF16)
    w_down_b = ffn_w_down.astype(BF16)
    w2 = jnp.zeros((depth, 2, 128, GLA_HEADS * GLA_DK), F32)
    w2 = w2.at[:, 0, 0:GLA_RANK].set(gla_w_gate2[:, 0]).at[:, 1, GLA_RANK:2 * GLA_RANK].set(gla_w_gate2[:, 1])
    w2 = w2.astype(BF16)
    cos, sin = _rope_tables(seq)

    xc = jnp.concatenate([ctx, x], axis=1)
    for l in range(depth):
        modl = mod[l, :bsz]
        modc = mod[l, bsz:bsz + 1]
        p = _nm_matmul(xc, g_pre_mix[l][None], modl, modc, w_in_r[l], k_shift=0, k_scale=1, tm=tm, tn=P_WIDTH)
        tabs = _s5_prep(s5_lam_re[l], s5_lam_im[l], s5_b_re[l], s5_b_im[l], s5_c_re[l], s5_c_im[l], s5_log_step[l])
        ys = _s5_scan(p, *tabs)
        y_na = _na_attention(p, na_rpb[l])
        y_gla = _gla_attention(p, w2[l], gla_b_gate[l], gla_g_norm[l][None])
        y_swa = _swa_attention(p, cos, sin, swa_sink[l])
        last = l == depth - 1
        ctx_rows = 0 if last else CTX_LEN
        xc = _out_proj(p, ys, s5_d[l][None], w_glu_b[l], y_na, y_gla, y_swa, w_out_b[l],
                       g_post_mix[l][None], modl, modc, xc, tm=CTX_LEN if last else tm, skip_ctx=last)
        gv = _nm_matmul(xc, g_pre_ffn[l][None], modl, modc, w_up_b[l], k_shift=3, k_scale=4,
                        tm=1024 if last else tm, tn=D_FF, ctx_rows=ctx_rows)
        xc = _ffn_down(gv, ffn_conv[l], w_down_b[l], g_post_ffn[l][None], modl, modc, xc, tm=256, ctx_rows=ctx_rows)
    return xc
```

```python
import functools
import math

import numpy as np
import jax
import jax.numpy as jnp
from jax import lax
from jax.experimental import pallas as pl
from jax.experimental.pallas import tpu as pltpu

F32 = jnp.float32
BF16 = jnp.bfloat16

D_MODEL = 1024
GRID_W = 64
CTX_LEN = 256
HEAD_DIM = 64
EPS = 1e-6
NEG_INF = -1e30

S5_CH = 256
S5_GROUP_CH = 16
S5_GROUPS = S5_CH // S5_GROUP_CH
S5_STATE = 64
S5_TC = 128
S5_K = S5_TC * S5_GROUP_CH

NA_HEADS = 4
NA_KH = 8
NA_KW = 16
NA_QROWS = 4
NA_WROWS = NA_QROWS + NA_KH

GLA_HEADS = 4
GLA_DK = 64
GLA_RANK = 16
GLA_TAU = 16.0
GLA_CHUNK = 64
GLA_GROUP = CTX_LEN // GLA_CHUNK

SWA_HEADS = 4
SWA_KV_HEADS = 2
SWA_WINDOW = 128
SWA_BLOCK = 128
ROPE_BASE = 10000.0

D_FF = 2816
FFN_HALO = 16

COL_S5, COL_NAQ, COL_NAK, COL_NAV = 0, 256, 512, 768
COL_GQ, COL_GK, COL_GV, COL_GR = 1024, 1280, 1536, 1792
COL_SQ, COL_SK, COL_SV, COL_GG = 2048, 2304, 2432, 2560
P_WIDTH = 2688

VMEM_LIMIT = 56 * 1024 * 1024


def _dot(a, b):
    return jnp.dot(a, b, preferred_element_type=F32)


def _dot_exact(a, b):
    return jnp.dot(a, b, preferred_element_type=F32, precision=lax.Precision.HIGHEST)


def _dot_nt(a, b):
    return lax.dot_general(a, b, (((1,), (1,)), ((), ())), preferred_element_type=F32)


def _dot_tn(a, b):
    return lax.dot_general(a, b, (((0,), (0,)), ((), ())), preferred_element_type=F32)


def _gelu(x):
    k = math.sqrt(2.0 / math.pi)
    return x * (0.5 + 0.5 * jnp.tanh(x * (k + (k * 0.044715) * (x * x))))


def _sigmoid(x):
    return 1.0 / (1.0 + jnp.exp(-x))


def _rms(x):
    return x * lax.rsqrt(jnp.mean(x * x, axis=-1, keepdims=True) + EPS)


def _params(sem):
    return pltpu.CompilerParams(dimension_semantics=sem, vmem_limit_bytes=VMEM_LIMIT)


def _mod_kernel(c_ref, w_ref, b_ref, o_ref):
    c = c_ref[...]
    s = (c * _sigmoid(c)).astype(BF16)
    o_ref[0] = _dot(s, w_ref[0].astype(BF16)) + b_ref[0]


def _modulation(cvec, w_mod, b_mod):
    depth, d, n = w_mod.shape
    r = cvec.shape[0]
    tn = 1536
    return pl.pallas_call(
        _mod_kernel,
        grid=(depth, n // tn),
        in_specs=[
            pl.BlockSpec((r, d), lambda l, j: (0, 0)),
            pl.BlockSpec((1, d, tn), lambda l, j: (l, 0, j)),
            pl.BlockSpec((1, 1, tn), lambda l, j: (l, 0, j)),
        ],
        out_specs=pl.BlockSpec((1, r, tn), lambda l, j: (l, 0, j)),
        out_shape=jax.ShapeDtypeStruct((depth, r, n), F32),
        compiler_params=_params(("arbitrary", "arbitrary")),
        name="modulation",
    )(cvec, w_mod, b_mod.reshape(depth, 1, n))


def _nm_matmul_kernel(x_ref, g_ref, modl_ref, modc_ref, w_ref, o_ref, *rest, tm, k_shift, k_scale, ctx_rows, s5_chunks):
    h_ref = rest[-1]

    @pl.when(pl.program_id(2) == 0)
    def _():
        y = _rms(x_ref[0]) * g_ref[...]
        row = pl.program_id(1) * tm + lax.broadcasted_iota(jnp.int32, (tm, 1), 0)
        is_ctx = row < ctx_rows
        scale = jnp.where(is_ctx, modc_ref[0, k_scale:k_scale + 1, :], modl_ref[0, k_scale:k_scale + 1, :])
        shift = jnp.where(is_ctx, modc_ref[0, k_shift:k_shift + 1, :], modl_ref[0, k_shift:k_shift + 1, :])
        h_ref[...] = (y * (1.0 + scale) + shift).astype(BF16)

    res = _dot(h_ref[...], w_ref[...])
    o_ref[0] = res.astype(o_ref.dtype)
    if s5_chunks:
        ug_ref = rest[0]
        for j in range(tm // S5_TC):
            z = res[j * S5_TC:(j + 1) * S5_TC, COL_S5:COL_S5 + S5_CH].T.astype(BF16)
            for g in range(S5_GROUPS):
                ug_ref[g, j, 0] = z[g * S5_GROUP_CH:(g + 1) * S5_GROUP_CH, :]


def _nm_matmul(x, g, modl, modc, w, *, k_shift, k_scale, tm, tn, ctx_rows=CTX_LEN, s5_chunks=False):
    b, t, d = x.shape
    n = w.shape[1]
    assert t % tm == 0 and n % tn == 0 and (not s5_chunks or (tn == n and tm % S5_TC == 0))
    out_specs = [pl.BlockSpec((1, tm, tn), lambda i, j, k: (i, j, k))]
    out_shape = [jax.ShapeDtypeStruct((b, t, n), BF16)]
    if s5_chunks:
        out_specs.append(pl.BlockSpec((S5_GROUPS, tm // S5_TC, 1, S5_GROUP_CH, S5_TC), lambda i, j, k: (0, j, i, 0, 0)))
        out_shape.append(jax.ShapeDtypeStruct((S5_GROUPS, t // S5_TC, b, S5_GROUP_CH, S5_TC), BF16))
    out = pl.pallas_call(
        functools.partial(_nm_matmul_kernel, tm=tm, k_shift=k_shift, k_scale=k_scale, ctx_rows=ctx_rows,
                          s5_chunks=s5_chunks),
        grid=(b, t // tm, n // tn),
        in_specs=[
            pl.BlockSpec((1, tm, d), lambda i, j, k: (i, j, 0)),
            pl.BlockSpec((1, d), lambda i, j, k: (0, 0)),
            pl.BlockSpec((1, 6, d), lambda i, j, k: (i, 0, 0)),
            pl.BlockSpec((1, 6, d), lambda i, j, k: (0, 0, 0)),
            pl.BlockSpec((d, tn), lambda i, j, k: (0, k)),
        ],
        out_specs=out_specs,
        out_shape=out_shape,
        scratch_shapes=[pltpu.VMEM((tm, d), BF16)],
        compiler_params=_params(("arbitrary", "arbitrary", "arbitrary")),
        name="norm_mod_matmul",
    )(x, g, modl, modc, w)
    return tuple(out) if s5_chunks else out[0]


def _s5_prep_kernel(lr_ref, lc_ref, ls_ref, bt_ref, ct_ref, t_ref, s_ref, c_ref, a_ref, kap_ref):
    tc = S5_TC
    p = S5_STATE
    lane = lax.broadcasted_iota(jnp.int32, (1, tc), 1).astype(F32)
    sub = lax.broadcasted_iota(jnp.int32, (tc, 1), 0).astype(F32)
    s_idx = lax.broadcasted_iota(jnp.int32, (tc, tc), 0)
    t_idx = lax.broadcasted_iota(jnp.int32, (tc, tc), 1)
    for d in range(2):
        step = jnp.exp(ls_ref[0, d:d + 1, :])
        lre = lr_ref[0, d, 0:1, :]
        lim = lr_ref[0, d, 1:2, :]
        zr = lre * step
        zi = lim * step
        mag = jnp.exp(zr)
        nr = mag * jnp.cos(zi) - 1.0
        ni = mag * jnp.sin(zi)
        den = lre * lre + lim * lim
        cr = (nr * lre + ni * lim) / den
        ci = (ni * lre - nr * lim) / den
        btr = bt_ref[0, d, 0]
        bti = bt_ref[0, d, 1]
        bbr = cr * btr - ci * bti
        bbi = cr * bti + ci * btr
        ma = jnp.exp(zr * tc)
        ar = ma * jnp.cos(zi * tc)
        ai = ma * jnp.sin(zi * tc)
        a_ref[0, 0:1, d * 2 * p:d * 2 * p + p] = ar
        a_ref[0, 0:1, d * 2 * p + p:(d + 1) * 2 * p] = ar
        a_ref[0, 1:2, d * 2 * p:d * 2 * p + p] = -ai
        a_ref[0, 1:2, d * 2 * p + p:(d + 1) * 2 * p] = ai
        zrc = lc_ref[0, d, :, 0:1] * step
        zic = lc_ref[0, d, :, 1:2] * step

        def powers(tau):
            m = jnp.exp(zrc * tau)
            ang = zic * tau
            return m * jnp.cos(ang), m * jnp.sin(ang)

        if d == 0:
            pkr, pki = powers(lane)
            pcr, pci = powers(lane + 1.0)
            es = (tc - 1.0) - sub
        else:
            pkr, pki = powers(jnp.where(lane == 0.0, 0.0, tc - lane))
            pcr, pci = powers(tc - lane)
            es = sub
        ctr = ct_ref[0, d, 0]
        cti = ct_ref[0, d, 1]
        for co in range(S5_GROUP_CH):
            ccr = ctr[:, co:co + 1]
            cci = cti[:, co:co + 1]
            kap_ref[d, co] = (_dot_exact(bbr, ccr * pkr - cci * pki)
                              - _dot_exact(bbi, ccr * pki + cci * pkr))
            cols = slice(co * tc, (co + 1) * tc)
            c_ref[0, d * 2 * p:d * 2 * p + p, cols] = (ccr * pcr - cci * pci).astype(BF16)
            c_ref[0, d * 2 * p + p:(d + 1) * 2 * p, cols] = (-(ccr * pci + cci * pcr)).astype(BF16)
        me = jnp.exp(zr * es)
        er = me * jnp.cos(zi * es)
        ei = me * jnp.sin(zi * es)
        for cin in range(S5_GROUP_CH):
            br = bbr[cin:cin + 1, :]
            bi = bbi[cin:cin + 1, :]
            rows = slice(cin * tc, (cin + 1) * tc)
            s_ref[0, rows, d * 2 * p:d * 2 * p + p] = (er * br - ei * bi).astype(BF16)
            s_ref[0, rows, d * 2 * p + p:(d + 1) * 2 * p] = (er * bi + ei * br).astype(BF16)

    for co in range(S5_GROUP_CH):
        cols = slice(co * tc, (co + 1) * tc)

        def toeplitz(i, carry):
            cins = [i * 4 + j for j in range(4)]
            kr = [kap_ref[1, co, pl.ds(cin, 1), :] for cin in cins]
            kf = [kap_ref[0, co, pl.ds(cin, 1), :] + jnp.where(lane == 0.0, r, 0.0)
                  for cin, r in zip(cins, kr)]
            mixed = [jnp.where(t_idx + s_idx < tc, jnp.broadcast_to(f, (tc, tc)), jnp.broadcast_to(r, (tc, tc)))
                     for f, r in zip(kf, kr)]
            blk = [pltpu.roll(m, 0, 1, stride=1, stride_axis=0).astype(BF16) for m in mixed]
            for cin, b in zip(cins, blk):
                t_ref[0, pl.ds(pl.multiple_of(cin * tc, tc), tc), cols] = b
            return carry

        lax.fori_loop(0, S5_GROUP_CH // 4, toeplitz, 0)


def _s5_prep(lam_re, lam_im, b_re, b_im, c_re, c_im, log_step):
    g = S5_GROUPS
    lam = jnp.stack([lam_re, lam_im], axis=-2).astype(F32).transpose(1, 0, 2, 3)
    lam_c = lam.transpose(0, 1, 3, 2)
    ls = log_step.astype(F32).T[:, :, None]
    bt = jnp.stack([b_re, b_im], axis=2).astype(F32).transpose(1, 0, 2, 4, 3)
    ct = jnp.stack([c_re, c_im], axis=2).astype(F32).transpose(1, 0, 2, 4, 3)
    blk = lambda a: pl.BlockSpec((1,) + a.shape[1:], lambda i: (i,) + (0,) * (a.ndim - 1))
    return pl.pallas_call(
        _s5_prep_kernel,
        grid=(g,),
        in_specs=[blk(lam), blk(lam_c), blk(ls), blk(bt), blk(ct)],
        out_specs=[
            pl.BlockSpec((1, S5_K, S5_K), lambda i: (i, 0, 0)),
            pl.BlockSpec((1, S5_K, 4 * S5_STATE), lambda i: (i, 0, 0)),
            pl.BlockSpec((1, 4 * S5_STATE, S5_K), lambda i: (i, 0, 0)),
            pl.BlockSpec((1, 2, 4 * S5_STATE), lambda i: (i, 0, 0)),
        ],
        out_shape=[
            jax.ShapeDtypeStruct((g, S5_K, S5_K), BF16),
            jax.ShapeDtypeStruct((g, S5_K, 4 * S5_STATE), BF16),
            jax.ShapeDtypeStruct((g, 4 * S5_STATE, S5_K), BF16),
            jax.ShapeDtypeStruct((g, 2, 4 * S5_STATE), F32),
        ],
        scratch_shapes=[pltpu.VMEM((2, S5_GROUP_CH, S5_GROUP_CH, S5_TC), F32)],
        compiler_params=_params(("arbitrary",)),
        name="s5_prep",
    )(lam, lam_c, ls, bt, ct)


def _s5_scan_kernel(u_ref, t_ref, s_ref, c_ref, a_ref, y_ref, loc_ref, hin_ref, *, bsz, nch):
    w = 2 * S5_STATE
    u = u_ref[0]
    loc_ref[...] = _dot(u, s_ref[0])
    a1f, a2f = a_ref[0, 0:1, 0:w], a_ref[0, 1:2, 0:w]
    a1r, a2r = a_ref[0, 0:1, w:2 * w], a_ref[0, 1:2, w:2 * w]
    nctx = CTX_LEN // S5_TC

    def body(i, carry):
        hf, hr = carry
        kr = jnp.where(i < nctx, nctx - 1 - i, nch + nctx - 1 - i)
        rf = pl.ds(pl.multiple_of(i * bsz, bsz), bsz)
        rr = pl.ds(pl.multiple_of(kr * bsz, bsz), bsz)
        hin_ref[rf, 0:w] = hf
        hin_ref[rr, w:2 * w] = hr
        hf = a1f * hf + a2f * pltpu.roll(hf, S5_STATE, 1) + loc_ref[rf, 0:w]
        hr = a1r * hr + a2r * pltpu.roll(hr, S5_STATE, 1) + loc_ref[rr, w:2 * w]
        return hf, hr

    zero = jnp.zeros((bsz, w), F32)
    lax.fori_loop(0, nch, body, (zero, zero))
    y_ref[0] = _dot(u, t_ref[0]) + _dot(hin_ref[...].astype(BF16), c_ref[0])


def _s5_scan(ug, tmat, smat, cmat, amat):
    g, nch, b = ug.shape[:3]
    m = nch * b
    yg = pl.pallas_call(
        functools.partial(_s5_scan_kernel, bsz=b, nch=nch),
        grid=(g,),
        in_specs=[
            pl.BlockSpec((1, m, S5_K), lambda i: (i, 0, 0)),
            pl.BlockSpec((1, S5_K, S5_K), lambda i: (i, 0, 0)),
            pl.BlockSpec((1, S5_K, 4 * S5_STATE), lambda i: (i, 0, 0)),
            pl.BlockSpec((1, 4 * S5_STATE, S5_K), lambda i: (i, 0, 0)),
            pl.BlockSpec((1, 2, 4 * S5_STATE), lambda i: (i, 0, 0)),
        ],
        out_specs=pl.BlockSpec((1, m, S5_K), lambda i: (i, 0, 0)),
        out_shape=jax.ShapeDtypeStruct((g, m, S5_K), F32),
        scratch_shapes=[pltpu.VMEM((m, 4 * S5_STATE), F32), pltpu.VMEM((m, 4 * S5_STATE), F32)],
        compiler_params=_params(("arbitrary",)),
        name="s5_scan",
    )(ug.reshape(g, m, S5_K), tmat, smat, cmat, amat)
    return yg.reshape(g, nch, b, S5_GROUP_CH, S5_TC)


def _na_block_geometry(rows):
    geo = []
    for r0 in (0, NA_QROWS, rows - NA_QROWS):
        geo.append((r0, min(max(r0 - NA_KH // 2, 0), rows - NA_WROWS)))
    return geo


def _na_fill_bias(rpb_ref, bt_ref, rows):
    w = GRID_W
    qc = lax.broadcasted_iota(jnp.int32, (w, w), 0)
    kc = lax.broadcasted_iota(jnp.int32, (w, w), 1)
    ws = jnp.clip(qc - NA_KW // 2, 0, w - NA_KW)
    in_win = (kc >= ws) & (kc < ws + NA_KW)
    neg = jnp.full((w, w), NEG_INF, F32)
    for h in range(NA_HEADS):
        cm = []
        for dr in range(2 * NA_KH - 1):
            v = jnp.broadcast_to(rpb_ref[h, dr:dr + 1, :], (w, 128))
            v = pltpu.roll(v, 128 - (NA_KW - 1), 1, stride=1, stride_axis=0)
            cm.append(jnp.where(in_win, v[:, :w], NEG_INF))
        for kind, (r0, w0) in enumerate(_na_block_geometry(rows)):
            for qr in range(NA_QROWS):
                start = min(max(r0 + qr - NA_KH // 2, 0), rows - NA_KH)
                for kr in range(0, NA_WROWS, 2):
                    pair = []
                    for k in (kr, kr + 1):
                        inside = start <= w0 + k < start + NA_KH
                        pair.append(cm[(w0 + k) - (r0 + qr) + NA_KH - 1] if inside else neg)
                    r_lo = (h * NA_QROWS + qr) * w
                    bt_ref[kind, r_lo:r_lo + w, kr * w:(kr + 2) * w] = jnp.concatenate(pair, axis=1)


def _na_kernel(q_ref, k_ref, v_ref, rpb_ref, o_ref, bt_ref, *, rows):
    qn = NA_QROWS * GRID_W
    kn = NA_WROWS * GRID_W
    nblk = rows // NA_QROWS
    scale = HEAD_DIM ** -0.5

    @pl.when(pl.program_id(0) == 0)
    def _():
        _na_fill_bias(rpb_ref, bt_ref, rows)

    lane_head = lax.broadcasted_iota(jnp.int32, (1, NA_HEADS * HEAD_DIM), 1) // HEAD_DIM

    def stack_heads(x):
        return jnp.concatenate([jnp.where(lane_head == h, x, 0.0) for h in range(NA_HEADS)], axis=0)

    def pick_heads(o4, n):
        out = o4[0:n]
        for h in range(1, NA_HEADS):
            out = jnp.where(lane_head == h, o4[h * n:(h + 1) * n], out)
        return out

    kc = k_ref[0, 0:CTX_LEN, :]
    vc = v_ref[0, 0:CTX_LEN, :]
    s = _dot_nt(stack_heads(q_ref[0, 0:CTX_LEN, :] * scale), kc)
    p = jnp.exp(s - jnp.max(s, axis=-1, keepdims=True))
    o = _dot(p.astype(BF16), vc) / jnp.sum(p, axis=-1, keepdims=True)
    o_ref[0, 0:CTX_LEN, :] = pick_heads(o, CTX_LEN).astype(o_ref.dtype)

    def logits(i):
        w0 = jnp.clip(i * NA_QROWS - NA_KH // 2, 0, rows - NA_WROWS)
        kind = jnp.where(i > 0, 1, 0) + jnp.where(i == nblk - 1, 1, 0)
        qrows = pl.ds(pl.multiple_of(CTX_LEN + i * qn, qn), qn)
        krows = pl.ds(pl.multiple_of(CTX_LEN + w0 * GRID_W, GRID_W), kn)
        q4 = stack_heads(q_ref[0, qrows, :] * scale)
        return _dot_nt(q4, k_ref[0, krows, :]), _dot_nt(q4, kc), kind, qrows, krows

    def attend(s_nb, s_cx, kind, qrows, krows):
        p_nb, p_cx, inv = [], [], []
        for h in range(NA_HEADS):
            r = slice(h * qn, (h + 1) * qn)
            a = s_nb[r] + bt_ref[kind, r, :]
            c = s_cx[r]
            mx = jnp.maximum(jnp.max(a, axis=-1, keepdims=True), jnp.max(c, axis=-1, keepdims=True))
            ea = jnp.exp(a - mx)
            ec = jnp.exp(c - mx)
            inv.append(1.0 / (jnp.sum(ea, axis=-1, keepdims=True) + jnp.sum(ec, axis=-1, keepdims=True)))
            p_nb.append(ea.astype(BF16))
            p_cx.append(ec.astype(BF16))
        o4 = (_dot(jnp.concatenate(p_nb, axis=0), v_ref[0, krows, :])
              + _dot(jnp.concatenate(p_cx, axis=0), vc)) * jnp.concatenate(inv, axis=0)
        o_ref[0, qrows, :] = pick_heads(o4, qn).astype(o_ref.dtype)

    def block_pair(j, carry):
        first = logits(2 * j)
        second = logits(2 * j + 1)
        attend(*first)
        attend(*second)
        return carry

    lax.fori_loop(0, nblk // 2, block_pair, 0)


def _na_attention(p, rpb):
    b, t, _ = p.shape
    rows = (t - CTX_LEN) // GRID_W
    rpb_p = jnp.pad(rpb.astype(F32), ((0, 0), (0, 1), (0, 128 - rpb.shape[2])))
    cb = lambda c: (lambda i: (i, 0, c // 256))
    return pl.pallas_call(
        functools.partial(_na_kernel, rows=rows),
        grid=(b,),
        in_specs=[
            pl.BlockSpec((1, t, 256), cb(COL_NAQ)),
            pl.BlockSpec((1, t, 256), cb(COL_NAK)),
            pl.BlockSpec((1, t, 256), cb(COL_NAV)),
            pl.BlockSpec(rpb_p.shape, lambda i: (0, 0, 0)),
        ],
        out_specs=pl.BlockSpec((1, t, 256), lambda i: (i, 0, 0)),
        out_shape=jax.ShapeDtypeStruct((b, t, 256), BF16),
        scratch_shapes=[pltpu.VMEM((3, NA_HEADS * NA_QROWS * GRID_W, NA_WROWS * GRID_W), F32)],
        compiler_params=_params(("arbitrary",)),
        name="na_attention",
    )(p, p, p, rpb_p)


def _log_sigmoid(x):
    return jnp.minimum(x, 0.0) - jnp.log(1.0 + jnp.exp(-jnp.abs(x)))


def _gla_kernel(q_ref, k_ref, v_ref, r_ref, gg_ref, w2_ref, bg_ref, gn_ref, o_ref,
                la_ref, acc_ref, qd_ref, ke_ref, el_ref, st_ref, *, t):
    c = GLA_CHUNK
    ng = GLA_GROUP
    gc = ng * c
    hd = GLA_HEADS * GLA_DK
    ngrp = t // gc
    scale = GLA_DK ** -0.5
    gi = lax.broadcasted_iota(jnp.int32, (gc, gc), 0)
    gj = lax.broadcasted_iota(jnp.int32, (gc, gc), 1)
    same_chunk = (gi // c) == (gj // c)
    same_head = (lax.broadcasted_iota(jnp.int32, (hd, hd), 0) // GLA_DK
                 == lax.broadcasted_iota(jnp.int32, (hd, hd), 1) // GLA_DK)
    row_i = lax.broadcasted_iota(jnp.int32, (c, hd), 0)
    col_j = lax.broadcasted_iota(jnp.int32, (c, hd), 1) % c
    ones4 = same_chunk.astype(BF16)
    tri4 = [(same_chunk & ((gj % c) <= (gi % c))).astype(BF16), (same_chunk & ((gj % c) >= (gi % c))).astype(BF16)]
    keep = [col_j <= row_i, col_j >= row_i]
    for d in range(2):
        x = _dot(gg_ref[0], w2_ref[d]) + bg_ref[d:d + 1, :]
        la_ref[d] = _log_sigmoid(x) * (1.0 / GLA_TAU)

    def interleave(stages):
        out = [None] * len(stages)
        live = list(range(len(stages)))
        while live:
            for i in list(live):
                try:
                    out[i] = next(stages[i])
                except StopIteration:
                    live.remove(i)
        return out

    def intra_stages(d, sg):
        rows4 = pl.ds(pl.multiple_of(sg * gc, gc), gc)
        la = la_ref[d, rows4, :]
        hi = la.astype(BF16)
        lo = (la - hi.astype(F32)).astype(BF16)
        bc = _dot(tri4[d], hi) + _dot(tri4[d], lo)
        bl = _dot(ones4, hi) + _dot(ones4, lo)
        yield None
        qd = (q_ref[0, rows4, :].astype(F32) * scale * jnp.exp(bc)).astype(BF16)
        kf = k_ref[0, rows4, :].astype(F32)
        kd = (kf * jnp.exp(-bc)).astype(BF16)
        vv = v_ref[0, rows4, :]
        qd_ref[d, rows4, :] = qd
        ke_ref[d, rows4, :] = (kf * jnp.exp(bl - bc)).astype(BF16)
        el = jnp.exp(bl)
        for g in range(ng):
            el_ref[d, sg * ng + g] = el[g * c:g * c + 1, :]
        kbd = [jnp.where(same_head, jnp.concatenate([kd[g * c:(g + 1) * c]] * GLA_HEADS, axis=0), 0.0)
               for g in range(ng)]
        sc = [_dot_nt(qd[g * c:(g + 1) * c], kbd[g]) for g in range(ng)]
        yield None
        a = [jnp.where(keep[d], s, 0.0).astype(BF16) for s in sc]
        vbd = [jnp.where(same_head, jnp.concatenate([vv[g * c:(g + 1) * c]] * GLA_HEADS, axis=0), 0.0)
               for g in range(ng)]
        yield jnp.concatenate([_dot(a[g], vbd[g]) for g in range(ng)], axis=0)

    def intra(sg, carry):
        o_fwd, o_bwd = interleave([intra_stages(0, sg), intra_stages(1, sg)])
        acc_ref[pl.ds(pl.multiple_of(sg * gc, gc), gc), :] = o_fwd + o_bwd
        return carry

    lax.fori_loop(0, ngrp, intra, 0)
    st_ref[...] = jnp.zeros_like(st_ref)

    def inter_stages(d, sg):
        if d == 0:
            grp, visit = sg, list(range(ng))
        else:
            grp, visit = jnp.where(sg == 0, 0, ngrp - sg), list(range(ng - 1, -1, -1))
        rows = [pl.ds(pl.multiple_of((grp * ng + g) * c, c), c) for g in visit]
        ds = [_dot_tn(v_ref[0, r, :], ke_ref[d, r, :]) for r in rows]
        yield None
        st = st_ref[d]
        sprev = []
        for i, g in enumerate(visit):
            sprev.append(st.astype(BF16))
            st = st * el_ref[d, grp * ng + g] + jnp.where(same_head, ds[i], 0.0)
        st_ref[d] = st
        oi = [_dot_nt(qd_ref[d, r, :], sprev[i]) for i, r in enumerate(rows)]
        yield None
        for i, r in enumerate(rows):
            acc_ref[r, :] += oi[i]
        yield None

    def inter(sg, carry):
        interleave([inter_stages(0, sg), inter_stages(1, sg)])
        return carry

    lax.fori_loop(0, ngrp, inter, 0)

    acc = acc_ref[...]
    sq = acc * acc
    sq_hi = sq.astype(BF16)
    sq_lo = (sq - sq_hi.astype(F32)).astype(BF16)
    avg = jnp.where(same_head, 1.0 / GLA_DK, 0.0).astype(BF16)
    ms = _dot(sq_hi, avg) + _dot(sq_lo, avg)
    r = r_ref[0].astype(F32)
    o_ref[0] = (acc * lax.rsqrt(ms + EPS) * gn_ref[...] * (r * _sigmoid(r))).astype(o_ref.dtype)


def _gla_attention(p, w2cat, b_gate, g_norm):
    b, t, _ = p.shape
    hd = GLA_HEADS * GLA_DK
    cb = lambda c: (lambda i: (i, 0, c // 256))
    return pl.pallas_call(
        functools.partial(_gla_kernel, t=t),
        grid=(b,),
        in_specs=[
            pl.BlockSpec((1, t, 256), cb(COL_GQ)),
            pl.BlockSpec((1, t, 256), cb(COL_GK)),
            pl.BlockSpec((1, t, 256), cb(COL_GV)),
            pl.BlockSpec((1, t, 256), cb(COL_GR)),
            pl.BlockSpec((1, t, 128), lambda i: (i, 0, COL_GG // 128)),
            pl.BlockSpec((2, 128, 256), lambda i: (0, 0, 0)),
            pl.BlockSpec((2, 256), lambda i: (0, 0)),
            pl.BlockSpec((1, hd), lambda i: (0, 0)),
        ],
        out_specs=pl.BlockSpec((1, t, 256), lambda i: (i, 0, 0)),
        out_shape=jax.ShapeDtypeStruct((b, t, 256), BF16),
        scratch_shapes=[
            pltpu.VMEM((2, t, hd), F32),
            pltpu.VMEM((t, hd), F32),
            pltpu.VMEM((2, t, hd), BF16),
            pltpu.VMEM((2, t, hd), BF16),
            pltpu.VMEM((2, t // GLA_CHUNK, 1, hd), F32),
            pltpu.VMEM((2, hd, hd), F32),
        ],
        compiler_params=_params(("arbitrary",)),
        name="gla_attention",
    )(p, p, p, p, p, w2cat, b_gate, jnp.tile(g_norm.reshape(1, GLA_DK), (1, GLA_HEADS)))


def _rope_tables(seq):
    pos = np.arange(seq)
    half = HEAD_DIM // 2
    inv_freq = ROPE_BASE ** (-np.arange(0, half, 2, dtype=np.float32) / half)
    cos_l, sin_l = [], []
    for p_axis in (pos // GRID_W, pos % GRID_W):
        ang = p_axis.astype(np.float32)[:, None] * inv_freq[None, :]
        cos_l += [np.cos(ang), np.cos(ang)]
        sin_l += [-np.sin(ang), np.sin(ang)]
    cos = np.concatenate(cos_l, axis=-1).astype(np.float32)
    sin = np.concatenate(sin_l, axis=-1).astype(np.float32)
    return jnp.asarray(np.tile(cos, (1, SWA_HEADS))), jnp.asarray(np.tile(sin, (1, SWA_HEADS)))


def _rope(x, cos, sin):
    n = x.shape[-1]
    lane = lax.broadcasted_iota(jnp.int32, x.shape, 1)
    partner = jnp.where((lane % 32) < 16, pltpu.roll(x, n - 16, 1), pltpu.roll(x, 16, 1))
    return x * cos + partner * sin


def _swa_kernel(q_ref, k_ref, v_ref, cos_ref, sin_ref, sink_ref, o_ref, qr_ref, kr_ref, *, t):
    seq = t - CTX_LEN
    nb = seq // SWA_BLOCK
    grp = SWA_HEADS // SWA_KV_HEADS
    kw = 3 * SWA_BLOCK
    scale = HEAD_DIM ** -0.5
    kv_w = SWA_KV_HEADS * HEAD_DIM
    lane = lax.broadcasted_iota(jnp.int32, (1, kv_w), 1)

    def to_kv_lanes(x, h):
        slab = x[:, (h // grp) * kv_w:(h // grp + 1) * kv_w]
        if h % grp != h // grp:
            slab = pltpu.roll(slab, HEAD_DIM, 1)
        return jnp.where(lane // HEAD_DIM == h // grp, slab, 0.0)

    def from_kv_lanes(o4, n):
        slabs = []
        for pair in range(SWA_HEADS // grp):
            lo = o4[(2 * pair) * n:(2 * pair + 1) * n]
            hi = o4[(2 * pair + 1) * n:(2 * pair + 2) * n]
            if pair == 0:
                hi = pltpu.roll(hi, HEAD_DIM, 1)
            else:
                lo = pltpu.roll(lo, HEAD_DIM, 1)
            slabs.append(jnp.where(lane < HEAD_DIM, lo, hi))
        return jnp.concatenate(slabs, axis=1)

    def sink_rows(n):
        row = lax.broadcasted_iota(jnp.int32, (SWA_HEADS * n, 1), 0) // n
        sk = jnp.full((SWA_HEADS * n, 1), sink_ref[0], F32)
        for h in range(1, SWA_HEADS):
            sk = jnp.where(row == h, sink_ref[h], sk)
        return sk

    cos = cos_ref[...]
    sin = sin_ref[...]
    qrot = _rope(q_ref[0, CTX_LEN:, :].astype(F32), cos, sin) * scale
    for h in range(SWA_HEADS):
        qr_ref[h] = to_kv_lanes(qrot, h).astype(BF16)
    kr_ref[...] = _rope(k_ref[0, CTX_LEN:, :].astype(F32), cos[:, :kv_w], sin[:, :kv_w]).astype(BF16)
    kc = k_ref[0, 0:CTX_LEN, :]
    vc = v_ref[0, 0:CTX_LEN, :]
    qc = q_ref[0, 0:CTX_LEN, :].astype(F32) * scale
    q4 = jnp.concatenate([to_kv_lanes(qc, h) for h in range(SWA_HEADS)], axis=0).astype(BF16)
    s = _dot_nt(q4, kc)
    sk = sink_rows(CTX_LEN)
    mx = jnp.maximum(jnp.max(s, axis=-1, keepdims=True), sk)
    p = jnp.exp(s - mx)
    den = jnp.sum(p, axis=-1, keepdims=True) + jnp.exp(sk - mx)
    o_ref[0, 0:CTX_LEN, :] = from_kv_lanes(_dot(p.astype(BF16), vc) / den, CTX_LEN).astype(o_ref.dtype)

    skb = sink_rows(SWA_BLOCK)
    row_b = lax.broadcasted_iota(jnp.int32, (SWA_HEADS * SWA_BLOCK, 1), 0) % SWA_BLOCK
    col_b = lax.broadcasted_iota(jnp.int32, (1, kw), 1)

    def logits(n):
        ws = jnp.clip(n - 1, 0, nb - 3) * SWA_BLOCK
        qrows = pl.ds(pl.multiple_of(n * SWA_BLOCK, SWA_BLOCK), SWA_BLOCK)
        krows = pl.ds(pl.multiple_of(ws, SWA_BLOCK), kw)
        qb = jnp.concatenate([qr_ref[h, qrows, :] for h in range(SWA_HEADS)], axis=0)
        return _dot_nt(qb, kr_ref[krows, :]), _dot_nt(qb, kc), n, ws

    def attend(s_b, s_c, n, ws):
        valid = jnp.abs(n * SWA_BLOCK + row_b - (ws + col_b)) <= SWA_WINDOW
        s_b = jnp.where(valid, s_b, NEG_INF)
        mxb = jnp.maximum(jnp.maximum(jnp.max(s_b, axis=-1, keepdims=True),
                                      jnp.max(s_c, axis=-1, keepdims=True)), skb)
        p_b = jnp.exp(s_b - mxb)
        p_c = jnp.exp(s_c - mxb)
        denb = (jnp.sum(p_b, axis=-1, keepdims=True) + jnp.sum(p_c, axis=-1, keepdims=True)
                + jnp.exp(skb - mxb))
        vrows = pl.ds(pl.multiple_of(CTX_LEN + ws, SWA_BLOCK), kw)
        o4 = (_dot(p_b.astype(BF16), v_ref[0, vrows, :]) + _dot(p_c.astype(BF16), vc)) / denb
        orows = pl.ds(pl.multiple_of(CTX_LEN + n * SWA_BLOCK, SWA_BLOCK), SWA_BLOCK)
        o_ref[0, orows, :] = from_kv_lanes(o4, SWA_BLOCK).astype(o_ref.dtype)

    def block_pair(j, carry):
        first = logits(2 * j)
        second = logits(2 * j + 1)
        attend(*first)
        attend(*second)
        return carry

    lax.fori_loop(0, nb // 2, block_pair, 0)


def _swa_attention(p, cos, sin, sink):
    b, t, _ = p.shape
    seq = t - CTX_LEN
    return pl.pallas_call(
        functools.partial(_swa_kernel, t=t),
        grid=(b,),
        in_specs=[
            pl.BlockSpec((1, t, 256), lambda i: (i, 0, COL_SQ // 256)),
            pl.BlockSpec((1, t, 128), lambda i: (i, 0, COL_SK // 128)),
            pl.BlockSpec((1, t, 128), lambda i: (i, 0, COL_SV // 128)),
            pl.BlockSpec((seq, 256), lambda i: (0, 0)),
            pl.BlockSpec((seq, 256), lambda i: (0, 0)),
            pl.BlockSpec(memory_space=pltpu.SMEM),
        ],
        out_specs=pl.BlockSpec((1, t, 256), lambda i: (i, 0, 0)),
        out_shape=jax.ShapeDtypeStruct((b, t, 256), BF16),
        scratch_shapes=[pltpu.VMEM((SWA_HEADS, seq, 128), BF16), pltpu.VMEM((seq, 128), BF16)],
        compiler_params=_params(("arbitrary",)),
        name="swa_attention",
    )(p, p, p, cos, sin, sink)


def _out_proj_kernel(u_ref, yg_ref, d_ref, wg_ref, na_ref, gla_ref, swa_ref, wo_ref, g_ref,
                     modl_ref, modc_ref, x_ref, o_ref, *, tm, first):
    ys = jnp.concatenate(
        [jnp.concatenate([yg_ref[g, j, 0] for g in range(S5_GROUPS)], axis=0).T for j in range(tm // S5_TC)], axis=0)
    y = d_ref[...] * u_ref[0].astype(F32) + ys
    z = _gelu(y)
    a = z * _sigmoid(_dot(z.astype(BF16), wg_ref[...]))
    acc = _dot(a.astype(BF16), wo_ref[0:256, :])
    acc += _dot(na_ref[0], wo_ref[256:512, :])
    acc += _dot(gla_ref[0], wo_ref[512:768, :])
    acc += _dot(swa_ref[0], wo_ref[768:1024, :])
    row = (pl.program_id(1) + first) * tm + lax.broadcasted_iota(jnp.int32, (tm, 1), 0)
    gate = jnp.where(row < CTX_LEN, modc_ref[0, 2:3, :], modl_ref[0, 2:3, :])
    o_ref[0] = x_ref[0] + gate * (_rms(acc) * g_ref[...])


def _out_proj(p, yg, s5_d, w_glu, y_na, y_gla, y_swa, w_out, g_post, modl, modc, x, *, tm, skip_ctx=False):
    b, t_in, d = x.shape
    first = CTX_LEN // tm if skip_ctx else 0
    assert tm % S5_TC == 0 and (not skip_ctx or CTX_LEN % tm == 0)
    t = t_in - first * tm
    nc = tm // S5_TC
    tok = lambda w: pl.BlockSpec((1, tm, w), lambda i, j: (i, j + first, 0))
    full = lambda a: pl.BlockSpec(a.shape, lambda i, j: (0,) * a.ndim)
    chunked = pl.BlockSpec((S5_GROUPS, nc, 1, S5_GROUP_CH, S5_TC), lambda i, j: (0, j + first, i, 0, 0))
    return pl.pallas_call(
        functools.partial(_out_proj_kernel, tm=tm, first=first),
        grid=(b, t // tm),
        in_specs=[
            tok(256), chunked, full(s5_d), full(w_glu), tok(256), tok(256), tok(256),
            full(w_out), full(g_post),
            pl.BlockSpec((1, 6, d), lambda i, j: (i, 0, 0)),
            pl.BlockSpec((1, 6, d), lambda i, j: (0, 0, 0)),
            tok(d),
        ],
        out_specs=pl.BlockSpec((1, tm, d), lambda i, j: (i, j, 0)),
        out_shape=jax.ShapeDtypeStruct((b, t, d), F32),
        compiler_params=_params(("arbitrary", "arbitrary")),
        name="out_proj",
    )(p, yg, s5_d, w_glu, y_na, y_gla, y_swa, w_out, g_post, modl, modc, x)


def _ffn_down_kernel(gt_ref, vl_ref, prev_ref, next_ref, cw_ref, wd_ref, g_ref, modl_ref, modc_ref, x_ref, o_ref,
                     a_ref, *, tm, t, ctx_rows):
    h = FFN_HALO
    row0 = pl.program_id(1) * tm
    w0, w1, w2 = cw_ref[0:1, :], cw_ref[1:2, :], cw_ref[2:3, :]
    g = gt_ref[0].astype(F32)
    gp = pltpu.roll(g, 1, 0)
    gn = pltpu.roll(g, tm - 1, 0)
    a_ref[...] = _gelu((gp * w0 + g * w1 + gn * w2).astype(BF16)) * vl_ref[0]
    loc = lax.broadcasted_iota(jnp.int32, (h, 1), 0)
    has_prev = jnp.where((row0 == 0) | (row0 == ctx_rows), 0.0, 1.0)
    has_next = jnp.where((row0 + tm == ctx_rows) | (row0 + tm == t), 0.0, 1.0)
    gp_top = jnp.where(loc == 0, prev_ref[0, h - 1:h, :].astype(F32) * has_prev, gp[0:h])
    top = gp_top * w0 + g[0:h] * w1 + gn[0:h] * w2
    a_ref[0:h, :] = _gelu(top.astype(BF16)) * vl_ref[0, 0:h, :]
    gn_bot = jnp.where(loc == h - 1, next_ref[0, 0:1, :].astype(F32) * has_next, gn[tm - h:tm])
    bot = gp[tm - h:tm] * w0 + g[tm - h:tm] * w1 + gn_bot * w2
    a_ref[tm - h:tm, :] = _gelu(bot.astype(BF16)) * vl_ref[0, tm - h:tm, :]
    acc = _dot(a_ref[...], wd_ref[...])
    gate = jnp.where(row0 < ctx_rows, modc_ref[0, 5:6, :], modl_ref[0, 5:6, :])
    o_ref[0] = x_ref[0] + gate * (_rms(acc) * g_ref[...])


def _ffn_down(gv, conv_w, w_down, g_post, modl, modc, x, *, tm, ctx_rows=CTX_LEN):
    b, t, d = x.shape
    assert ctx_rows % tm == 0 and t % tm == 0 and tm > 2 * FFN_HALO
    nbh = tm // FFN_HALO
    lasth = t // FFN_HALO - 1
    return pl.pallas_call(
        functools.partial(_ffn_down_kernel, tm=tm, t=t, ctx_rows=ctx_rows),
        grid=(b, t // tm),
        in_specs=[
            pl.BlockSpec((1, tm, D_FF), lambda i, j: (i, j, 0)),
            pl.BlockSpec((1, tm, D_FF), lambda i, j: (i, j, 1)),
            pl.BlockSpec((1, FFN_HALO, D_FF), lambda i, j: (i, jnp.maximum(j * nbh - 1, 0), 0)),
            pl.BlockSpec((1, FFN_HALO, D_FF), lambda i, j: (i, jnp.minimum((j + 1) * nbh, lasth), 0)),
            pl.BlockSpec((3, D_FF), lambda i, j: (0, 0)),
            pl.BlockSpec((D_FF, d), lambda i, j: (0, 0)),
            pl.BlockSpec((1, d), lambda i, j: (0, 0)),
            pl.BlockSpec((1, 6, d), lambda i, j: (i, 0, 0)),
            pl.BlockSpec((1, 6, d), lambda i, j: (0, 0, 0)),
            pl.BlockSpec((1, tm, d), lambda i, j: (i, j, 0)),
        ],
        out_specs=pl.BlockSpec((1, tm, d), lambda i, j: (i, j, 0)),
        out_shape=jax.ShapeDtypeStruct((b, t, d), F32),
        scratch_shapes=[pltpu.VMEM((tm, D_FF), BF16)],
        compiler_params=_params(("arbitrary", "arbitrary")),
        name="ffn_down",
    )(gv, gv, gv, gv, conv_w, w_down, g_post, modl, modc, x)


def _reorder_w_in(w_in):
    sizes = (256, 256, 256, 256, 256, 256, 256, 16, 16, 256, 256, 128, 128)
    offs = np.concatenate([[0], np.cumsum(sizes)])
    seg = lambda i: w_in[..., offs[i]:offs[i + 1]]
    pad = jnp.zeros(w_in.shape[:-1] + (P_WIDTH - COL_GG - 2 * GLA_RANK,), w_in.dtype)
    return jnp.concatenate([seg(0), seg(1), seg(2), seg(3), seg(4), seg(5), seg(6), seg(9),
                            seg(10), seg(11), seg(12), seg(7), seg(8), pad], axis=-1)


def kernel(x, c, ctx, c_ctx, w_mod, b_mod, g_pre_mix, g_post_mix, g_pre_ffn, g_post_ffn, w_in, w_out, s5_lam_re, s5_lam_im, s5_b_re, s5_b_im, s5_c_re, s5_c_im, s5_log_step, s5_d, s5_w_glu, na_rpb, gla_w_gate2, gla_b_gate, gla_g_norm, swa_sink, ffn_w_up, ffn_conv, ffn_w_down):
    bsz, seq, d = x.shape
    depth = w_mod.shape[0]
    t = CTX_LEN + seq
    tm = 768

    cvec = jnp.concatenate([c, c_ctx[None, :]], axis=0)
    cvec = jnp.pad(cvec, ((0, (-cvec.shape[0]) % 8), (0, 0)))
    mod = _modulation(cvec, w_mod, b_mod).reshape(depth, cvec.shape[0], 6, d)

    w_in_r = _reorder_w_in(w_in).astype(BF16)
    w_out_b = w_out.astype(BF16)
    w_glu_b = s5_w_glu.astype(BF16)
    w_up_b = ffn_w_up.astype(BF16)
    w_down_b = ffn_w_down.astype(BF16)
    w2 = jnp.stack([jnp.pad(gla_w_gate2[:, 0], ((0, 0), (0, 128 - GLA_RANK), (0, 0))),
                    jnp.pad(gla_w_gate2[:, 1], ((0, 0), (GLA_RANK, 128 - 2 * GLA_RANK), (0, 0)))], axis=1).astype(BF16)
    cos, sin = _rope_tables(seq)

    xc = jnp.concatenate([ctx, x], axis=1)
    for l in range(depth):
        modl = mod[l, :bsz]
        modc = mod[l, bsz:bsz + 1]
        p, ug = _nm_matmul(xc, g_pre_mix[l][None], modl, modc, w_in_r[l], k_shift=0, k_scale=1, tm=tm, tn=P_WIDTH,
                           s5_chunks=True)
        tabs = _s5_prep(s5_lam_re[l], s5_lam_im[l], s5_b_re[l], s5_b_im[l], s5_c_re[l], s5_c_im[l], s5_log_step[l])
        ys = _s5_scan(ug, *tabs)
        y_na = _na_attention(p, na_rpb[l])
        y_gla = _gla_attention(p, w2[l], gla_b_gate[l], gla_g_norm[l][None])
        y_swa = _swa_attention(p, cos, sin, swa_sink[l])
        last = l == depth - 1
        ctx_rows = 0 if last else CTX_LEN
        xc = _out_proj(p, ys, s5_d[l][None], w_glu_b[l], y_na, y_gla, y_swa, w_out_b[l],
                       g_post_mix[l][None], modl, modc, xc, tm=CTX_LEN if last else tm, skip_ctx=last)
        gv = _nm_matmul(xc, g_pre_ffn[l][None], modl, modc, w_up_b[l], k_shift=3, k_scale=4,
                        tm=1024 if last else tm, tn=D_FF, ctx_rows=ctx_rows)
        xc = _ffn_down(gv, ffn_conv[l], w_down_b[l], g_post_ffn[l][None], modl, modc, xc, tm=256, ctx_rows=ctx_rows)
    return xc
```

```python
import functools
import math

import numpy as np
import jax
import jax.numpy as jnp
from jax import lax
from jax.experimental import pallas as pl
from jax.experimental.pallas import tpu as pltpu

F32 = jnp.float32
BF16 = jnp.bfloat16

D_MODEL = 1024
GRID_W = 64
CTX_LEN = 256
HEAD_DIM = 64
EPS = 1e-6
NEG_INF = -1e30

S5_CH = 256
S5_GROUP_CH = 16
S5_GROUPS = S5_CH // S5_GROUP_CH
S5_STATE = 64
S5_TC = 128
S5_K = S5_TC * S5_GROUP_CH

NA_HEADS = 4
NA_KH = 8
NA_KW = 16
NA_QROWS = 4
NA_WROWS = NA_QROWS + NA_KH

GLA_HEADS = 4
GLA_DK = 64
GLA_RANK = 16
GLA_TAU = 16.0
GLA_CHUNK = 64
GLA_GROUP = CTX_LEN // GLA_CHUNK

SWA_HEADS = 4
SWA_KV_HEADS = 2
SWA_WINDOW = 128
SWA_BLOCK = 128
ROPE_BASE = 10000.0

D_FF = 2816
FFN_HALO = 16

COL_S5, COL_NAQ, COL_NAK, COL_NAV = 0, 256, 512, 768
COL_GQ, COL_GK, COL_GV, COL_GR = 1024, 1280, 1536, 1792
COL_SQ, COL_SK, COL_SV, COL_GG = 2048, 2304, 2432, 2560
P_WIDTH = 2688

VMEM_LIMIT = 56 * 1024 * 1024


def _dot(a, b):
    return jnp.dot(a, b, preferred_element_type=F32)


def _dot_exact(a, b):
    return jnp.dot(a, b, preferred_element_type=F32, precision=lax.Precision.HIGHEST)


def _dot_nt(a, b):
    return lax.dot_general(a, b, (((1,), (1,)), ((), ())), preferred_element_type=F32)


def _dot_tn(a, b):
    return lax.dot_general(a, b, (((0,), (0,)), ((), ())), preferred_element_type=F32)


def _gelu(x):
    k = math.sqrt(2.0 / math.pi)
    return x * (0.5 + 0.5 * jnp.tanh(x * (k + (k * 0.044715) * (x * x))))


def _sigmoid(x):
    return 1.0 / (1.0 + jnp.exp(-x))


def _rms(x):
    return x * lax.rsqrt(jnp.mean(x * x, axis=-1, keepdims=True) + EPS)


def _params(sem):
    return pltpu.CompilerParams(dimension_semantics=sem, vmem_limit_bytes=VMEM_LIMIT)


def _mod_kernel(c_ref, w_ref, b_ref, o_ref):
    c = c_ref[...]
    s = (c * _sigmoid(c)).astype(BF16)
    o_ref[0] = _dot(s, w_ref[0].astype(BF16)) + b_ref[0]


def _modulation(cvec, w_mod, b_mod):
    depth, d, n = w_mod.shape
    r = cvec.shape[0]
    tn = 1536
    return pl.pallas_call(
        _mod_kernel,
        grid=(depth, n // tn),
        in_specs=[
            pl.BlockSpec((r, d), lambda l, j: (0, 0)),
            pl.BlockSpec((1, d, tn), lambda l, j: (l, 0, j)),
            pl.BlockSpec((1, 1, tn), lambda l, j: (l, 0, j)),
        ],
        out_specs=pl.BlockSpec((1, r, tn), lambda l, j: (l, 0, j)),
        out_shape=jax.ShapeDtypeStruct((depth, r, n), F32),
        compiler_params=_params(("arbitrary", "arbitrary")),
        name="modulation",
    )(cvec, w_mod, b_mod.reshape(depth, 1, n))


def _nm_matmul_kernel(x0_ref, xn_ref, g_ref, modl0_ref, modln_ref, modc_ref, w_ref, o_ref, *rest,
                      tm, nt, nk, k_shift, k_scale, ctx_rows, s5_chunks):
    ha_ref, hb_ref = rest[-2], rest[-1]
    tile = pl.program_id(0) * nt + pl.program_id(1)
    k = pl.program_id(2)
    part = tm // nk

    def normed(x, modl_ref, row0):
        y = _rms(x) * g_ref[...]
        row = row0 + lax.broadcasted_iota(jnp.int32, (x.shape[0], 1), 0)
        is_ctx = row < ctx_rows
        scale = jnp.where(is_ctx, modc_ref[0, k_scale:k_scale + 1, :], modl_ref[0, k_scale:k_scale + 1, :])
        shift = jnp.where(is_ctx, modc_ref[0, k_shift:k_shift + 1, :], modl_ref[0, k_shift:k_shift + 1, :])
        return (y * (1.0 + scale) + shift).astype(BF16)

    @pl.when((tile == 0) & (k == 0))
    def _():
        ha_ref[...] = normed(x0_ref[0], modl0_ref, 0)

    def step(cur_ref, nxt_ref):
        rows = pl.ds(pl.multiple_of(k * part, part), part)
        nxt_ref[rows, :] = normed(xn_ref[0, rows, :], modln_ref, ((pl.program_id(1) + 1) % nt) * tm + k * part)
        res = _dot(cur_ref[...], w_ref[...])
        o_ref[0] = res.astype(o_ref.dtype)
        if s5_chunks:
            ug_ref = rest[0]
            for j in range(tm // S5_TC):
                z = res[j * S5_TC:(j + 1) * S5_TC, COL_S5:COL_S5 + S5_CH].T.astype(BF16)
                for g in range(S5_GROUPS):
                    ug_ref[g, j, 0] = z[g * S5_GROUP_CH:(g + 1) * S5_GROUP_CH, :]

    @pl.when(tile % 2 == 0)
    def _():
        step(ha_ref, hb_ref)

    @pl.when(tile % 2 == 1)
    def _():
        step(hb_ref, ha_ref)


def _nm_matmul(x, g, modl, modc, w, *, k_shift, k_scale, tm, tn, ctx_rows=CTX_LEN, s5_chunks=False):
    b, t, d = x.shape
    n = w.shape[1]
    assert t % tm == 0 and n % tn == 0 and (not s5_chunks or (tn == n and tm % S5_TC == 0))
    out_specs = [pl.BlockSpec((1, tm, tn), lambda i, j, k: (i, j, k))]
    out_shape = [jax.ShapeDtypeStruct((b, t, n), BF16)]
    if s5_chunks:
        out_specs.append(pl.BlockSpec((S5_GROUPS, tm // S5_TC, 1, S5_GROUP_CH, S5_TC), lambda i, j, k: (0, j, i, 0, 0)))
        out_shape.append(jax.ShapeDtypeStruct((S5_GROUPS, t // S5_TC, b, S5_GROUP_CH, S5_TC), BF16))
    nt, nk = t // tm, n // tn
    assert (tm // nk) % 16 == 0

    def next_tile(i, j):
        s = i * nt + j + 1
        return jnp.minimum(s // nt, b - 1), s % nt

    out = pl.pallas_call(
        functools.partial(_nm_matmul_kernel, tm=tm, nt=nt, nk=nk, k_shift=k_shift, k_scale=k_scale,
                          ctx_rows=ctx_rows, s5_chunks=s5_chunks),
        grid=(b, nt, nk),
        in_specs=[
            pl.BlockSpec((1, tm, d), lambda i, j, k: (0, 0, 0)),
            pl.BlockSpec((1, tm, d), lambda i, j, k: next_tile(i, j) + (0,)),
            pl.BlockSpec((1, d), lambda i, j, k: (0, 0)),
            pl.BlockSpec((1, 6, d), lambda i, j, k: (0, 0, 0)),
            pl.BlockSpec((1, 6, d), lambda i, j, k: (next_tile(i, j)[0], 0, 0)),
            pl.BlockSpec((1, 6, d), lambda i, j, k: (0, 0, 0)),
            pl.BlockSpec((d, tn), lambda i, j, k: (0, k)),
        ],
        out_specs=out_specs,
        out_shape=out_shape,
        scratch_shapes=[pltpu.VMEM((tm, d), BF16), pltpu.VMEM((tm, d), BF16)],
        compiler_params=_params(("arbitrary", "arbitrary", "arbitrary")),
        name="norm_mod_matmul",
    )(x, x, g, modl, modl, modc, w)
    return tuple(out) if s5_chunks else out[0]


def _s5_prep_kernel(lr_ref, lc_ref, ls_ref, bt_ref, ct_ref, t_ref, s_ref, c_ref, a_ref, kap_ref):
    tc = S5_TC
    p = S5_STATE
    lane = lax.broadcasted_iota(jnp.int32, (1, tc), 1).astype(F32)
    sub = lax.broadcasted_iota(jnp.int32, (tc, 1), 0).astype(F32)
    s_idx = lax.broadcasted_iota(jnp.int32, (tc, tc), 0)
    t_idx = lax.broadcasted_iota(jnp.int32, (tc, tc), 1)
    for d in range(2):
        step = jnp.exp(ls_ref[0, d:d + 1, :])
        lre = lr_ref[0, d, 0:1, :]
        lim = lr_ref[0, d, 1:2, :]
        zr = lre * step
        zi = lim * step
        mag = jnp.exp(zr)
        nr = mag * jnp.cos(zi) - 1.0
        ni = mag * jnp.sin(zi)
        den = lre * lre + lim * lim
        cr = (nr * lre + ni * lim) / den
        ci = (ni * lre - nr * lim) / den
        btr = bt_ref[0, d, 0]
        bti = bt_ref[0, d, 1]
        bbr = cr * btr - ci * bti
        bbi = cr * bti + ci * btr
        ma = jnp.exp(zr * tc)
        ar = ma * jnp.cos(zi * tc)
        ai = ma * jnp.sin(zi * tc)
        a_ref[0, 0:1, d * 2 * p:d * 2 * p + p] = ar
        a_ref[0, 0:1, d * 2 * p + p:(d + 1) * 2 * p] = ar
        a_ref[0, 1:2, d * 2 * p:d * 2 * p + p] = -ai
        a_ref[0, 1:2, d * 2 * p + p:(d + 1) * 2 * p] = ai
        zrc = lc_ref[0, d, :, 0:1] * step
        zic = lc_ref[0, d, :, 1:2] * step

        def powers(tau):
            m = jnp.exp(zrc * tau)
            ang = zic * tau
            return m * jnp.cos(ang), m * jnp.sin(ang)

        if d == 0:
            pkr, pki = powers(lane)
            pcr, pci = powers(lane + 1.0)
            es = (tc - 1.0) - sub
        else:
            pkr, pki = powers(jnp.where(lane == 0.0, 0.0, tc - lane))
            pcr, pci = powers(tc - lane)
            es = sub
        ctr = ct_ref[0, d, 0]
        cti = ct_ref[0, d, 1]
        for co in range(S5_GROUP_CH):
            ccr = ctr[:, co:co + 1]
            cci = cti[:, co:co + 1]
            kap_ref[d, co] = (_dot_exact(bbr, ccr * pkr - cci * pki)
                              - _dot_exact(bbi, ccr * pki + cci * pkr))
            cols = slice(co * tc, (co + 1) * tc)
            c_ref[0, d * 2 * p:d * 2 * p + p, cols] = (ccr * pcr - cci * pci).astype(BF16)
            c_ref[0, d * 2 * p + p:(d + 1) * 2 * p, cols] = (-(ccr * pci + cci * pcr)).astype(BF16)
        me = jnp.exp(zr * es)
        er = me * jnp.cos(zi * es)
        ei = me * jnp.sin(zi * es)
        for cin in range(S5_GROUP_CH):
            br = bbr[cin:cin + 1, :]
            bi = bbi[cin:cin + 1, :]
            rows = slice(cin * tc, (cin + 1) * tc)
            s_ref[0, rows, d * 2 * p:d * 2 * p + p] = (er * br - ei * bi).astype(BF16)
            s_ref[0, rows, d * 2 * p + p:(d + 1) * 2 * p] = (er * bi + ei * br).astype(BF16)

    for co in range(S5_GROUP_CH):
        cols = slice(co * tc, (co + 1) * tc)

        def toeplitz(i, carry):
            cins = [i * 8 + j for j in range(8)]
            kr = [kap_ref[1, co, pl.ds(cin, 1), :] for cin in cins]
            kf = [kap_ref[0, co, pl.ds(cin, 1), :] + jnp.where(lane == 0.0, r, 0.0)
                  for cin, r in zip(cins, kr)]
            mixed = [jnp.where(t_idx + s_idx < tc, jnp.broadcast_to(f, (tc, tc)), jnp.broadcast_to(r, (tc, tc)))
                     for f, r in zip(kf, kr)]
            blk = [pltpu.roll(m, 0, 1, stride=1, stride_axis=0).astype(BF16) for m in mixed]
            for cin, b in zip(cins, blk):
                t_ref[0, pl.ds(pl.multiple_of(cin * tc, tc), tc), cols] = b
            return carry

        lax.fori_loop(0, S5_GROUP_CH // 8, toeplitz, 0)


def _s5_prep(lam_re, lam_im, b_re, b_im, c_re, c_im, log_step):
    g = S5_GROUPS
    lam = jnp.stack([lam_re, lam_im], axis=-2).astype(F32).transpose(1, 0, 2, 3)
    lam_c = lam.transpose(0, 1, 3, 2)
    ls = log_step.astype(F32).T[:, :, None]
    bt = jnp.stack([b_re, b_im], axis=2).astype(F32).transpose(1, 0, 2, 4, 3)
    ct = jnp.stack([c_re, c_im], axis=2).astype(F32).transpose(1, 0, 2, 4, 3)
    blk = lambda a: pl.BlockSpec((1,) + a.shape[1:], lambda i: (i,) + (0,) * (a.ndim - 1))
    return pl.pallas_call(
        _s5_prep_kernel,
        grid=(g,),
        in_specs=[blk(lam), blk(lam_c), blk(ls), blk(bt), blk(ct)],
        out_specs=[
            pl.BlockSpec((1, S5_K, S5_K), lambda i: (i, 0, 0)),
            pl.BlockSpec((1, S5_K, 4 * S5_STATE), lambda i: (i, 0, 0)),
            pl.BlockSpec((1, 4 * S5_STATE, S5_K), lambda i: (i, 0, 0)),
            pl.BlockSpec((1, 2, 4 * S5_STATE), lambda i: (i, 0, 0)),
        ],
        out_shape=[
            jax.ShapeDtypeStruct((g, S5_K, S5_K), BF16),
            jax.ShapeDtypeStruct((g, S5_K, 4 * S5_STATE), BF16),
            jax.ShapeDtypeStruct((g, 4 * S5_STATE, S5_K), BF16),
            jax.ShapeDtypeStruct((g, 2, 4 * S5_STATE), F32),
        ],
        scratch_shapes=[pltpu.VMEM((2, S5_GROUP_CH, S5_GROUP_CH, S5_TC), F32)],
        compiler_params=_params(("arbitrary",)),
        name="s5_prep",
    )(lam, lam_c, ls, bt, ct)


def _s5_scan_kernel(u_ref, t_ref, s_ref, c_ref, a_ref, y_ref, loc_ref, hin_ref, *, bsz, nch):
    w = 2 * S5_STATE
    u = u_ref[0]
    loc_ref[...] = _dot(u, s_ref[0])
    a1f, a2f = a_ref[0, 0:1, 0:w], a_ref[0, 1:2, 0:w]
    a1r, a2r = a_ref[0, 0:1, w:2 * w], a_ref[0, 1:2, w:2 * w]
    nctx = CTX_LEN // S5_TC

    def body(i, carry):
        hf, hr = carry
        kr = jnp.where(i < nctx, nctx - 1 - i, nch + nctx - 1 - i)
        rf = pl.ds(pl.multiple_of(i * bsz, bsz), bsz)
        rr = pl.ds(pl.multiple_of(kr * bsz, bsz), bsz)
        hin_ref[rf, 0:w] = hf
        hin_ref[rr, w:2 * w] = hr
        hf = a1f * hf + a2f * pltpu.roll(hf, S5_STATE, 1) + loc_ref[rf, 0:w]
        hr = a1r * hr + a2r * pltpu.roll(hr, S5_STATE, 1) + loc_ref[rr, w:2 * w]
        return hf, hr

    zero = jnp.zeros((bsz, w), F32)
    lax.fori_loop(0, nch, body, (zero, zero))
    y_ref[0] = _dot(u, t_ref[0]) + _dot(hin_ref[...].astype(BF16), c_ref[0])


def _s5_scan(ug, tmat, smat, cmat, amat):
    g, nch, b = ug.shape[:3]
    m = nch * b
    yg = pl.pallas_call(
        functools.partial(_s5_scan_kernel, bsz=b, nch=nch),
        grid=(g,),
        in_specs=[
            pl.BlockSpec((1, m, S5_K), lambda i: (i, 0, 0)),
            pl.BlockSpec((1, S5_K, S5_K), lambda i: (i, 0, 0)),
            pl.BlockSpec((1, S5_K, 4 * S5_STATE), lambda i: (i, 0, 0)),
            pl.BlockSpec((1, 4 * S5_STATE, S5_K), lambda i: (i, 0, 0)),
            pl.BlockSpec((1, 2, 4 * S5_STATE), lambda i: (i, 0, 0)),
        ],
        out_specs=pl.BlockSpec((1, m, S5_K), lambda i: (i, 0, 0)),
        out_shape=jax.ShapeDtypeStruct((g, m, S5_K), F32),
        scratch_shapes=[pltpu.VMEM((m, 4 * S5_STATE), F32), pltpu.VMEM((m, 4 * S5_STATE), F32)],
        compiler_params=_params(("arbitrary",)),
        name="s5_scan",
    )(ug.reshape(g, m, S5_K), tmat, smat, cmat, amat)
    return yg.reshape(g, nch, b, S5_GROUP_CH, S5_TC)


def _na_block_geometry(rows):
    geo = []
    for r0 in (0, NA_QROWS, rows - NA_QROWS):
        geo.append((r0, min(max(r0 - NA_KH // 2, 0), rows - NA_WROWS)))
    return geo


def _na_fill_bias(rpb_ref, bt_ref, rows):
    w = GRID_W
    qc = lax.broadcasted_iota(jnp.int32, (w, w), 0)
    kc = lax.broadcasted_iota(jnp.int32, (w, w), 1)
    ws = jnp.clip(qc - NA_KW // 2, 0, w - NA_KW)
    in_win = (kc >= ws) & (kc < ws + NA_KW)
    neg = jnp.full((w, w), NEG_INF, F32)
    for h in range(NA_HEADS):
        cm = []
        for dr in range(2 * NA_KH - 1):
            v = jnp.broadcast_to(rpb_ref[h, dr:dr + 1, :], (w, 128))
            v = pltpu.roll(v, 128 - (NA_KW - 1), 1, stride=1, stride_axis=0)
            cm.append(jnp.where(in_win, v[:, :w], NEG_INF))
        for kind, (r0, w0) in enumerate(_na_block_geometry(rows)):
            for qr in range(NA_QROWS):
                start = min(max(r0 + qr - NA_KH // 2, 0), rows - NA_KH)
                for kr in range(0, NA_WROWS, 2):
                    pair = []
                    for k in (kr, kr + 1):
                        inside = start <= w0 + k < start + NA_KH
                        pair.append(cm[(w0 + k) - (r0 + qr) + NA_KH - 1] if inside else neg)
                    r_lo = (h * NA_QROWS + qr) * w
                    bt_ref[kind, r_lo:r_lo + w, kr * w:(kr + 2) * w] = jnp.concatenate(pair, axis=1)


def _na_kernel(q_ref, k_ref, v_ref, rpb_ref, o_ref, bt_ref, *, rows):
    qn = NA_QROWS * GRID_W
    kn = NA_WROWS * GRID_W
    nblk = rows // NA_QROWS
    scale = HEAD_DIM ** -0.5

    @pl.when(pl.program_id(0) == 0)
    def _():
        _na_fill_bias(rpb_ref, bt_ref, rows)

    lane_head = lax.broadcasted_iota(jnp.int32, (1, NA_HEADS * HEAD_DIM), 1) // HEAD_DIM

    def stack_heads(x):
        return jnp.concatenate([jnp.where(lane_head == h, x, 0.0) for h in range(NA_HEADS)], axis=0)

    def pick_heads(o4, n):
        out = o4[0:n]
        for h in range(1, NA_HEADS):
            out = jnp.where(lane_head == h, o4[h * n:(h + 1) * n], out)
        return out

    kc = k_ref[0, 0:CTX_LEN, :]
    vc = v_ref[0, 0:CTX_LEN, :]
    s = _dot_nt(stack_heads(q_ref[0, 0:CTX_LEN, :] * scale), kc)
    p = jnp.exp(s - jnp.max(s, axis=-1, keepdims=True))
    o = _dot(p.astype(BF16), vc) / jnp.sum(p, axis=-1, keepdims=True)
    o_ref[0, 0:CTX_LEN, :] = pick_heads(o, CTX_LEN).astype(o_ref.dtype)

    def logits(i):
        w0 = jnp.clip(i * NA_QROWS - NA_KH // 2, 0, rows - NA_WROWS)
        kind = jnp.where(i > 0, 1, 0) + jnp.where(i == nblk - 1, 1, 0)
        qrows = pl.ds(pl.multiple_of(CTX_LEN + i * qn, qn), qn)
        krows = pl.ds(pl.multiple_of(CTX_LEN + w0 * GRID_W, GRID_W), kn)
        q4 = stack_heads(q_ref[0, qrows, :] * scale)
        return _dot_nt(q4, k_ref[0, krows, :]), _dot_nt(q4, kc), kind, qrows, krows

    def attend(s_nb, s_cx, kind, qrows, krows):
        p_nb, p_cx, inv = [], [], []
        for h in range(NA_HEADS):
            r = slice(h * qn, (h + 1) * qn)
            a = s_nb[r] + bt_ref[kind, r, :]
            c = s_cx[r]
            mx = jnp.maximum(jnp.max(a, axis=-1, keepdims=True), jnp.max(c, axis=-1, keepdims=True))
            ea = jnp.exp(a - mx)
            ec = jnp.exp(c - mx)
            inv.append(1.0 / (jnp.sum(ea, axis=-1, keepdims=True) + jnp.sum(ec, axis=-1, keepdims=True)))
            p_nb.append(ea.astype(BF16))
            p_cx.append(ec.astype(BF16))
        o4 = (_dot(jnp.concatenate(p_nb, axis=0), v_ref[0, krows, :])
              + _dot(jnp.concatenate(p_cx, axis=0), vc)) * jnp.concatenate(inv, axis=0)
        o_ref[0, qrows, :] = pick_heads(o4, qn).astype(o_ref.dtype)

    def block_pair(j, carry):
        first = logits(2 * j)
        second = logits(2 * j + 1)
        attend(*first)
        attend(*second)
        return carry

    lax.fori_loop(0, nblk // 2, block_pair, 0)


def _na_attention(p, rpb):
    b, t, _ = p.shape
    rows = (t - CTX_LEN) // GRID_W
    rpb_p = jnp.pad(rpb.astype(F32), ((0, 0), (0, 1), (0, 128 - rpb.shape[2])))
    cb = lambda c: (lambda i: (i, 0, c // 256))
    return pl.pallas_call(
        functools.partial(_na_kernel, rows=rows),
        grid=(b,),
        in_specs=[
            pl.BlockSpec((1, t, 256), cb(COL_NAQ)),
            pl.BlockSpec((1, t, 256), cb(COL_NAK)),
            pl.BlockSpec((1, t, 256), cb(COL_NAV)),
            pl.BlockSpec(rpb_p.shape, lambda i: (0, 0, 0)),
        ],
        out_specs=pl.BlockSpec((1, t, 256), lambda i: (i, 0, 0)),
        out_shape=jax.ShapeDtypeStruct((b, t, 256), BF16),
        scratch_shapes=[pltpu.VMEM((3, NA_HEADS * NA_QROWS * GRID_W, NA_WROWS * GRID_W), F32)],
        compiler_params=_params(("arbitrary",)),
        name="na_attention",
    )(p, p, p, rpb_p)


def _log_sigmoid(x):
    return jnp.minimum(x, 0.0) - jnp.log(1.0 + jnp.exp(-jnp.abs(x)))


def _gla_kernel(q_ref, k_ref, v_ref, r_ref, gg_ref, w2_ref, bg_ref, gn_ref, o_ref,
                la_ref, acc_ref, qd_ref, ke_ref, el_ref, st_ref, *, t):
    c = GLA_CHUNK
    ng = GLA_GROUP
    gc = ng * c
    hd = GLA_HEADS * GLA_DK
    ngrp = t // gc
    scale = GLA_DK ** -0.5
    gi = lax.broadcasted_iota(jnp.int32, (gc, gc), 0)
    gj = lax.broadcasted_iota(jnp.int32, (gc, gc), 1)
    same_chunk = (gi // c) == (gj // c)
    same_head = (lax.broadcasted_iota(jnp.int32, (hd, hd), 0) // GLA_DK
                 == lax.broadcasted_iota(jnp.int32, (hd, hd), 1) // GLA_DK)
    row_i = lax.broadcasted_iota(jnp.int32, (c, hd), 0)
    col_j = lax.broadcasted_iota(jnp.int32, (c, hd), 1) % c
    ones4 = same_chunk.astype(BF16)
    tri4 = [(same_chunk & ((gj % c) <= (gi % c))).astype(BF16), (same_chunk & ((gj % c) >= (gi % c))).astype(BF16)]
    keep = [col_j <= row_i, col_j >= row_i]
    for d in range(2):
        x = _dot(gg_ref[0], w2_ref[d]) + bg_ref[d:d + 1, :]
        la_ref[d] = _log_sigmoid(x) * (1.0 / GLA_TAU)

    def interleave(stages):
        out = [None] * len(stages)
        live = list(range(len(stages)))
        while live:
            for i in list(live):
                try:
                    out[i] = next(stages[i])
                except StopIteration:
                    live.remove(i)
        return out

    def intra_stages(d, sg):
        rows4 = pl.ds(pl.multiple_of(sg * gc, gc), gc)
        la = la_ref[d, rows4, :]
        hi = la.astype(BF16)
        lo = (la - hi.astype(F32)).astype(BF16)
        bc = _dot(tri4[d], hi) + _dot(tri4[d], lo)
        bl = _dot(ones4, hi) + _dot(ones4, lo)
        yield None
        qd = (q_ref[0, rows4, :].astype(F32) * scale * jnp.exp(bc)).astype(BF16)
        kf = k_ref[0, rows4, :].astype(F32)
        kd = (kf * jnp.exp(-bc)).astype(BF16)
        vv = v_ref[0, rows4, :]
        qd_ref[d, rows4, :] = qd
        ke_ref[d, rows4, :] = (kf * jnp.exp(bl - bc)).astype(BF16)
        el = jnp.exp(bl)
        for g in range(ng):
            el_ref[d, sg * ng + g] = el[g * c:g * c + 1, :]
        kbd = [jnp.where(same_head, jnp.concatenate([kd[g * c:(g + 1) * c]] * GLA_HEADS, axis=0), 0.0)
               for g in range(ng)]
        sc = [_dot_nt(qd[g * c:(g + 1) * c], kbd[g]) for g in range(ng)]
        yield None
        a = [jnp.where(keep[d], s, 0.0).astype(BF16) for s in sc]
        vbd = [jnp.where(same_head, jnp.concatenate([vv[g * c:(g + 1) * c]] * GLA_HEADS, axis=0), 0.0)
               for g in range(ng)]
        yield jnp.concatenate([_dot(a[g], vbd[g]) for g in range(ng)], axis=0)

    def intra(sg, carry):
        o_fwd, o_bwd = interleave([intra_stages(0, sg), intra_stages(1, sg)])
        acc_ref[pl.ds(pl.multiple_of(sg * gc, gc), gc), :] = o_fwd + o_bwd
        return carry

    lax.fori_loop(0, ngrp, intra, 0)
    st_ref[...] = jnp.zeros_like(st_ref)

    def inter_stages(d, sg):
        if d == 0:
            grp, visit = sg, list(range(ng))
        else:
            grp, visit = jnp.where(sg == 0, 0, ngrp - sg), list(range(ng - 1, -1, -1))
        rows = [pl.ds(pl.multiple_of((grp * ng + g) * c, c), c) for g in visit]
        ds = [_dot_tn(v_ref[0, r, :], ke_ref[d, r, :]) for r in rows]
        yield None
        st = st_ref[d]
        sprev = []
        for i, g in enumerate(visit):
            sprev.append(st.astype(BF16))
            st = st * el_ref[d, grp * ng + g] + jnp.where(same_head, ds[i], 0.0)
        st_ref[d] = st
        oi = [_dot_nt(qd_ref[d, r, :], sprev[i]) for i, r in enumerate(rows)]
        yield None
        for i, r in enumerate(rows):
            acc_ref[r, :] += oi[i]
        yield None

    def inter(sg, carry):
        interleave([inter_stages(0, sg), inter_stages(1, sg)])
        return carry

    lax.fori_loop(0, ngrp, inter, 0)

    acc = acc_ref[...]
    sq = acc * acc
    sq_hi = sq.astype(BF16)
    sq_lo = (sq - sq_hi.astype(F32)).astype(BF16)
    avg = jnp.where(same_head, 1.0 / GLA_DK, 0.0).astype(BF16)
    ms = _dot(sq_hi, avg) + _dot(sq_lo, avg)
    r = r_ref[0].astype(F32)
    o_ref[0] = (acc * lax.rsqrt(ms + EPS) * gn_ref[...] * (r * _sigmoid(r))).astype(o_ref.dtype)


def _gla_attention(p, w2cat, b_gate, g_norm):
    b, t, _ = p.shape
    hd = GLA_HEADS * GLA_DK
    cb = lambda c: (lambda i: (i, 0, c // 256))
    return pl.pallas_call(
        functools.partial(_gla_kernel, t=t),
        grid=(b,),
        in_specs=[
            pl.BlockSpec((1, t, 256), cb(COL_GQ)),
            pl.BlockSpec((1, t, 256), cb(COL_GK)),
            pl.BlockSpec((1, t, 256), cb(COL_GV)),
            pl.BlockSpec((1, t, 256), cb(COL_GR)),
            pl.BlockSpec((1, t, 128), lambda i: (i, 0, COL_GG // 128)),
            pl.BlockSpec((2, 128, 256), lambda i: (0, 0, 0)),
            pl.BlockSpec((2, 256), lambda i: (0, 0)),
            pl.BlockSpec((1, hd), lambda i: (0, 0)),
        ],
        out_specs=pl.BlockSpec((1, t, 256), lambda i: (i, 0, 0)),
        out_shape=jax.ShapeDtypeStruct((b, t, 256), BF16),
        scratch_shapes=[
            pltpu.VMEM((2, t, hd), F32),
            pltpu.VMEM((t, hd), F32),
            pltpu.VMEM((2, t, hd), BF16),
            pltpu.VMEM((2, t, hd), BF16),
            pltpu.VMEM((2, t // GLA_CHUNK, 1, hd), F32),
            pltpu.VMEM((2, hd, hd), F32),
        ],
        compiler_params=_params(("arbitrary",)),
        name="gla_attention",
    )(p, p, p, p, p, w2cat, b_gate, jnp.tile(g_norm.reshape(1, GLA_DK), (1, GLA_HEADS)))


def _rope_tables(seq):
    pos = np.arange(seq)
    half = HEAD_DIM // 2
    inv_freq = ROPE_BASE ** (-np.arange(0, half, 2, dtype=np.float32) / half)
    cos_l, sin_l = [], []
    for p_axis in (pos // GRID_W, pos % GRID_W):
        ang = p_axis.astype(np.float32)[:, None] * inv_freq[None, :]
        cos_l += [np.cos(ang), np.cos(ang)]
        sin_l += [-np.sin(ang), np.sin(ang)]
    cos = np.concatenate(cos_l, axis=-1).astype(np.float32)
    sin = np.concatenate(sin_l, axis=-1).astype(np.float32)
    return jnp.asarray(np.tile(cos, (1, SWA_HEADS))), jnp.asarray(np.tile(sin, (1, SWA_HEADS)))


def _rope(x, cos, sin):
    n = x.shape[-1]
    lane = lax.broadcasted_iota(jnp.int32, x.shape, 1)
    partner = jnp.where((lane % 32) < 16, pltpu.roll(x, n - 16, 1), pltpu.roll(x, 16, 1))
    return x * cos + partner * sin


def _swa_kernel(q_ref, k_ref, v_ref, cos_ref, sin_ref, sink_ref, o_ref, qr_ref, kr_ref, *, t):
    seq = t - CTX_LEN
    nb = seq // SWA_BLOCK
    grp = SWA_HEADS // SWA_KV_HEADS
    kw = 3 * SWA_BLOCK
    scale = HEAD_DIM ** -0.5
    kv_w = SWA_KV_HEADS * HEAD_DIM
    lane = lax.broadcasted_iota(jnp.int32, (1, kv_w), 1)

    def to_kv_lanes(x, h):
        slab = x[:, (h // grp) * kv_w:(h // grp + 1) * kv_w]
        if h % grp != h // grp:
            slab = pltpu.roll(slab, HEAD_DIM, 1)
        return jnp.where(lane // HEAD_DIM == h // grp, slab, 0.0)

    def from_kv_lanes(o4, n):
        slabs = []
        for pair in range(SWA_HEADS // grp):
            lo = o4[(2 * pair) * n:(2 * pair + 1) * n]
            hi = o4[(2 * pair + 1) * n:(2 * pair + 2) * n]
            if pair == 0:
                hi = pltpu.roll(hi, HEAD_DIM, 1)
            else:
                lo = pltpu.roll(lo, HEAD_DIM, 1)
            slabs.append(jnp.where(lane < HEAD_DIM, lo, hi))
        return jnp.concatenate(slabs, axis=1)

    def sink_rows(n):
        row = lax.broadcasted_iota(jnp.int32, (SWA_HEADS * n, 1), 0) // n
        sk = jnp.full((SWA_HEADS * n, 1), sink_ref[0], F32)
        for h in range(1, SWA_HEADS):
            sk = jnp.where(row == h, sink_ref[h], sk)
        return sk

    cos = cos_ref[...]
    sin = sin_ref[...]
    qrot = _rope(q_ref[0, CTX_LEN:, :].astype(F32), cos, sin) * scale
    for h in range(SWA_HEADS):
        qr_ref[h] = to_kv_lanes(qrot, h).astype(BF16)
    kr_ref[...] = _rope(k_ref[0, CTX_LEN:, :].astype(F32), cos[:, :kv_w], sin[:, :kv_w]).astype(BF16)
    kc = k_ref[0, 0:CTX_LEN, :]
    vc = v_ref[0, 0:CTX_LEN, :]
    qc = q_ref[0, 0:CTX_LEN, :].astype(F32) * scale
    q4 = jnp.concatenate([to_kv_lanes(qc, h) for h in range(SWA_HEADS)], axis=0).astype(BF16)
    s = _dot_nt(q4, kc)
    sk = sink_rows(CTX_LEN)
    mx = jnp.maximum(jnp.max(s, axis=-1, keepdims=True), sk)
    p = jnp.exp(s - mx)
    den = jnp.sum(p, axis=-1, keepdims=True) + jnp.exp(sk - mx)
    o_ref[0, 0:CTX_LEN, :] = from_kv_lanes(_dot(p.astype(BF16), vc) / den, CTX_LEN).astype(o_ref.dtype)

    skb = sink_rows(SWA_BLOCK)
    row_b = lax.broadcasted_iota(jnp.int32, (SWA_HEADS * SWA_BLOCK, 1), 0) % SWA_BLOCK
    col_b = lax.broadcasted_iota(jnp.int32, (1, kw), 1)

    def logits(n):
        ws = jnp.clip(n - 1, 0, nb - 3) * SWA_BLOCK
        qrows = pl.ds(pl.multiple_of(n * SWA_BLOCK, SWA_BLOCK), SWA_BLOCK)
        krows = pl.ds(pl.multiple_of(ws, SWA_BLOCK), kw)
        qb = jnp.concatenate([qr_ref[h, qrows, :] for h in range(SWA_HEADS)], axis=0)
        return _dot_nt(qb, kr_ref[krows, :]), _dot_nt(qb, kc), n, ws

    def attend(s_b, s_c, n, ws):
        valid = jnp.abs(n * SWA_BLOCK + row_b - (ws + col_b)) <= SWA_WINDOW
        s_b = jnp.where(valid, s_b, NEG_INF)
        mxb = jnp.maximum(jnp.maximum(jnp.max(s_b, axis=-1, keepdims=True),
                                      jnp.max(s_c, axis=-1, keepdims=True)), skb)
        p_b = jnp.exp(s_b - mxb)
        p_c = jnp.exp(s_c - mxb)
        denb = (jnp.sum(p_b, axis=-1, keepdims=True) + jnp.sum(p_c, axis=-1, keepdims=True)
                + jnp.exp(skb - mxb))
        vrows = pl.ds(pl.multiple_of(CTX_LEN + ws, SWA_BLOCK), kw)
        o4 = (_dot(p_b.astype(BF16), v_ref[0, vrows, :]) + _dot(p_c.astype(BF16), vc)) / denb
        orows = pl.ds(pl.multiple_of(CTX_LEN + n * SWA_BLOCK, SWA_BLOCK), SWA_BLOCK)
        o_ref[0, orows, :] = from_kv_lanes(o4, SWA_BLOCK).astype(o_ref.dtype)

    def block_pair(j, carry):
        first = logits(2 * j)
        second = logits(2 * j + 1)
        attend(*first)
        attend(*second)
        return carry

    lax.fori_loop(0, nb // 2, block_pair, 0)


def _swa_attention(p, cos, sin, sink):
    b, t, _ = p.shape
    seq = t - CTX_LEN
    return pl.pallas_call(
        functools.partial(_swa_kernel, t=t),
        grid=(b,),
        in_specs=[
            pl.BlockSpec((1, t, 256), lambda i: (i, 0, COL_SQ // 256)),
            pl.BlockSpec((1, t, 128), lambda i: (i, 0, COL_SK // 128)),
            pl.BlockSpec((1, t, 128), lambda i: (i, 0, COL_SV // 128)),
            pl.BlockSpec((seq, 256), lambda i: (0, 0)),
            pl.BlockSpec((seq, 256), lambda i: (0, 0)),
            pl.BlockSpec(memory_space=pltpu.SMEM),
        ],
        out_specs=pl.BlockSpec((1, t, 256), lambda i: (i, 0, 0)),
        out_shape=jax.ShapeDtypeStruct((b, t, 256), BF16),
        scratch_shapes=[pltpu.VMEM((SWA_HEADS, seq, 128), BF16), pltpu.VMEM((seq, 128), BF16)],
        compiler_params=_params(("arbitrary",)),
        name="swa_attention",
    )(p, p, p, cos, sin, sink)


def _out_proj_kernel(u_ref, yg_ref, d_ref, wg_ref, na_ref, gla_ref, swa_ref, wo_ref, g_ref,
                     modl_ref, modc_ref, x_ref, o_ref, *, tm, first):
    ys = jnp.concatenate(
        [jnp.concatenate([yg_ref[g, j, 0] for g in range(S5_GROUPS)], axis=0).T for j in range(tm // S5_TC)], axis=0)
    y = d_ref[...] * u_ref[0].astype(F32) + ys
    z = _gelu(y)
    a = z * _sigmoid(_dot(z.astype(BF16), wg_ref[...]))
    acc = _dot(jnp.concatenate([a.astype(BF16), na_ref[0], gla_ref[0], swa_ref[0]], axis=1), wo_ref[...])
    row = (pl.program_id(1) + first) * tm + lax.broadcasted_iota(jnp.int32, (tm, 1), 0)
    gate = jnp.where(row < CTX_LEN, modc_ref[0, 2:3, :], modl_ref[0, 2:3, :])
    o_ref[0] = x_ref[0] + gate * (_rms(acc) * g_ref[...])


def _out_proj(p, yg, s5_d, w_glu, y_na, y_gla, y_swa, w_out, g_post, modl, modc, x, *, tm, skip_ctx=False):
    b, t_in, d = x.shape
    first = CTX_LEN // tm if skip_ctx else 0
    assert tm % S5_TC == 0 and (not skip_ctx or CTX_LEN % tm == 0)
    t = t_in - first * tm
    nc = tm // S5_TC
    tok = lambda w: pl.BlockSpec((1, tm, w), lambda i, j: (i, j + first, 0))
    full = lambda a: pl.BlockSpec(a.shape, lambda i, j: (0,) * a.ndim)
    chunked = pl.BlockSpec((S5_GROUPS, nc, 1, S5_GROUP_CH, S5_TC), lambda i, j: (0, j + first, i, 0, 0))
    return pl.pallas_call(
        functools.partial(_out_proj_kernel, tm=tm, first=first),
        grid=(b, t // tm),
        in_specs=[
            tok(256), chunked, full(s5_d), full(w_glu), tok(256), tok(256), tok(256),
            full(w_out), full(g_post),
            pl.BlockSpec((1, 6, d), lambda i, j: (i, 0, 0)),
            pl.BlockSpec((1, 6, d), lambda i, j: (0, 0, 0)),
            tok(d),
        ],
        out_specs=pl.BlockSpec((1, tm, d), lambda i, j: (i, j, 0)),
        out_shape=jax.ShapeDtypeStruct((b, t, d), F32),
        compiler_params=_params(("arbitrary", "arbitrary")),
        name="out_proj",
    )(p, yg, s5_d, w_glu, y_na, y_gla, y_swa, w_out, g_post, modl, modc, x)


def _ffn_down_kernel(gt_ref, vl_ref, prev_ref, next_ref, cw_ref, wd_ref, g_ref, modl_ref, modc_ref, x_ref, o_ref,
                     a_ref, *, tm, t, ctx_rows):
    h = FFN_HALO
    row0 = pl.program_id(1) * tm
    w0, w1, w2 = cw_ref[0:1, :], cw_ref[1:2, :], cw_ref[2:3, :]
    g = gt_ref[0].astype(F32)
    gp = pltpu.roll(g, 1, 0)
    gn = pltpu.roll(g, tm - 1, 0)
    a_ref[...] = _gelu((gp * w0 + g * w1 + gn * w2).astype(BF16)) * vl_ref[0]
    loc = lax.broadcasted_iota(jnp.int32, (h, 1), 0)
    has_prev = jnp.where((row0 == 0) | (row0 == ctx_rows), 0.0, 1.0)
    has_next = jnp.where((row0 + tm == ctx_rows) | (row0 + tm == t), 0.0, 1.0)
    gp_top = jnp.where(loc == 0, prev_ref[0, h - 1:h, :].astype(F32) * has_prev, gp[0:h])
    top = gp_top * w0 + g[0:h] * w1 + gn[0:h] * w2
    a_ref[0:h, :] = _gelu(top.astype(BF16)) * vl_ref[0, 0:h, :]
    gn_bot = jnp.where(loc == h - 1, next_ref[0, 0:1, :].astype(F32) * has_next, gn[tm - h:tm])
    bot = gp[tm - h:tm] * w0 + g[tm - h:tm] * w1 + gn_bot * w2
    a_ref[tm - h:tm, :] = _gelu(bot.astype(BF16)) * vl_ref[0, tm - h:tm, :]
    acc = _dot(a_ref[...], wd_ref[...])
    gate = jnp.where(row0 < ctx_rows, modc_ref[0, 5:6, :], modl_ref[0, 5:6, :])
    o_ref[0] = x_ref[0] + gate * (_rms(acc) * g_ref[...])


def _ffn_down(gv, conv_w, w_down, g_post, modl, modc, x, *, tm, ctx_rows=CTX_LEN):
    b, t, d = x.shape
    assert ctx_rows % tm == 0 and t % tm == 0 and tm > 2 * FFN_HALO
    nbh = tm // FFN_HALO
    lasth = t // FFN_HALO - 1
    return pl.pallas_call(
        functools.partial(_ffn_down_kernel, tm=tm, t=t, ctx_rows=ctx_rows),
        grid=(b, t // tm),
        in_specs=[
            pl.BlockSpec((1, tm, D_FF), lambda i, j: (i, j, 0)),
            pl.BlockSpec((1, tm, D_FF), lambda i, j: (i, j, 1)),
            pl.BlockSpec((1, FFN_HALO, D_FF), lambda i, j: (i, jnp.maximum(j * nbh - 1, 0), 0)),
            pl.BlockSpec((1, FFN_HALO, D_FF), lambda i, j: (i, jnp.minimum((j + 1) * nbh, lasth), 0)),
            pl.BlockSpec((3, D_FF), lambda i, j: (0, 0)),
            pl.BlockSpec((D_FF, d), lambda i, j: (0, 0)),
            pl.BlockSpec((1, d), lambda i, j: (0, 0)),
            pl.BlockSpec((1, 6, d), lambda i, j: (i, 0, 0)),
            pl.BlockSpec((1, 6, d), lambda i, j: (0, 0, 0)),
            pl.BlockSpec((1, tm, d), lambda i, j: (i, j, 0)),
        ],
        out_specs=pl.BlockSpec((1, tm, d), lambda i, j: (i, j, 0)),
        out_shape=jax.ShapeDtypeStruct((b, t, d), F32),
        scratch_shapes=[pltpu.VMEM((tm, D_FF), BF16)],
        compiler_params=_params(("arbitrary", "arbitrary")),
        name="ffn_down",
    )(gv, gv, gv, gv, conv_w, w_down, g_post, modl, modc, x)


def _reorder_w_in(w_in):
    sizes = (256, 256, 256, 256, 256, 256, 256, 16, 16, 256, 256, 128, 128)
    offs = np.concatenate([[0], np.cumsum(sizes)])
    seg = lambda i: w_in[..., offs[i]:offs[i + 1]]
    pad = jnp.zeros(w_in.shape[:-1] + (P_WIDTH - COL_GG - 2 * GLA_RANK,), w_in.dtype)
    return jnp.concatenate([seg(0), seg(1), seg(2), seg(3), seg(4), seg(5), seg(6), seg(9),
                            seg(10), seg(11), seg(12), seg(7), seg(8), pad], axis=-1)


def kernel(x, c, ctx, c_ctx, w_mod, b_mod, g_pre_mix, g_post_mix, g_pre_ffn, g_post_ffn, w_in, w_out, s5_lam_re, s5_lam_im, s5_b_re, s5_b_im, s5_c_re, s5_c_im, s5_log_step, s5_d, s5_w_glu, na_rpb, gla_w_gate2, gla_b_gate, gla_g_norm, swa_sink, ffn_w_up, ffn_conv, ffn_w_down):
    bsz, seq, d = x.shape
    depth = w_mod.shape[0]
    t = CTX_LEN + seq
    tm = 768

    cvec = jnp.concatenate([c, c_ctx[None, :]], axis=0)
    cvec = jnp.pad(cvec, ((0, (-cvec.shape[0]) % 8), (0, 0)))
    mod = _modulation(cvec, w_mod, b_mod).reshape(depth, cvec.shape[0], 6, d)

    w_in_r = _reorder_w_in(w_in).astype(BF16)
    w_out_b = w_out.astype(BF16)
    w_glu_b = s5_w_glu.astype(BF16)
    w_up_b = ffn_w_up.astype(BF16)
    w_down_b = ffn_w_down.astype(BF16)
    w2 = jnp.stack([jnp.pad(gla_w_gate2[:, 0], ((0, 0), (0, 128 - GLA_RANK), (0, 0))),
                    jnp.pad(gla_w_gate2[:, 1], ((0, 0), (GLA_RANK, 128 - 2 * GLA_RANK), (0, 0)))], axis=1).astype(BF16)
    cos, sin = _rope_tables(seq)

    xc = jnp.concatenate([ctx, x], axis=1)
    for l in range(depth):
        modl = mod[l, :bsz]
        modc = mod[l, bsz:bsz + 1]
        p, ug = _nm_matmul(xc, g_pre_mix[l][None], modl, modc, w_in_r[l], k_shift=0, k_scale=1, tm=tm, tn=P_WIDTH,
                           s5_chunks=True)
        tabs = _s5_prep(s5_lam_re[l], s5_lam_im[l], s5_b_re[l], s5_b_im[l], s5_c_re[l], s5_c_im[l], s5_log_step[l])
        ys = _s5_scan(ug, *tabs)
        y_na = _na_attention(p, na_rpb[l])
        y_gla = _gla_attention(p, w2[l], gla_b_gate[l], gla_g_norm[l][None])
        y_swa = _swa_attention(p, cos, sin, swa_sink[l])
        last = l == depth - 1
        ctx_rows = 0 if last else CTX_LEN
        xc = _out_proj(p, ys, s5_d[l][None], w_glu_b[l], y_na, y_gla, y_swa, w_out_b[l],
                       g_post_mix[l][None], modl, modc, xc, tm=CTX_LEN if last else tm, skip_ctx=last)
        gv = _nm_matmul(xc, g_pre_ffn[l][None], modl, modc, w_up_b[l], k_shift=3, k_scale=4,
                        tm=1024 if last else tm, tn=D_FF, ctx_rows=ctx_rows)
        xc = _ffn_down(gv, ffn_conv[l], w_down_b[l], g_post_ffn[l][None], modl, modc, xc, tm=256, ctx_rows=ctx_rows)
    return xc
```

```python
import functools
import math

import numpy as np
import jax
import jax.numpy as jnp
from jax import lax
from jax.experimental import pallas as pl
from jax.experimental.pallas import tpu as pltpu

F32 = jnp.float32
BF16 = jnp.bfloat16

D_MODEL = 1024
GRID_W = 64
CTX_LEN = 256
HEAD_DIM = 64
EPS = 1e-6
NEG_INF = -1e30

S5_CH = 256
S5_GROUP_CH = 16
S5_GROUPS = S5_CH // S5_GROUP_CH
S5_STATE = 64
S5_TC = 128
S5_K = S5_TC * S5_GROUP_CH

NA_HEADS = 4
NA_KH = 8
NA_KW = 16
NA_QROWS = 4
NA_WROWS = NA_QROWS + NA_KH

GLA_HEADS = 4
GLA_DK = 64
GLA_RANK = 16
GLA_TAU = 16.0
GLA_CHUNK = 64
GLA_GROUP = CTX_LEN // GLA_CHUNK

SWA_HEADS = 4
SWA_KV_HEADS = 2
SWA_WINDOW = 128
SWA_BLOCK = 128
ROPE_BASE = 10000.0

D_FF = 2816
FFN_HALO = 16

COL_S5, COL_NAQ, COL_NAK, COL_NAV = 0, 256, 512, 768
COL_GQ, COL_GK, COL_GV, COL_GR = 1024, 1280, 1536, 1792
COL_SQ, COL_SK, COL_SV, COL_GG = 2048, 2304, 2432, 2560
P_WIDTH = 2688

VMEM_LIMIT = 56 * 1024 * 1024


def _dot(a, b):
    return jnp.dot(a, b, preferred_element_type=F32)


def _dot_exact(a, b):
    return jnp.dot(a, b, preferred_element_type=F32, precision=lax.Precision.HIGHEST)


def _dot_nt(a, b):
    return lax.dot_general(a, b, (((1,), (1,)), ((), ())), preferred_element_type=F32)


def _dot_tn(a, b):
    return lax.dot_general(a, b, (((0,), (0,)), ((), ())), preferred_element_type=F32)


def _gelu(x):
    k = math.sqrt(2.0 / math.pi)
    return x * (0.5 + 0.5 * jnp.tanh(x * (k + (k * 0.044715) * (x * x))))


def _sigmoid(x):
    return 1.0 / (1.0 + jnp.exp(-x))


def _rms(x):
    return x * lax.rsqrt(jnp.mean(x * x, axis=-1, keepdims=True) + EPS)


def _params(sem):
    return pltpu.CompilerParams(dimension_semantics=sem, vmem_limit_bytes=VMEM_LIMIT)


def _mod_kernel(c_ref, w_ref, b_ref, o_ref):
    c = c_ref[...]
    s = (c * _sigmoid(c)).astype(BF16)
    o_ref[0] = _dot(s, w_ref[0].astype(BF16)) + b_ref[0]


def _modulation(cvec, w_mod, b_mod):
    depth, d, n = w_mod.shape
    r = cvec.shape[0]
    tn = 1536
    return pl.pallas_call(
        _mod_kernel,
        grid=(depth, n // tn),
        in_specs=[
            pl.BlockSpec((r, d), lambda l, j: (0, 0)),
            pl.BlockSpec((1, d, tn), lambda l, j: (l, 0, j)),
            pl.BlockSpec((1, 1, tn), lambda l, j: (l, 0, j)),
        ],
        out_specs=pl.BlockSpec((1, r, tn), lambda l, j: (l, 0, j)),
        out_shape=jax.ShapeDtypeStruct((depth, r, n), F32),
        compiler_params=_params(("arbitrary", "arbitrary")),
        name="modulation",
    )(cvec, w_mod, b_mod.reshape(depth, 1, n))


def _nm_matmul_kernel(x0_ref, xn_ref, g_ref, modl0_ref, modln_ref, modc_ref, w_ref, o_ref, *rest,
                      tm, nt, nk, k_shift, k_scale, ctx_rows, s5_chunks):
    ha_ref, hb_ref = rest[-2], rest[-1]
    tile = pl.program_id(0) * nt + pl.program_id(1)
    k = pl.program_id(2)
    part = tm // nk

    def normed(x, modl_ref, row0):
        y = _rms(x) * g_ref[...]
        row = row0 + lax.broadcasted_iota(jnp.int32, (x.shape[0], 1), 0)
        is_ctx = row < ctx_rows
        scale = jnp.where(is_ctx, modc_ref[0, k_scale:k_scale + 1, :], modl_ref[0, k_scale:k_scale + 1, :])
        shift = jnp.where(is_ctx, modc_ref[0, k_shift:k_shift + 1, :], modl_ref[0, k_shift:k_shift + 1, :])
        return (y * (1.0 + scale) + shift).astype(BF16)

    @pl.when((tile == 0) & (k == 0))
    def _():
        ha_ref[...] = normed(x0_ref[0], modl0_ref, 0)

    def step(cur_ref, nxt_ref):
        rows = pl.ds(pl.multiple_of(k * part, part), part)
        nxt_ref[rows, :] = normed(xn_ref[0, rows, :], modln_ref, ((pl.program_id(1) + 1) % nt) * tm + k * part)
        res = _dot(cur_ref[...], w_ref[...])
        o_ref[0] = res.astype(o_ref.dtype)
        if s5_chunks:
            ug_ref = rest[0]
            for j in range(tm // S5_TC):
                z = res[j * S5_TC:(j + 1) * S5_TC, COL_S5:COL_S5 + S5_CH].T.astype(BF16)
                for g in range(S5_GROUPS):
                    ug_ref[g, j, 0] = z[g * S5_GROUP_CH:(g + 1) * S5_GROUP_CH, :]

    @pl.when(tile % 2 == 0)
    def _():
        step(ha_ref, hb_ref)

    @pl.when(tile % 2 == 1)
    def _():
        step(hb_ref, ha_ref)


def _nm_matmul(x, g, modl, modc, w, *, k_shift, k_scale, tm, tn, ctx_rows=CTX_LEN, s5_chunks=False):
    b, t, d = x.shape
    n = w.shape[1]
    assert t % tm == 0 and n % tn == 0 and (not s5_chunks or (tn == n and tm % S5_TC == 0))
    out_specs = [pl.BlockSpec((1, tm, tn), lambda i, j, k: (i, j, k))]
    out_shape = [jax.ShapeDtypeStruct((b, t, n), BF16)]
    if s5_chunks:
        out_specs.append(pl.BlockSpec((S5_GROUPS, tm // S5_TC, 1, S5_GROUP_CH, S5_TC), lambda i, j, k: (0, j, i, 0, 0)))
        out_shape.append(jax.ShapeDtypeStruct((S5_GROUPS, t // S5_TC, b, S5_GROUP_CH, S5_TC), BF16))
    nt, nk = t // tm, n // tn
    assert (tm // nk) % 16 == 0

    def next_tile(i, j):
        s = i * nt + j + 1
        return jnp.minimum(s // nt, b - 1), s % nt

    out = pl.pallas_call(
        functools.partial(_nm_matmul_kernel, tm=tm, nt=nt, nk=nk, k_shift=k_shift, k_scale=k_scale,
                          ctx_rows=ctx_rows, s5_chunks=s5_chunks),
        grid=(b, nt, nk),
        in_specs=[
            pl.BlockSpec((1, tm, d), lambda i, j, k: (0, 0, 0)),
            pl.BlockSpec((1, tm, d), lambda i, j, k: next_tile(i, j) + (0,)),
            pl.BlockSpec((1, d), lambda i, j, k: (0, 0)),
            pl.BlockSpec((1, 6, d), lambda i, j, k: (0, 0, 0)),
            pl.BlockSpec((1, 6, d), lambda i, j, k: (next_tile(i, j)[0], 0, 0)),
            pl.BlockSpec((1, 6, d), lambda i, j, k: (0, 0, 0)),
            pl.BlockSpec((d, tn), lambda i, j, k: (0, k)),
        ],
        out_specs=out_specs,
        out_shape=out_shape,
        scratch_shapes=[pltpu.VMEM((tm, d), BF16), pltpu.VMEM((tm, d), BF16)],
        compiler_params=_params(("arbitrary", "arbitrary", "arbitrary")),
        name="norm_mod_matmul",
    )(x, x, g, modl, modl, modc, w)
    return tuple(out) if s5_chunks else out[0]


def _s5_prep_kernel(lr_ref, lc_ref, ls_ref, bt_ref, ct_ref, t_ref, s_ref, c_ref, a_ref, kap_ref):
    tc = S5_TC
    p = S5_STATE
    lane = lax.broadcasted_iota(jnp.int32, (1, tc), 1).astype(F32)
    sub = lax.broadcasted_iota(jnp.int32, (tc, 1), 0).astype(F32)
    s_idx = lax.broadcasted_iota(jnp.int32, (tc, tc), 0)
    t_idx = lax.broadcasted_iota(jnp.int32, (tc, tc), 1)
    for d in range(2):
        step = jnp.exp(ls_ref[0, d:d + 1, :])
        lre = lr_ref[0, d, 0:1, :]
        lim = lr_ref[0, d, 1:2, :]
        zr = lre * step
        zi = lim * step
        mag = jnp.exp(zr)
        nr = mag * jnp.cos(zi) - 1.0
        ni = mag * jnp.sin(zi)
        den = lre * lre + lim * lim
        cr = (nr * lre + ni * lim) / den
        ci = (ni * lre - nr * lim) / den
        btr = bt_ref[0, d, 0]
        bti = bt_ref[0, d, 1]
        bbr = cr * btr - ci * bti
        bbi = cr * bti + ci * btr
        ma = jnp.exp(zr * tc)
        ar = ma * jnp.cos(zi * tc)
        ai = ma * jnp.sin(zi * tc)
        a_ref[0, 0:1, d * 2 * p:d * 2 * p + p] = ar
        a_ref[0, 0:1, d * 2 * p + p:(d + 1) * 2 * p] = ar
        a_ref[0, 1:2, d * 2 * p:d * 2 * p + p] = -ai
        a_ref[0, 1:2, d * 2 * p + p:(d + 1) * 2 * p] = ai
        zrc = lc_ref[0, d, :, 0:1] * step
        zic = lc_ref[0, d, :, 1:2] * step

        def powers(tau):
            m = jnp.exp(zrc * tau)
            ang = zic * tau
            return m * jnp.cos(ang), m * jnp.sin(ang)

        if d == 0:
            pkr, pki = powers(lane)
            pcr, pci = powers(lane + 1.0)
            es = (tc - 1.0) - sub
        else:
            pkr, pki = powers(jnp.where(lane == 0.0, 0.0, tc - lane))
            pcr, pci = powers(tc - lane)
            es = sub
        ctr = ct_ref[0, d, 0]
        cti = ct_ref[0, d, 1]
        for co in range(S5_GROUP_CH):
            ccr = ctr[:, co:co + 1]
            cci = cti[:, co:co + 1]
            kap_ref[d, co] = (_dot_exact(bbr, ccr * pkr - cci * pki)
                              - _dot_exact(bbi, ccr * pki + cci * pkr))
            cols = slice(co * tc, (co + 1) * tc)
            c_ref[0, d * 2 * p:d * 2 * p + p, cols] = (ccr * pcr - cci * pci).astype(BF16)
            c_ref[0, d * 2 * p + p:(d + 1) * 2 * p, cols] = (-(ccr * pci + cci * pcr)).astype(BF16)
        me = jnp.exp(zr * es)
        er = me * jnp.cos(zi * es)
        ei = me * jnp.sin(zi * es)
        for cin in range(S5_GROUP_CH):
            br = bbr[cin:cin + 1, :]
            bi = bbi[cin:cin + 1, :]
            rows = slice(cin * tc, (cin + 1) * tc)
            s_ref[0, rows, d * 2 * p:d * 2 * p + p] = (er * br - ei * bi).astype(BF16)
            s_ref[0, rows, d * 2 * p + p:(d + 1) * 2 * p] = (er * bi + ei * br).astype(BF16)

    for co in range(S5_GROUP_CH):
        cols = slice(co * tc, (co + 1) * tc)

        def toeplitz(i, carry):
            cins = [i * 8 + j for j in range(8)]
            kr = [kap_ref[1, co, pl.ds(cin, 1), :] for cin in cins]
            kf = [kap_ref[0, co, pl.ds(cin, 1), :] + jnp.where(lane == 0.0, r, 0.0)
                  for cin, r in zip(cins, kr)]
            mixed = [jnp.where(t_idx + s_idx < tc, jnp.broadcast_to(f, (tc, tc)), jnp.broadcast_to(r, (tc, tc)))
                     for f, r in zip(kf, kr)]
            blk = [pltpu.roll(m, 0, 1, stride=1, stride_axis=0).astype(BF16) for m in mixed]
            for cin, b in zip(cins, blk):
                t_ref[0, pl.ds(pl.multiple_of(cin * tc, tc), tc), cols] = b
            return carry

        lax.fori_loop(0, S5_GROUP_CH // 8, toeplitz, 0)


def _s5_prep(lam_re, lam_im, b_re, b_im, c_re, c_im, log_step):
    g = S5_GROUPS
    lam = jnp.stack([lam_re, lam_im], axis=-2).astype(F32).transpose(1, 0, 2, 3)
    lam_c = lam.transpose(0, 1, 3, 2)
    ls = log_step.astype(F32).T[:, :, None]
    bt = jnp.stack([b_re, b_im], axis=2).astype(F32).transpose(1, 0, 2, 4, 3)
    ct = jnp.stack([c_re, c_im], axis=2).astype(F32).transpose(1, 0, 2, 4, 3)
    blk = lambda a: pl.BlockSpec((1,) + a.shape[1:], lambda i: (i,) + (0,) * (a.ndim - 1))
    return pl.pallas_call(
        _s5_prep_kernel,
        grid=(g,),
        in_specs=[blk(lam), blk(lam_c), blk(ls), blk(bt), blk(ct)],
        out_specs=[
            pl.BlockSpec((1, S5_K, S5_K), lambda i: (i, 0, 0)),
            pl.BlockSpec((1, S5_K, 4 * S5_STATE), lambda i: (i, 0, 0)),
            pl.BlockSpec((1, 4 * S5_STATE, S5_K), lambda i: (i, 0, 0)),
            pl.BlockSpec((1, 2, 4 * S5_STATE), lambda i: (i, 0, 0)),
        ],
        out_shape=[
            jax.ShapeDtypeStruct((g, S5_K, S5_K), BF16),
            jax.ShapeDtypeStruct((g, S5_K, 4 * S5_STATE), BF16),
            jax.ShapeDtypeStruct((g, 4 * S5_STATE, S5_K), BF16),
            jax.ShapeDtypeStruct((g, 2, 4 * S5_STATE), F32),
        ],
        scratch_shapes=[pltpu.VMEM((2, S5_GROUP_CH, S5_GROUP_CH, S5_TC), F32)],
        compiler_params=_params(("arbitrary",)),
        name="s5_prep",
    )(lam, lam_c, ls, bt, ct)


def _s5_scan_kernel(u_ref, t_ref, s_ref, c_ref, a_ref, y_ref, loc_ref, hin_ref, *, bsz, nch):
    w = 2 * S5_STATE
    u = u_ref[0]
    loc_ref[...] = _dot(u, s_ref[0])
    a1f, a2f = a_ref[0, 0:1, 0:w], a_ref[0, 1:2, 0:w]
    a1r, a2r = a_ref[0, 0:1, w:2 * w], a_ref[0, 1:2, w:2 * w]
    nctx = CTX_LEN // S5_TC

    def body(i, carry):
        hf, hr = carry
        kr = jnp.where(i < nctx, nctx - 1 - i, nch + nctx - 1 - i)
        rf = pl.ds(pl.multiple_of(i * bsz, bsz), bsz)
        rr = pl.ds(pl.multiple_of(kr * bsz, bsz), bsz)
        hin_ref[rf, 0:w] = hf
        hin_ref[rr, w:2 * w] = hr
        hf = a1f * hf + a2f * pltpu.roll(hf, S5_STATE, 1) + loc_ref[rf, 0:w]
        hr = a1r * hr + a2r * pltpu.roll(hr, S5_STATE, 1) + loc_ref[rr, w:2 * w]
        return hf, hr

    zero = jnp.zeros((bsz, w), F32)
    lax.fori_loop(0, nch, body, (zero, zero))
    y_ref[0] = _dot(u, t_ref[0]) + _dot(hin_ref[...].astype(BF16), c_ref[0])


def _s5_scan(ug, tmat, smat, cmat, amat):
    g, nch, b = ug.shape[:3]
    m = nch * b
    yg = pl.pallas_call(
        functools.partial(_s5_scan_kernel, bsz=b, nch=nch),
        grid=(g,),
        in_specs=[
            pl.BlockSpec((1, m, S5_K), lambda i: (i, 0, 0)),
            pl.BlockSpec((1, S5_K, S5_K), lambda i: (i, 0, 0)),
            pl.BlockSpec((1, S5_K, 4 * S5_STATE), lambda i: (i, 0, 0)),
            pl.BlockSpec((1, 4 * S5_STATE, S5_K), lambda i: (i, 0, 0)),
            pl.BlockSpec((1, 2, 4 * S5_STATE), lambda i: (i, 0, 0)),
        ],
        out_specs=pl.BlockSpec((1, m, S5_K), lambda i: (i, 0, 0)),
        out_shape=jax.ShapeDtypeStruct((g, m, S5_K), F32),
        scratch_shapes=[pltpu.VMEM((m, 4 * S5_STATE), F32), pltpu.VMEM((m, 4 * S5_STATE), F32)],
        compiler_params=_params(("arbitrary",)),
        name="s5_scan",
    )(ug.reshape(g, m, S5_K), tmat, smat, cmat, amat)
    return yg.reshape(g, nch, b, S5_GROUP_CH, S5_TC)


def _na_block_geometry(rows):
    geo = []
    for r0 in (0, NA_QROWS, rows - NA_QROWS):
        geo.append((r0, min(max(r0 - NA_KH // 2, 0), rows - NA_WROWS)))
    return geo


def _na_fill_bias(rpb_ref, bt_ref, rows):
    w = GRID_W
    qc = lax.broadcasted_iota(jnp.int32, (w, w), 0)
    kc = lax.broadcasted_iota(jnp.int32, (w, w), 1)
    ws = jnp.clip(qc - NA_KW // 2, 0, w - NA_KW)
    in_win = (kc >= ws) & (kc < ws + NA_KW)
    neg = jnp.full((w, w), NEG_INF, F32)
    for h in range(NA_HEADS):
        cm = []
        for dr in range(2 * NA_KH - 1):
            v = jnp.broadcast_to(rpb_ref[h, dr:dr + 1, :], (w, 128))
            v = pltpu.roll(v, 128 - (NA_KW - 1), 1, stride=1, stride_axis=0)
            cm.append(jnp.where(in_win, v[:, :w], NEG_INF))
        for kind, (r0, w0) in enumerate(_na_block_geometry(rows)):
            for qr in range(NA_QROWS):
                start = min(max(r0 + qr - NA_KH // 2, 0), rows - NA_KH)
                for kr in range(0, NA_WROWS, 2):
                    pair = []
                    for k in (kr, kr + 1):
                        inside = start <= w0 + k < start + NA_KH
                        pair.append(cm[(w0 + k) - (r0 + qr) + NA_KH - 1] if inside else neg)
                    r_lo = (h * NA_QROWS + qr) * w
                    bt_ref[kind, r_lo:r_lo + w, kr * w:(kr + 2) * w] = jnp.concatenate(pair, axis=1)


def _na_kernel(q_ref, k_ref, v_ref, rpb_ref, o_ref, bt_ref, *, rows):
    qn = NA_QROWS * GRID_W
    kn = NA_WROWS * GRID_W
    nblk = rows // NA_QROWS
    scale = HEAD_DIM ** -0.5

    @pl.when(pl.program_id(0) == 0)
    def _():
        _na_fill_bias(rpb_ref, bt_ref, rows)

    lane_head = lax.broadcasted_iota(jnp.int32, (1, NA_HEADS * HEAD_DIM), 1) // HEAD_DIM

    def stack_heads(x):
        return jnp.concatenate([jnp.where(lane_head == h, x, 0.0) for h in range(NA_HEADS)], axis=0)

    def pick_heads(o4, n):
        out = o4[0:n]
        for h in range(1, NA_HEADS):
            out = jnp.where(lane_head == h, o4[h * n:(h + 1) * n], out)
        return out

    kc = k_ref[0, 0:CTX_LEN, :]
    vc = v_ref[0, 0:CTX_LEN, :]
    s = _dot_nt(stack_heads(q_ref[0, 0:CTX_LEN, :] * scale), kc)
    p = jnp.exp(s - jnp.max(s, axis=-1, keepdims=True))
    o = _dot(p.astype(BF16), vc) / jnp.sum(p, axis=-1, keepdims=True)
    o_ref[0, 0:CTX_LEN, :] = pick_heads(o, CTX_LEN).astype(o_ref.dtype)

    def logits(i):
        w0 = jnp.clip(i * NA_QROWS - NA_KH // 2, 0, rows - NA_WROWS)
        kind = jnp.where(i > 0, 1, 0) + jnp.where(i == nblk - 1, 1, 0)
        qrows = pl.ds(pl.multiple_of(CTX_LEN + i * qn, qn), qn)
        krows = pl.ds(pl.multiple_of(CTX_LEN + w0 * GRID_W, GRID_W), kn)
        q4 = stack_heads(q_ref[0, qrows, :] * scale)
        return _dot_nt(q4, k_ref[0, krows, :]), _dot_nt(q4, kc), kind, qrows, krows

    def attend(s_nb, s_cx, kind, qrows, krows):
        p_nb, p_cx, inv = [], [], []
        for h in range(NA_HEADS):
            r = slice(h * qn, (h + 1) * qn)
            a = s_nb[r] + bt_ref[kind, r, :]
            c = s_cx[r]
            mx = jnp.maximum(jnp.max(a, axis=-1, keepdims=True), jnp.max(c, axis=-1, keepdims=True))
            ea = jnp.exp(a - mx)
            ec = jnp.exp(c - mx)
            inv.append(1.0 / (jnp.sum(ea, axis=-1, keepdims=True) + jnp.sum(ec, axis=-1, keepdims=True)))
            p_nb.append(ea.astype(BF16))
            p_cx.append(ec.astype(BF16))
        o4 = (_dot(jnp.concatenate(p_nb, axis=0), v_ref[0, krows, :])
              + _dot(jnp.concatenate(p_cx, axis=0), vc)) * jnp.concatenate(inv, axis=0)
        o_ref[0, qrows, :] = pick_heads(o4, qn).astype(o_ref.dtype)

    def block_pair(j, carry):
        first = logits(2 * j)
        second = logits(2 * j + 1)
        attend(*first)
        attend(*second)
        return carry

    lax.fori_loop(0, nblk // 2, block_pair, 0)


def _na_attention(p, rpb):
    b, t, _ = p.shape
    rows = (t - CTX_LEN) // GRID_W
    rpb_p = jnp.pad(rpb.astype(F32), ((0, 0), (0, 1), (0, 128 - rpb.shape[2])))
    cb = lambda c: (lambda i: (i, 0, c // 256))
    return pl.pallas_call(
        functools.partial(_na_kernel, rows=rows),
        grid=(b,),
        in_specs=[
            pl.BlockSpec((1, t, 256), cb(COL_NAQ)),
            pl.BlockSpec((1, t, 256), cb(COL_NAK)),
            pl.BlockSpec((1, t, 256), cb(COL_NAV)),
            pl.BlockSpec(rpb_p.shape, lambda i: (0, 0, 0)),
        ],
        out_specs=pl.BlockSpec((1, t, 256), lambda i: (i, 0, 0)),
        out_shape=jax.ShapeDtypeStruct((b, t, 256), BF16),
        scratch_shapes=[pltpu.VMEM((3, NA_HEADS * NA_QROWS * GRID_W, NA_WROWS * GRID_W), F32)],
        compiler_params=_params(("arbitrary",)),
        name="na_attention",
    )(p, p, p, rpb_p)


def _log_sigmoid(x):
    return jnp.minimum(x, 0.0) - jnp.log(1.0 + jnp.exp(-jnp.abs(x)))


def _gla_kernel(q_ref, k_ref, v_ref, r_ref, gg_ref, w2_ref, bg_ref, gn_ref, o_ref,
                la_ref, acc_ref, qd_ref, ke_ref, el_ref, st_ref, *, t):
    c = GLA_CHUNK
    ng = GLA_GROUP
    gc = ng * c
    hd = GLA_HEADS * GLA_DK
    ngrp = t // gc
    scale = GLA_DK ** -0.5
    gi = lax.broadcasted_iota(jnp.int32, (gc, gc), 0)
    gj = lax.broadcasted_iota(jnp.int32, (gc, gc), 1)
    same_chunk = (gi // c) == (gj // c)
    same_head = (lax.broadcasted_iota(jnp.int32, (hd, hd), 0) // GLA_DK
                 == lax.broadcasted_iota(jnp.int32, (hd, hd), 1) // GLA_DK)
    row_i = lax.broadcasted_iota(jnp.int32, (c, hd), 0)
    col_j = lax.broadcasted_iota(jnp.int32, (c, hd), 1) % c
    ones4 = same_chunk.astype(BF16)
    tri4 = [(same_chunk & ((gj % c) <= (gi % c))).astype(BF16), (same_chunk & ((gj % c) >= (gi % c))).astype(BF16)]
    keep = [col_j <= row_i, col_j >= row_i]
    for d in range(2):
        x = _dot(gg_ref[0], w2_ref[d]) + bg_ref[d:d + 1, :]
        la_ref[d] = _log_sigmoid(x) * (1.0 / GLA_TAU)

    def interleave(stages):
        out = [None] * len(stages)
        live = list(range(len(stages)))
        while live:
            for i in list(live):
                try:
                    out[i] = next(stages[i])
                except StopIteration:
                    live.remove(i)
        return out

    def group_rows(sg):
        return pl.ds(sg * gc if isinstance(sg, int) else pl.multiple_of(sg * gc, gc), gc)

    def intra_stages(d, sg):
        rows4 = group_rows(sg)
        la = la_ref[d, rows4, :]
        hi = la.astype(BF16)
        lo = (la - hi.astype(F32)).astype(BF16)
        bc = _dot(tri4[d], hi) + _dot(tri4[d], lo)
        bl = _dot(ones4, hi) + _dot(ones4, lo)
        yield None
        qd = (q_ref[0, rows4, :].astype(F32) * scale * jnp.exp(bc)).astype(BF16)
        kf = k_ref[0, rows4, :].astype(F32)
        kd = (kf * jnp.exp(-bc)).astype(BF16)
        vv = v_ref[0, rows4, :]
        qd_ref[d, rows4, :] = qd
        ke_ref[d, rows4, :] = (kf * jnp.exp(bl - bc)).astype(BF16)
        el = jnp.exp(bl)
        for g in range(ng):
            el_ref[d, sg * ng + g] = el[g * c:g * c + 1, :]
        kbd = [jnp.where(same_head, jnp.concatenate([kd[g * c:(g + 1) * c]] * GLA_HEADS, axis=0), 0.0)
               for g in range(ng)]
        sc = [_dot_nt(qd[g * c:(g + 1) * c], kbd[g]) for g in range(ng)]
        yield None
        a = [jnp.where(keep[d], s, 0.0).astype(BF16) for s in sc]
        vbd = [jnp.where(same_head, jnp.concatenate([vv[g * c:(g + 1) * c]] * GLA_HEADS, axis=0), 0.0)
               for g in range(ng)]
        yield jnp.concatenate([_dot(a[g], vbd[g]) for g in range(ng)], axis=0)

    def intra(groups):
        outs = interleave([intra_stages(d, sg) for sg in groups for d in range(2)])
        for i, sg in enumerate(groups):
            acc_ref[group_rows(sg), :] = outs[2 * i] + outs[2 * i + 1]

    def intra_pair(j, carry):
        intra([2 * j, 2 * j + 1])
        return carry

    lax.fori_loop(0, ngrp // 2, intra_pair, 0)
    if ngrp % 2:
        intra([ngrp - 1])
    st_ref[...] = jnp.zeros_like(st_ref)

    def inter_stages(d, sg):
        if d == 0:
            grp, visit = sg, list(range(ng))
        else:
            grp, visit = jnp.where(sg == 0, 0, ngrp - sg), list(range(ng - 1, -1, -1))
        rows = [pl.ds(pl.multiple_of((grp * ng + g) * c, c), c) for g in visit]
        ds = [_dot_tn(v_ref[0, r, :], ke_ref[d, r, :]) for r in rows]
        yield None
        st = st_ref[d]
        sprev = []
        for i, g in enumerate(visit):
            sprev.append(st.astype(BF16))
            st = st * el_ref[d, grp * ng + g] + jnp.where(same_head, ds[i], 0.0)
        st_ref[d] = st
        oi = [_dot_nt(qd_ref[d, r, :], sprev[i]) for i, r in enumerate(rows)]
        yield None
        for i, r in enumerate(rows):
            acc_ref[r, :] += oi[i]
        yield None

    def inter(sg, carry):
        interleave([inter_stages(0, sg), inter_stages(1, sg)])
        return carry

    lax.fori_loop(0, ngrp, inter, 0)

    acc = acc_ref[...]
    sq = acc * acc
    sq_hi = sq.astype(BF16)
    sq_lo = (sq - sq_hi.astype(F32)).astype(BF16)
    avg = jnp.where(same_head, 1.0 / GLA_DK, 0.0).astype(BF16)
    ms = _dot(sq_hi, avg) + _dot(sq_lo, avg)
    r = r_ref[0].astype(F32)
    o_ref[0] = (acc * lax.rsqrt(ms + EPS) * gn_ref[...] * (r * _sigmoid(r))).astype(o_ref.dtype)


def _gla_attention(p, w2cat, b_gate, g_norm):
    b, t, _ = p.shape
    hd = GLA_HEADS * GLA_DK
    cb = lambda c: (lambda i: (i, 0, c // 256))
    return pl.pallas_call(
        functools.partial(_gla_kernel, t=t),
        grid=(b,),
        in_specs=[
            pl.BlockSpec((1, t, 256), cb(COL_GQ)),
            pl.BlockSpec((1, t, 256), cb(COL_GK)),
            pl.BlockSpec((1, t, 256), cb(COL_GV)),
            pl.BlockSpec((1, t, 256), cb(COL_GR)),
            pl.BlockSpec((1, t, 128), lambda i: (i, 0, COL_GG // 128)),
            pl.BlockSpec((2, 128, 256), lambda i: (0, 0, 0)),
            pl.BlockSpec((2, 256), lambda i: (0, 0)),
            pl.BlockSpec((1, hd), lambda i: (0, 0)),
        ],
        out_specs=pl.BlockSpec((1, t, 256), lambda i: (i, 0, 0)),
        out_shape=jax.ShapeDtypeStruct((b, t, 256), BF16),
        scratch_shapes=[
            pltpu.VMEM((2, t, hd), F32),
            pltpu.VMEM((t, hd), F32),
            pltpu.VMEM((2, t, hd), BF16),
            pltpu.VMEM((2, t, hd), BF16),
            pltpu.VMEM((2, t // GLA_CHUNK, 1, hd), F32),
            pltpu.VMEM((2, hd, hd), F32),
        ],
        compiler_params=_params(("arbitrary",)),
        name="gla_attention",
    )(p, p, p, p, p, w2cat, b_gate, jnp.tile(g_norm.reshape(1, GLA_DK), (1, GLA_HEADS)))


def _rope_tables(seq):
    pos = np.arange(seq)
    half = HEAD_DIM // 2
    inv_freq = ROPE_BASE ** (-np.arange(0, half, 2, dtype=np.float32) / half)
    cos_l, sin_l = [], []
    for p_axis in (pos // GRID_W, pos % GRID_W):
        ang = p_axis.astype(np.float32)[:, None] * inv_freq[None, :]
        cos_l += [np.cos(ang), np.cos(ang)]
        sin_l += [-np.sin(ang), np.sin(ang)]
    cos = np.concatenate(cos_l, axis=-1).astype(np.float32)
    sin = np.concatenate(sin_l, axis=-1).astype(np.float32)
    return jnp.asarray(np.tile(cos, (1, SWA_HEADS))), jnp.asarray(np.tile(sin, (1, SWA_HEADS)))


def _rope(x, cos, sin):
    n = x.shape[-1]
    lane = lax.broadcasted_iota(jnp.int32, x.shape, 1)
    partner = jnp.where((lane % 32) < 16, pltpu.roll(x, n - 16, 1), pltpu.roll(x, 16, 1))
    return x * cos + partner * sin


def _swa_kernel(q_ref, k_ref, v_ref, cos_ref, sin_ref, sink_ref, o_ref, qr_ref, kr_ref, *, t):
    seq = t - CTX_LEN
    nb = seq // SWA_BLOCK
    grp = SWA_HEADS // SWA_KV_HEADS
    kw = 3 * SWA_BLOCK
    scale = HEAD_DIM ** -0.5
    kv_w = SWA_KV_HEADS * HEAD_DIM
    lane = lax.broadcasted_iota(jnp.int32, (1, kv_w), 1)

    def to_kv_lanes(x, h):
        slab = x[:, (h // grp) * kv_w:(h // grp + 1) * kv_w]
        if h % grp != h // grp:
            slab = pltpu.roll(slab, HEAD_DIM, 1)
        return jnp.where(lane // HEAD_DIM == h // grp, slab, 0.0)

    def from_kv_lanes(o4, n):
        slabs = []
        for pair in range(SWA_HEADS // grp):
            lo = o4[(2 * pair) * n:(2 * pair + 1) * n]
            hi = o4[(2 * pair + 1) * n:(2 * pair + 2) * n]
            if pair == 0:
                hi = pltpu.roll(hi, HEAD_DIM, 1)
            else:
                lo = pltpu.roll(lo, HEAD_DIM, 1)
            slabs.append(jnp.where(lane < HEAD_DIM, lo, hi))
        return jnp.concatenate(slabs, axis=1)

    def sink_rows(n):
        row = lax.broadcasted_iota(jnp.int32, (SWA_HEADS * n, 1), 0) // n
        sk = jnp.full((SWA_HEADS * n, 1), sink_ref[0], F32)
        for h in range(1, SWA_HEADS):
            sk = jnp.where(row == h, sink_ref[h], sk)
        return sk

    cos = cos_ref[...]
    sin = sin_ref[...]
    qrot = _rope(q_ref[0, CTX_LEN:, :].astype(F32), cos, sin) * scale
    for h in range(SWA_HEADS):
        qr_ref[h] = to_kv_lanes(qrot, h).astype(BF16)
    kr_ref[...] = _rope(k_ref[0, CTX_LEN:, :].astype(F32), cos[:, :kv_w], sin[:, :kv_w]).astype(BF16)
    kc = k_ref[0, 0:CTX_LEN, :]
    vc = v_ref[0, 0:CTX_LEN, :]
    qc = q_ref[0, 0:CTX_LEN, :].astype(F32) * scale
    q4 = jnp.concatenate([to_kv_lanes(qc, h) for h in range(SWA_HEADS)], axis=0).astype(BF16)
    s = _dot_nt(q4, kc)
    sk = sink_rows(CTX_LEN)
    mx = jnp.maximum(jnp.max(s, axis=-1, keepdims=True), sk)
    p = jnp.exp(s - mx)
    den = jnp.sum(p, axis=-1, keepdims=True) + jnp.exp(sk - mx)
    o_ref[0, 0:CTX_LEN, :] = from_kv_lanes(_dot(p.astype(BF16), vc) / den, CTX_LEN).astype(o_ref.dtype)

    skb = sink_rows(SWA_BLOCK)
    row_b = lax.broadcasted_iota(jnp.int32, (SWA_HEADS * SWA_BLOCK, 1), 0) % SWA_BLOCK
    col_b = lax.broadcasted_iota(jnp.int32, (1, kw), 1)

    def logits(n):
        ws = jnp.clip(n - 1, 0, nb - 3) * SWA_BLOCK
        qrows = pl.ds(pl.multiple_of(n * SWA_BLOCK, SWA_BLOCK), SWA_BLOCK)
        krows = pl.ds(pl.multiple_of(ws, SWA_BLOCK), kw)
        qb = jnp.concatenate([qr_ref[h, qrows, :] for h in range(SWA_HEADS)], axis=0)
        return _dot_nt(qb, kr_ref[krows, :]), _dot_nt(qb, kc), n, ws

    def attend(s_b, s_c, n, ws):
        valid = jnp.abs(n * SWA_BLOCK + row_b - (ws + col_b)) <= SWA_WINDOW
        s_b = jnp.where(valid, s_b, NEG_INF)
        mxb = jnp.maximum(jnp.maximum(jnp.max(s_b, axis=-1, keepdims=True),
                                      jnp.max(s_c, axis=-1, keepdims=True)), skb)
        p_b = jnp.exp(s_b - mxb)
        p_c = jnp.exp(s_c - mxb)
        denb = (jnp.sum(p_b, axis=-1, keepdims=True) + jnp.sum(p_c, axis=-1, keepdims=True)
                + jnp.exp(skb - mxb))
        vrows = pl.ds(pl.multiple_of(CTX_LEN + ws, SWA_BLOCK), kw)
        o4 = (_dot(p_b.astype(BF16), v_ref[0, vrows, :]) + _dot(p_c.astype(BF16), vc)) / denb
        orows = pl.ds(pl.multiple_of(CTX_LEN + n * SWA_BLOCK, SWA_BLOCK), SWA_BLOCK)
        o_ref[0, orows, :] = from_kv_lanes(o4, SWA_BLOCK).astype(o_ref.dtype)

    def block_pair(j, carry):
        first = logits(2 * j)
        second = logits(2 * j + 1)
        attend(*first)
        attend(*second)
        return carry

    lax.fori_loop(0, nb // 2, block_pair, 0)


def _swa_attention(p, cos, sin, sink):
    b, t, _ = p.shape
    seq = t - CTX_LEN
    return pl.pallas_call(
        functools.partial(_swa_kernel, t=t),
        grid=(b,),
        in_specs=[
            pl.BlockSpec((1, t, 256), lambda i: (i, 0, COL_SQ // 256)),
            pl.BlockSpec((1, t, 128), lambda i: (i, 0, COL_SK // 128)),
            pl.BlockSpec((1, t, 128), lambda i: (i, 0, COL_SV // 128)),
            pl.BlockSpec((seq, 256), lambda i: (0, 0)),
            pl.BlockSpec((seq, 256), lambda i: (0, 0)),
            pl.BlockSpec(memory_space=pltpu.SMEM),
        ],
        out_specs=pl.BlockSpec((1, t, 256), lambda i: (i, 0, 0)),
        out_shape=jax.ShapeDtypeStruct((b, t, 256), BF16),
        scratch_shapes=[pltpu.VMEM((SWA_HEADS, seq, 128), BF16), pltpu.VMEM((seq, 128), BF16)],
        compiler_params=_params(("arbitrary",)),
        name="swa_attention",
    )(p, p, p, cos, sin, sink)


def _out_proj_kernel(u_ref, yg_ref, d_ref, wg_ref, na_ref, gla_ref, swa_ref, wo_ref, g_ref,
                     modl_ref, modc_ref, x_ref, o_ref, *, tm, first):
    ys = jnp.concatenate(
        [jnp.concatenate([yg_ref[g, j, 0] for g in range(S5_GROUPS)], axis=0).T for j in range(tm // S5_TC)], axis=0)
    y = d_ref[...] * u_ref[0].astype(F32) + ys
    z = _gelu(y)
    a = z * _sigmoid(_dot(z.astype(BF16), wg_ref[...]))
    acc = _dot(jnp.concatenate([a.astype(BF16), na_ref[0], gla_ref[0], swa_ref[0]], axis=1), wo_ref[...])
    row = (pl.program_id(1) + first) * tm + lax.broadcasted_iota(jnp.int32, (tm, 1), 0)
    gate = jnp.where(row < CTX_LEN, modc_ref[0, 2:3, :], modl_ref[0, 2:3, :])
    o_ref[0] = x_ref[0] + gate * (_rms(acc) * g_ref[...])


def _out_proj(p, yg, s5_d, w_glu, y_na, y_gla, y_swa, w_out, g_post, modl, modc, x, *, tm, skip_ctx=False):
    b, t_in, d = x.shape
    first = CTX_LEN // tm if skip_ctx else 0
    assert tm % S5_TC == 0 and (not skip_ctx or CTX_LEN % tm == 0)
    t = t_in - first * tm
    nc = tm // S5_TC
    tok = lambda w: pl.BlockSpec((1, tm, w), lambda i, j: (i, j + first, 0))
    full = lambda a: pl.BlockSpec(a.shape, lambda i, j: (0,) * a.ndim)
    chunked = pl.BlockSpec((S5_GROUPS, nc, 1, S5_GROUP_CH, S5_TC), lambda i, j: (0, j + first, i, 0, 0))
    return pl.pallas_call(
        functools.partial(_out_proj_kernel, tm=tm, first=first),
        grid=(b, t // tm),
        in_specs=[
            tok(256), chunked, full(s5_d), full(w_glu), tok(256), tok(256), tok(256),
            full(w_out), full(g_post),
            pl.BlockSpec((1, 6, d), lambda i, j: (i, 0, 0)),
            pl.BlockSpec((1, 6, d), lambda i, j: (0, 0, 0)),
            tok(d),
        ],
        out_specs=pl.BlockSpec((1, tm, d), lambda i, j: (i, j, 0)),
        out_shape=jax.ShapeDtypeStruct((b, t, d), F32),
        compiler_params=_params(("arbitrary", "arbitrary")),
        name="out_proj",
    )(p, yg, s5_d, w_glu, y_na, y_gla, y_swa, w_out, g_post, modl, modc, x)


def _ffn_down_kernel(gt_ref, vl_ref, prev_ref, next_ref, cw_ref, wd_ref, g_ref, modl_ref, modc_ref, x_ref, o_ref,
                     a_ref, *, tm, t, ctx_rows):
    h = FFN_HALO
    row0 = pl.program_id(1) * tm
    w0, w1, w2 = cw_ref[0:1, :], cw_ref[1:2, :], cw_ref[2:3, :]
    g = gt_ref[0].astype(F32)
    gp = pltpu.roll(g, 1, 0)
    gn = pltpu.roll(g, tm - 1, 0)
    w0b, w1b, w2b = w0.astype(BF16), w1.astype(BF16), w2.astype(BF16)
    a_ref[...] = _gelu(gp.astype(BF16) * w0b + gt_ref[0] * w1b + gn.astype(BF16) * w2b) * vl_ref[0]
    loc = lax.broadcasted_iota(jnp.int32, (h, 1), 0)
    has_prev = jnp.where((row0 == 0) | (row0 == ctx_rows), 0.0, 1.0)
    has_next = jnp.where((row0 + tm == ctx_rows) | (row0 + tm == t), 0.0, 1.0)
    gp_top = jnp.where(loc == 0, prev_ref[0, h - 1:h, :].astype(F32) * has_prev, gp[0:h])
    top = gp_top * w0 + g[0:h] * w1 + gn[0:h] * w2
    a_ref[0:h, :] = _gelu(top.astype(BF16)) * vl_ref[0, 0:h, :]
    gn_bot = jnp.where(loc == h - 1, next_ref[0, 0:1, :].astype(F32) * has_next, gn[tm - h:tm])
    bot = gp[tm - h:tm] * w0 + g[tm - h:tm] * w1 + gn_bot * w2
    a_ref[tm - h:tm, :] = _gelu(bot.astype(BF16)) * vl_ref[0, tm - h:tm, :]
    acc = _dot(a_ref[...], wd_ref[...])
    gate = jnp.where(row0 < ctx_rows, modc_ref[0, 5:6, :], modl_ref[0, 5:6, :])
    o_ref[0] = x_ref[0] + gate * (_rms(acc) * g_ref[...])


def _ffn_down(gv, conv_w, w_down, g_post, modl, modc, x, *, tm, ctx_rows=CTX_LEN):
    b, t, d = x.shape
    assert ctx_rows % tm == 0 and t % tm == 0 and tm > 2 * FFN_HALO
    nbh = tm // FFN_HALO
    lasth = t // FFN_HALO - 1
    return pl.pallas_call(
        functools.partial(_ffn_down_kernel, tm=tm, t=t, ctx_rows=ctx_rows),
        grid=(b, t // tm),
        in_specs=[
            pl.BlockSpec((1, tm, D_FF), lambda i, j: (i, j, 0)),
            pl.BlockSpec((1, tm, D_FF), lambda i, j: (i, j, 1)),
            pl.BlockSpec((1, FFN_HALO, D_FF), lambda i, j: (i, jnp.maximum(j * nbh - 1, 0), 0)),
            pl.BlockSpec((1, FFN_HALO, D_FF), lambda i, j: (i, jnp.minimum((j + 1) * nbh, lasth), 0)),
            pl.BlockSpec((3, D_FF), lambda i, j: (0, 0)),
            pl.BlockSpec((D_FF, d), lambda i, j: (0, 0)),
            pl.BlockSpec((1, d), lambda i, j: (0, 0)),
            pl.BlockSpec((1, 6, d), lambda i, j: (i, 0, 0)),
            pl.BlockSpec((1, 6, d), lambda i, j: (0, 0, 0)),
            pl.BlockSpec((1, tm, d), lambda i, j: (i, j, 0)),
        ],
        out_specs=pl.BlockSpec((1, tm, d), lambda i, j: (i, j, 0)),
        out_shape=jax.ShapeDtypeStruct((b, t, d), F32),
        scratch_shapes=[pltpu.VMEM((tm, D_FF), BF16)],
        compiler_params=_params(("arbitrary", "arbitrary")),
        name="ffn_down",
    )(gv, gv, gv, gv, conv_w, w_down, g_post, modl, modc, x)


def _reorder_w_in(w_in):
    sizes = (256, 256, 256, 256, 256, 256, 256, 16, 16, 256, 256, 128, 128)
    offs = np.concatenate([[0], np.cumsum(sizes)])
    seg = lambda i: w_in[..., offs[i]:offs[i + 1]]
    pad = jnp.zeros(w_in.shape[:-1] + (P_WIDTH - COL_GG - 2 * GLA_RANK,), w_in.dtype)
    return jnp.concatenate([seg(0), seg(1), seg(2), seg(3), seg(4), seg(5), seg(6), seg(9),
                            seg(10), seg(11), seg(12), seg(7), seg(8), pad], axis=-1)


def kernel(x, c, ctx, c_ctx, w_mod, b_mod, g_pre_mix, g_post_mix, g_pre_ffn, g_post_ffn, w_in, w_out, s5_lam_re, s5_lam_im, s5_b_re, s5_b_im, s5_c_re, s5_c_im, s5_log_step, s5_d, s5_w_glu, na_rpb, gla_w_gate2, gla_b_gate, gla_g_norm, swa_sink, ffn_w_up, ffn_conv, ffn_w_down):
    bsz, seq, d = x.shape
    depth = w_mod.shape[0]
    t = CTX_LEN + seq
    tm = 768

    cvec = jnp.concatenate([c, c_ctx[None, :]], axis=0)
    cvec = jnp.pad(cvec, ((0, (-cvec.shape[0]) % 8), (0, 0)))
    mod = _modulation(cvec, w_mod, b_mod).reshape(depth, cvec.shape[0], 6, d)

    w_in_r = _reorder_w_in(w_in.astype(BF16))
    w_out_b = w_out.astype(BF16)
    w_glu_b = s5_w_glu.astype(BF16)
    w_up_b = ffn_w_up.astype(BF16)
    w_down_b = ffn_w_down.astype(BF16)
    w2 = jnp.stack([jnp.pad(gla_w_gate2[:, 0], ((0, 0), (0, 128 - GLA_RANK), (0, 0))),
                    jnp.pad(gla_w_gate2[:, 1], ((0, 0), (GLA_RANK, 128 - 2 * GLA_RANK), (0, 0)))], axis=1).astype(BF16)
    cos, sin = _rope_tables(seq)

    xc = jnp.concatenate([ctx, x], axis=1)
    for l in range(depth):
        modl = mod[l, :bsz]
        modc = mod[l, bsz:bsz + 1]
        p, ug = _nm_matmul(xc, g_pre_mix[l][None], modl, modc, w_in_r[l], k_shift=0, k_scale=1, tm=tm, tn=P_WIDTH,
                           s5_chunks=True)
        tabs = _s5_prep(s5_lam_re[l], s5_lam_im[l], s5_b_re[l], s5_b_im[l], s5_c_re[l], s5_c_im[l], s5_log_step[l])
        ys = _s5_scan(ug, *tabs)
        y_na = _na_attention(p, na_rpb[l])
        y_gla = _gla_attention(p, w2[l], gla_b_gate[l], gla_g_norm[l][None])
        y_swa = _swa_attention(p, cos, sin, swa_sink[l])
        last = l == depth - 1
        ctx_rows = 0 if last else CTX_LEN
        xc = _out_proj(p, ys, s5_d[l][None], w_glu_b[l], y_na, y_gla, y_swa, w_out_b[l],
                       g_post_mix[l][None], modl, modc, xc, tm=CTX_LEN if last else tm, skip_ctx=last)
        gv = _nm_matmul(xc, g_pre_ffn[l][None], modl, modc, w_up_b[l], k_shift=3, k_scale=4,
                        tm=1024 if last else tm, tn=D_FF, ctx_rows=ctx_rows)
        xc = _ffn_down(gv, ffn_conv[l], w_down_b[l], g_post_ffn[l][None], modl, modc, xc, tm=256, ctx_rows=ctx_rows)
    return xc
```

```python
import functools
import math

import numpy as np
import jax
import jax.numpy as jnp
from jax import lax
from jax.experimental import pallas as pl
from jax.experimental.pallas import tpu as pltpu

F32 = jnp.float32
BF16 = jnp.bfloat16

D_MODEL = 1024
GRID_W = 64
CTX_LEN = 256
HEAD_DIM = 64
EPS = 1e-6
NEG_INF = -1e30

S5_CH = 256
S5_GROUP_CH = 16
S5_GROUPS = S5_CH // S5_GROUP_CH
S5_STATE = 64
S5_TC = 128
S5_K = S5_TC * S5_GROUP_CH

NA_HEADS = 4
NA_KH = 8
NA_KW = 16
NA_QROWS = 4
NA_WROWS = NA_QROWS + NA_KH
NA_RUN = 2

GLA_HEADS = 4
GLA_DK = 64
GLA_RANK = 16
GLA_TAU = 16.0
GLA_CHUNK = 64
GLA_GROUP = CTX_LEN // GLA_CHUNK
GLA_RUN = 3

SWA_HEADS = 4
SWA_KV_HEADS = 2
SWA_WINDOW = 128
SWA_BLOCK = 128
SWA_RUN = 4
ROPE_BASE = 10000.0

D_FF = 2816
FFN_HALO = 16

COL_S5, COL_NAQ, COL_NAK, COL_NAV = 0, 256, 512, 768
COL_GQ, COL_GK, COL_GV, COL_GR = 1024, 1280, 1536, 1792
COL_SQ, COL_SK, COL_SV, COL_GG = 2048, 2304, 2432, 2560
P_WIDTH = 2688

VMEM_LIMIT = 56 * 1024 * 1024


def _dot(a, b):
    return jnp.dot(a, b, preferred_element_type=F32)


def _dot_exact(a, b):
    return jnp.dot(a, b, preferred_element_type=F32, precision=lax.Precision.HIGHEST)


def _dot_nt(a, b):
    return lax.dot_general(a, b, (((1,), (1,)), ((), ())), preferred_element_type=F32)


def _dot_tn(a, b):
    return lax.dot_general(a, b, (((0,), (0,)), ((), ())), preferred_element_type=F32)


def _gelu(x):
    k = math.sqrt(2.0 / math.pi)
    return x * (0.5 + 0.5 * jnp.tanh(x * (k + (k * 0.044715) * (x * x))))


def _sigmoid(x):
    return 1.0 / (1.0 + jnp.exp(-x))


def _rms(x):
    return x * lax.rsqrt(jnp.mean(x * x, axis=-1, keepdims=True) + EPS)


def _params(sem):
    return pltpu.CompilerParams(dimension_semantics=sem, vmem_limit_bytes=VMEM_LIMIT)


def _mod_kernel(c_ref, w_ref, b_ref, o_ref):
    c = c_ref[...]
    s = (c * _sigmoid(c)).astype(BF16)
    o_ref[0] = _dot(s, w_ref[0].astype(BF16)) + b_ref[0]


def _modulation(cvec, w_mod, b_mod):
    depth, d, n = w_mod.shape
    r = cvec.shape[0]
    tn = 1536
    return pl.pallas_call(
        _mod_kernel,
        grid=(depth, n // tn),
        in_specs=[
            pl.BlockSpec((r, d), lambda l, j: (0, 0)),
            pl.BlockSpec((1, d, tn), lambda l, j: (l, 0, j)),
            pl.BlockSpec((1, 1, tn), lambda l, j: (l, 0, j)),
        ],
        out_specs=pl.BlockSpec((1, r, tn), lambda l, j: (l, 0, j)),
        out_shape=jax.ShapeDtypeStruct((depth, r, n), F32),
        compiler_params=_params(("arbitrary", "arbitrary")),
        name="modulation",
    )(cvec, w_mod, b_mod.reshape(depth, 1, n))


def _nm_matmul_kernel(x0_ref, xn_ref, g_ref, modl0_ref, modln_ref, modc_ref, w_ref, o_ref, *rest,
                      tm, nt, nk, k_shift, k_scale, ctx_rows, s5_chunks):
    ha_ref, hb_ref = rest[-2], rest[-1]
    tile = pl.program_id(0) * nt + pl.program_id(1)
    k = pl.program_id(2)
    part = tm // nk

    def normed(x, modl_ref, row0):
        y = _rms(x) * g_ref[...]
        row = row0 + lax.broadcasted_iota(jnp.int32, (x.shape[0], 1), 0)
        is_ctx = row < ctx_rows
        scale = jnp.where(is_ctx, modc_ref[0, k_scale:k_scale + 1, :], modl_ref[0, k_scale:k_scale + 1, :])
        shift = jnp.where(is_ctx, modc_ref[0, k_shift:k_shift + 1, :], modl_ref[0, k_shift:k_shift + 1, :])
        return (y * (1.0 + scale) + shift).astype(BF16)

    @pl.when((tile == 0) & (k == 0))
    def _():
        ha_ref[...] = normed(x0_ref[0], modl0_ref, 0)

    def step(cur_ref, nxt_ref):
        rows = pl.ds(pl.multiple_of(k * part, part), part)
        nxt_ref[rows, :] = normed(xn_ref[0, rows, :], modln_ref, ((pl.program_id(1) + 1) % nt) * tm + k * part)
        res = _dot(cur_ref[...], w_ref[...])
        o_ref[0] = res.astype(o_ref.dtype)
        if s5_chunks:
            ug_ref = rest[0]
            for j in range(tm // S5_TC):
                z = res[j * S5_TC:(j + 1) * S5_TC, COL_S5:COL_S5 + S5_CH].T.astype(BF16)
                for g in range(S5_GROUPS):
                    ug_ref[g, j, 0] = z[g * S5_GROUP_CH:(g + 1) * S5_GROUP_CH, :]

    @pl.when(tile % 2 == 0)
    def _():
        step(ha_ref, hb_ref)

    @pl.when(tile % 2 == 1)
    def _():
        step(hb_ref, ha_ref)


def _nm_matmul(x, g, modl, modc, w, *, k_shift, k_scale, tm, tn, ctx_rows=CTX_LEN, s5_chunks=False):
    b, t, d = x.shape
    n = w.shape[1]
    assert t % tm == 0 and n % tn == 0 and (not s5_chunks or (tn == n and tm % S5_TC == 0))
    out_specs = [pl.BlockSpec((1, tm, tn), lambda i, j, k: (i, j, k))]
    out_shape = [jax.ShapeDtypeStruct((b, t, n), BF16)]
    if s5_chunks:
        out_specs.append(pl.BlockSpec((S5_GROUPS, tm // S5_TC, 1, S5_GROUP_CH, S5_TC), lambda i, j, k: (0, j, i, 0, 0)))
        out_shape.append(jax.ShapeDtypeStruct((S5_GROUPS, t // S5_TC, b, S5_GROUP_CH, S5_TC), BF16))
    nt, nk = t // tm, n // tn
    assert (tm // nk) % 16 == 0

    def next_tile(i, j):
        s = i * nt + j + 1
        return jnp.minimum(s // nt, b - 1), s % nt

    out = pl.pallas_call(
        functools.partial(_nm_matmul_kernel, tm=tm, nt=nt, nk=nk, k_shift=k_shift, k_scale=k_scale,
                          ctx_rows=ctx_rows, s5_chunks=s5_chunks),
        grid=(b, nt, nk),
        in_specs=[
            pl.BlockSpec((1, tm, d), lambda i, j, k: (0, 0, 0)),
            pl.BlockSpec((1, tm, d), lambda i, j, k: next_tile(i, j) + (0,)),
            pl.BlockSpec((1, d), lambda i, j, k: (0, 0)),
            pl.BlockSpec((1, 6, d), lambda i, j, k: (0, 0, 0)),
            pl.BlockSpec((1, 6, d), lambda i, j, k: (next_tile(i, j)[0], 0, 0)),
            pl.BlockSpec((1, 6, d), lambda i, j, k: (0, 0, 0)),
            pl.BlockSpec((d, tn), lambda i, j, k: (0, k)),
        ],
        out_specs=out_specs,
        out_shape=out_shape,
        scratch_shapes=[pltpu.VMEM((tm, d), BF16), pltpu.VMEM((tm, d), BF16)],
        compiler_params=_params(("arbitrary", "arbitrary", "arbitrary")),
        name="norm_mod_matmul",
    )(x, x, g, modl, modl, modc, w)
    return tuple(out) if s5_chunks else out[0]


def _s5_prep_kernel(lr_ref, lc_ref, ls_ref, bt_ref, ct_ref, t_ref, s_ref, c_ref, a_ref, kap_ref):
    tc = S5_TC
    p = S5_STATE
    lane = lax.broadcasted_iota(jnp.int32, (1, tc), 1).astype(F32)
    sub = lax.broadcasted_iota(jnp.int32, (tc, 1), 0).astype(F32)
    s_idx = lax.broadcasted_iota(jnp.int32, (tc, tc), 0)
    t_idx = lax.broadcasted_iota(jnp.int32, (tc, tc), 1)
    for d in range(2):
        step = jnp.exp(ls_ref[0, d:d + 1, :])
        lre = lr_ref[0, d, 0:1, :]
        lim = lr_ref[0, d, 1:2, :]
        zr = lre * step
        zi = lim * step
        mag = jnp.exp(zr)
        nr = mag * jnp.cos(zi) - 1.0
        ni = mag * jnp.sin(zi)
        den = lre * lre + lim * lim
        cr = (nr * lre + ni * lim) / den
        ci = (ni * lre - nr * lim) / den
        btr = bt_ref[0, d, 0]
        bti = bt_ref[0, d, 1]
        bbr = cr * btr - ci * bti
        bbi = cr * bti + ci * btr
        ma = jnp.exp(zr * tc)
        ar = ma * jnp.cos(zi * tc)
        ai = ma * jnp.sin(zi * tc)
        a_ref[0, 0:1, d * 2 * p:d * 2 * p + p] = ar
        a_ref[0, 0:1, d * 2 * p + p:(d + 1) * 2 * p] = ar
        a_ref[0, 1:2, d * 2 * p:d * 2 * p + p] = -ai
        a_ref[0, 1:2, d * 2 * p + p:(d + 1) * 2 * p] = ai
        zrc = lc_ref[0, d, :, 0:1] * step
        zic = lc_ref[0, d, :, 1:2] * step

        def powers(tau):
            m = jnp.exp(zrc * tau)
            ang = zic * tau
            return m * jnp.cos(ang), m * jnp.sin(ang)

        if d == 0:
            pkr, pki = powers(lane)
            pcr, pci = powers(lane + 1.0)
            es = (tc - 1.0) - sub
        else:
            pkr, pki = powers(jnp.where(lane == 0.0, 0.0, tc - lane))
            pcr, pci = powers(tc - lane)
            es = sub
        ctr = ct_ref[0, d, 0]
        cti = ct_ref[0, d, 1]
        for co in range(S5_GROUP_CH):
            ccr = ctr[:, co:co + 1]
            cci = cti[:, co:co + 1]
            kap_ref[d, co] = (_dot_exact(bbr, ccr * pkr - cci * pki)
                              - _dot_exact(bbi, ccr * pki + cci * pkr))
            cols = slice(co * tc, (co + 1) * tc)
            c_ref[0, d * 2 * p:d * 2 * p + p, cols] = (ccr * pcr - cci * pci).astype(BF16)
            c_ref[0, d * 2 * p + p:(d + 1) * 2 * p, cols] = (-(ccr * pci + cci * pcr)).astype(BF16)
        me = jnp.exp(zr * es)
        er = me * jnp.cos(zi * es)
        ei = me * jnp.sin(zi * es)
        for cin in range(S5_GROUP_CH):
            br = bbr[cin:cin + 1, :]
            bi = bbi[cin:cin + 1, :]
            rows = slice(cin * tc, (cin + 1) * tc)
            s_ref[0, rows, d * 2 * p:d * 2 * p + p] = (er * br - ei * bi).astype(BF16)
            s_ref[0, rows, d * 2 * p + p:(d + 1) * 2 * p] = (er * bi + ei * br).astype(BF16)

    for co in range(S5_GROUP_CH):
        cols = slice(co * tc, (co + 1) * tc)

        def toeplitz(i, carry):
            cins = [i * 8 + j for j in range(8)]
            kr = [kap_ref[1, co, pl.ds(cin, 1), :] for cin in cins]
            kf = [kap_ref[0, co, pl.ds(cin, 1), :] + jnp.where(lane == 0.0, r, 0.0)
                  for cin, r in zip(cins, kr)]
            mixed = [jnp.where(t_idx + s_idx < tc, jnp.broadcast_to(f, (tc, tc)), jnp.broadcast_to(r, (tc, tc)))
                     for f, r in zip(kf, kr)]
            blk = [pltpu.roll(m, 0, 1, stride=1, stride_axis=0).astype(BF16) for m in mixed]
            for cin, b in zip(cins, blk):
                t_ref[0, pl.ds(pl.multiple_of(cin * tc, tc), tc), cols] = b
            return carry

        lax.fori_loop(0, S5_GROUP_CH // 8, toeplitz, 0)


def _s5_prep(lam_re, lam_im, b_re, b_im, c_re, c_im, log_step):
    g = S5_GROUPS
    lam = jnp.stack([lam_re, lam_im], axis=-2).astype(F32).transpose(1, 0, 2, 3)
    lam_c = lam.transpose(0, 1, 3, 2)
    ls = log_step.astype(F32).T[:, :, None]
    bt = jnp.stack([b_re, b_im], axis=2).astype(F32).transpose(1, 0, 2, 4, 3)
    ct = jnp.stack([c_re, c_im], axis=2).astype(F32).transpose(1, 0, 2, 4, 3)
    blk = lambda a: pl.BlockSpec((1,) + a.shape[1:], lambda i: (i,) + (0,) * (a.ndim - 1))
    return pl.pallas_call(
        _s5_prep_kernel,
        grid=(g,),
        in_specs=[blk(lam), blk(lam_c), blk(ls), blk(bt), blk(ct)],
        out_specs=[
            pl.BlockSpec((1, S5_K, S5_K), lambda i: (i, 0, 0)),
            pl.BlockSpec((1, S5_K, 4 * S5_STATE), lambda i: (i, 0, 0)),
            pl.BlockSpec((1, 4 * S5_STATE, S5_K), lambda i: (i, 0, 0)),
            pl.BlockSpec((1, 2, 4 * S5_STATE), lambda i: (i, 0, 0)),
        ],
        out_shape=[
            jax.ShapeDtypeStruct((g, S5_K, S5_K), BF16),
            jax.ShapeDtypeStruct((g, S5_K, 4 * S5_STATE), BF16),
            jax.ShapeDtypeStruct((g, 4 * S5_STATE, S5_K), BF16),
            jax.ShapeDtypeStruct((g, 2, 4 * S5_STATE), F32),
        ],
        scratch_shapes=[pltpu.VMEM((2, S5_GROUP_CH, S5_GROUP_CH, S5_TC), F32)],
        compiler_params=_params(("arbitrary",)),
        name="s5_prep",
    )(lam, lam_c, ls, bt, ct)


def _s5_scan_kernel(u_ref, t_ref, s_ref, c_ref, a_ref, y_ref, loc_ref, hin_ref, *, bsz, nch):
    w = 2 * S5_STATE
    u = u_ref[0]
    loc_ref[...] = _dot(u, s_ref[0])
    a1f, a2f = a_ref[0, 0:1, 0:w], a_ref[0, 1:2, 0:w]
    a1r, a2r = a_ref[0, 0:1, w:2 * w], a_ref[0, 1:2, w:2 * w]
    nctx = CTX_LEN // S5_TC

    def body(i, carry):
        hf, hr = carry
        kr = jnp.where(i < nctx, nctx - 1 - i, nch + nctx - 1 - i)
        rf = pl.ds(pl.multiple_of(i * bsz, bsz), bsz)
        rr = pl.ds(pl.multiple_of(kr * bsz, bsz), bsz)
        hin_ref[rf, 0:w] = hf
        hin_ref[rr, w:2 * w] = hr
        hf = a1f * hf + a2f * pltpu.roll(hf, S5_STATE, 1) + loc_ref[rf, 0:w]
        hr = a1r * hr + a2r * pltpu.roll(hr, S5_STATE, 1) + loc_ref[rr, w:2 * w]
        return hf, hr

    zero = jnp.zeros((bsz, w), F32)
    lax.fori_loop(0, nch, body, (zero, zero))
    y_ref[0] = _dot(u, t_ref[0]) + _dot(hin_ref[...].astype(BF16), c_ref[0])


def _s5_scan(ug, tmat, smat, cmat, amat):
    g, nch, b = ug.shape[:3]
    m = nch * b
    yg = pl.pallas_call(
        functools.partial(_s5_scan_kernel, bsz=b, nch=nch),
        grid=(g,),
        in_specs=[
            pl.BlockSpec((1, m, S5_K), lambda i: (i, 0, 0)),
            pl.BlockSpec((1, S5_K, S5_K), lambda i: (i, 0, 0)),
            pl.BlockSpec((1, S5_K, 4 * S5_STATE), lambda i: (i, 0, 0)),
            pl.BlockSpec((1, 4 * S5_STATE, S5_K), lambda i: (i, 0, 0)),
            pl.BlockSpec((1, 2, 4 * S5_STATE), lambda i: (i, 0, 0)),
        ],
        out_specs=pl.BlockSpec((1, m, S5_K), lambda i: (i, 0, 0)),
        out_shape=jax.ShapeDtypeStruct((g, m, S5_K), F32),
        scratch_shapes=[pltpu.VMEM((m, 4 * S5_STATE), F32), pltpu.VMEM((m, 4 * S5_STATE), F32)],
        compiler_params=_params(("arbitrary",)),
        name="s5_scan",
    )(ug.reshape(g, m, S5_K), tmat, smat, cmat, amat)
    return yg.reshape(g, nch, b, S5_GROUP_CH, S5_TC)


def _na_block_geometry(rows):
    geo = []
    for r0 in (0, NA_QROWS, rows - NA_QROWS):
        geo.append((r0, min(max(r0 - NA_KH // 2, 0), rows - NA_WROWS)))
    return geo


def _na_fill_bias(rpb_ref, bt_ref, rows):
    w = GRID_W
    qc = lax.broadcasted_iota(jnp.int32, (w, w), 0)
    kc = lax.broadcasted_iota(jnp.int32, (w, w), 1)
    ws = jnp.clip(qc - NA_KW // 2, 0, w - NA_KW)
    in_win = (kc >= ws) & (kc < ws + NA_KW)
    neg = jnp.full((w, w), NEG_INF, F32)
    for h in range(NA_HEADS):
        cm = []
        for dr in range(2 * NA_KH - 1):
            v = jnp.broadcast_to(rpb_ref[h, dr:dr + 1, :], (w, 128))
            v = pltpu.roll(v, 128 - (NA_KW - 1), 1, stride=1, stride_axis=0)
            cm.append(jnp.where(in_win, v[:, :w], NEG_INF))
        for kind, (r0, w0) in enumerate(_na_block_geometry(rows)):
            for qr in range(NA_QROWS):
                start = min(max(r0 + qr - NA_KH // 2, 0), rows - NA_KH)
                for kr in range(0, NA_WROWS, 2):
                    pair = []
                    for k in (kr, kr + 1):
                        inside = start <= w0 + k < start + NA_KH
                        pair.append(cm[(w0 + k) - (r0 + qr) + NA_KH - 1] if inside else neg)
                    r_lo = (h * NA_QROWS + qr) * w
                    bt_ref[kind, r_lo:r_lo + w, kr * w:(kr + 2) * w] = jnp.concatenate(pair, axis=1)


def _na_kernel(q_ref, k_ref, v_ref, rpb_ref, o_ref, bt_ref, *, rows):
    qn = NA_QROWS * GRID_W
    kn = NA_WROWS * GRID_W
    nblk = rows // NA_QROWS
    scale = HEAD_DIM ** -0.5

    @pl.when(pl.program_id(0) == 0)
    def _():
        _na_fill_bias(rpb_ref, bt_ref, rows)

    lane_head = lax.broadcasted_iota(jnp.int32, (1, NA_HEADS * HEAD_DIM), 1) // HEAD_DIM

    def stack_heads(x):
        return jnp.concatenate([jnp.where(lane_head == h, x, 0.0) for h in range(NA_HEADS)], axis=0)

    def pick_heads(o4, n):
        out = o4[0:n]
        for h in range(1, NA_HEADS):
            out = jnp.where(lane_head == h, o4[h * n:(h + 1) * n], out)
        return out

    kc = k_ref[0, 0:CTX_LEN, :]
    vc = v_ref[0, 0:CTX_LEN, :]
    s = _dot_nt(stack_heads(q_ref[0, 0:CTX_LEN, :] * scale), kc)
    p = jnp.exp(s - jnp.max(s, axis=-1, keepdims=True))
    o = _dot(p.astype(BF16), vc) / jnp.sum(p, axis=-1, keepdims=True)
    o_ref[0, 0:CTX_LEN, :] = pick_heads(o, CTX_LEN).astype(o_ref.dtype)

    def logits(i):
        w0 = jnp.clip(i * NA_QROWS - NA_KH // 2, 0, rows - NA_WROWS)
        kind = jnp.where(i > 0, 1, 0) + jnp.where(i == nblk - 1, 1, 0)
        qrows = pl.ds(pl.multiple_of(CTX_LEN + i * qn, qn), qn)
        krows = pl.ds(pl.multiple_of(CTX_LEN + w0 * GRID_W, GRID_W), kn)
        q4 = stack_heads(q_ref[0, qrows, :] * scale)
        return _dot_nt(q4, k_ref[0, krows, :]), _dot_nt(q4, kc), kind, qrows, krows

    def attend(s_nb, s_cx, kind, qrows, krows):
        p_nb, p_cx, inv = [], [], []
        for h in range(NA_HEADS):
            r = slice(h * qn, (h + 1) * qn)
            a = s_nb[r] + bt_ref[kind, r, :]
            c = s_cx[r]
            mx = jnp.maximum(jnp.max(a, axis=-1, keepdims=True), jnp.max(c, axis=-1, keepdims=True))
            ea = jnp.exp(a - mx)
            ec = jnp.exp(c - mx)
            inv.append(1.0 / (jnp.sum(ea, axis=-1, keepdims=True) + jnp.sum(ec, axis=-1, keepdims=True)))
            p_nb.append(ea.astype(BF16))
            p_cx.append(ec.astype(BF16))
        o4 = (_dot(jnp.concatenate(p_nb, axis=0), v_ref[0, krows, :])
              + _dot(jnp.concatenate(p_cx, axis=0), vc)) * jnp.concatenate(inv, axis=0)
        o_ref[0, qrows, :] = pick_heads(o4, qn).astype(o_ref.dtype)

    def block_run(j, carry):
        ahead = logits(NA_RUN * j)
        for i in range(1, NA_RUN):
            cur, ahead = ahead, logits(NA_RUN * j + i)
            attend(*cur)
        attend(*ahead)
        return carry

    lax.fori_loop(0, nblk // NA_RUN, block_run, 0)


def _na_attention(p, rpb):
    b, t, _ = p.shape
    rows = (t - CTX_LEN) // GRID_W
    rpb_p = jnp.pad(rpb.astype(F32), ((0, 0), (0, 1), (0, 128 - rpb.shape[2])))
    cb = lambda c: (lambda i: (i, 0, c // 256))
    return pl.pallas_call(
        functools.partial(_na_kernel, rows=rows),
        grid=(b,),
        in_specs=[
            pl.BlockSpec((1, t, 256), cb(COL_NAQ)),
            pl.BlockSpec((1, t, 256), cb(COL_NAK)),
            pl.BlockSpec((1, t, 256), cb(COL_NAV)),
            pl.BlockSpec(rpb_p.shape, lambda i: (0, 0, 0)),
        ],
        out_specs=pl.BlockSpec((1, t, 256), lambda i: (i, 0, 0)),
        out_shape=jax.ShapeDtypeStruct((b, t, 256), BF16),
        scratch_shapes=[pltpu.VMEM((3, NA_HEADS * NA_QROWS * GRID_W, NA_WROWS * GRID_W), F32)],
        compiler_params=_params(("arbitrary",)),
        name="na_attention",
    )(p, p, p, rpb_p)


def _log_sigmoid(x):
    return jnp.minimum(x, 0.0) - jnp.log(1.0 + jnp.exp(-jnp.abs(x)))


def _gla_kernel(q_ref, k_ref, v_ref, r_ref, gg_ref, w2_ref, bg_ref, gn_ref, o_ref,
                la_ref, acc_ref, qd_ref, ke_ref, el_ref, st_ref, *, t):
    c = GLA_CHUNK
    ng = GLA_GROUP
    gc = ng * c
    hd = GLA_HEADS * GLA_DK
    ngrp = t // gc
    scale = GLA_DK ** -0.5
    gi = lax.broadcasted_iota(jnp.int32, (gc, gc), 0)
    gj = lax.broadcasted_iota(jnp.int32, (gc, gc), 1)
    same_chunk = (gi // c) == (gj // c)
    same_head = (lax.broadcasted_iota(jnp.int32, (hd, hd), 0) // GLA_DK
                 == lax.broadcasted_iota(jnp.int32, (hd, hd), 1) // GLA_DK)
    row_i = lax.broadcasted_iota(jnp.int32, (c, hd), 0)
    col_j = lax.broadcasted_iota(jnp.int32, (c, hd), 1) % c
    ones4 = same_chunk.astype(BF16)
    tri4 = [(same_chunk & ((gj % c) <= (gi % c))).astype(BF16), (same_chunk & ((gj % c) >= (gi % c))).astype(BF16)]
    keep = [col_j <= row_i, col_j >= row_i]
    for d in range(2):
        x = _dot(gg_ref[0], w2_ref[d]) + bg_ref[d:d + 1, :]
        la_ref[d] = _log_sigmoid(x) * (1.0 / GLA_TAU)

    def interleave(stages):
        out = [None] * len(stages)
        live = list(range(len(stages)))
        while live:
            for i in list(live):
                try:
                    out[i] = next(stages[i])
                except StopIteration:
                    live.remove(i)
        return out

    def group_rows(sg):
        return pl.ds(sg * gc if isinstance(sg, int) else pl.multiple_of(sg * gc, gc), gc)

    def intra_stages(d, sg):
        rows4 = group_rows(sg)
        la = la_ref[d, rows4, :]
        hi = la.astype(BF16)
        lo = (la - hi.astype(F32)).astype(BF16)
        bc = _dot(tri4[d], hi) + _dot(tri4[d], lo)
        bl = _dot(ones4, hi) + _dot(ones4, lo)
        yield None
        qd = (q_ref[0, rows4, :].astype(F32) * scale * jnp.exp(bc)).astype(BF16)
        kf = k_ref[0, rows4, :].astype(F32)
        kd = (kf * jnp.exp(-bc)).astype(BF16)
        vv = v_ref[0, rows4, :]
        qd_ref[d, rows4, :] = qd
        ke_ref[d, rows4, :] = (kf * jnp.exp(bl - bc)).astype(BF16)
        el = jnp.exp(bl)
        for g in range(ng):
            el_ref[d, sg * ng + g] = el[g * c:g * c + 1, :]
        kbd = [jnp.where(same_head, jnp.concatenate([kd[g * c:(g + 1) * c]] * GLA_HEADS, axis=0), 0.0)
               for g in range(ng)]
        sc = [_dot_nt(qd[g * c:(g + 1) * c], kbd[g]) for g in range(ng)]
        yield None
        a = [jnp.where(keep[d], s, 0.0).astype(BF16) for s in sc]
        vbd = [jnp.where(same_head, jnp.concatenate([vv[g * c:(g + 1) * c]] * GLA_HEADS, axis=0), 0.0)
               for g in range(ng)]
        yield jnp.concatenate([_dot(a[g], vbd[g]) for g in range(ng)], axis=0)

    def intra(groups):
        outs = interleave([intra_stages(d, sg) for sg in groups for d in range(2)])
        for i, sg in enumerate(groups):
            acc_ref[group_rows(sg), :] = outs[2 * i] + outs[2 * i + 1]

    run = GLA_RUN if ngrp % GLA_RUN == 0 else 1

    def intra_run(j, carry):
        intra([run * j + i for i in range(run)])
        return carry

    lax.fori_loop(0, ngrp // run, intra_run, 0)
    st_ref[...] = jnp.zeros_like(st_ref)

    def inter_stages(d, sg):
        if d == 0:
            grp, visit = sg, list(range(ng))
        else:
            grp, visit = jnp.where(sg == 0, 0, ngrp - sg), list(range(ng - 1, -1, -1))
        rows = [pl.ds(pl.multiple_of((grp * ng + g) * c, c), c) for g in visit]
        ds = [_dot_tn(v_ref[0, r, :], ke_ref[d, r, :]) for r in rows]
        yield None
        st = st_ref[d]
        sprev = []
        for i, g in enumerate(visit):
            sprev.append(st.astype(BF16))
            st = st * el_ref[d, grp * ng + g] + jnp.where(same_head, ds[i], 0.0)
        st_ref[d] = st
        oi = [_dot_nt(qd_ref[d, r, :], sprev[i]) for i, r in enumerate(rows)]
        yield None
        for i, r in enumerate(rows):
            acc_ref[r, :] += oi[i]
        yield None

    def inter(sg, carry):
        interleave([inter_stages(0, sg), inter_stages(1, sg)])
        return carry

    lax.fori_loop(0, ngrp, inter, 0)

    acc = acc_ref[...]
    sq = acc * acc
    sq_hi = sq.astype(BF16)
    sq_lo = (sq - sq_hi.astype(F32)).astype(BF16)
    avg = jnp.where(same_head, 1.0 / GLA_DK, 0.0).astype(BF16)
    ms = _dot(sq_hi, avg) + _dot(sq_lo, avg)
    r = r_ref[0].astype(F32)
    o_ref[0] = (acc * lax.rsqrt(ms + EPS) * gn_ref[...] * (r * _sigmoid(r))).astype(o_ref.dtype)


def _gla_attention(p, w2cat, b_gate, g_norm):
    b, t, _ = p.shape
    hd = GLA_HEADS * GLA_DK
    cb = lambda c: (lambda i: (i, 0, c // 256))
    return pl.pallas_call(
        functools.partial(_gla_kernel, t=t),
        grid=(b,),
        in_specs=[
            pl.BlockSpec((1, t, 256), cb(COL_GQ)),
            pl.BlockSpec((1, t, 256), cb(COL_GK)),
            pl.BlockSpec((1, t, 256), cb(COL_GV)),
            pl.BlockSpec((1, t, 256), cb(COL_GR)),
            pl.BlockSpec((1, t, 128), lambda i: (i, 0, COL_GG // 128)),
            pl.BlockSpec((2, 128, 256), lambda i: (0, 0, 0)),
            pl.BlockSpec((2, 256), lambda i: (0, 0)),
            pl.BlockSpec((1, hd), lambda i: (0, 0)),
        ],
        out_specs=pl.BlockSpec((1, t, 256), lambda i: (i, 0, 0)),
        out_shape=jax.ShapeDtypeStruct((b, t, 256), BF16),
        scratch_shapes=[
            pltpu.VMEM((2, t, hd), F32),
            pltpu.VMEM((t, hd), F32),
            pltpu.VMEM((2, t, hd), BF16),
            pltpu.VMEM((2, t, hd), BF16),
            pltpu.VMEM((2, t // GLA_CHUNK, 1, hd), F32),
            pltpu.VMEM((2, hd, hd), F32),
        ],
        compiler_params=_params(("arbitrary",)),
        name="gla_attention",
    )(p, p, p, p, p, w2cat, b_gate, jnp.tile(g_norm.reshape(1, GLA_DK), (1, GLA_HEADS)))


def _rope_tables(seq):
    pos = np.arange(seq)
    half = HEAD_DIM // 2
    inv_freq = ROPE_BASE ** (-np.arange(0, half, 2, dtype=np.float32) / half)
    cos_l, sin_l = [], []
    for p_axis in (pos // GRID_W, pos % GRID_W):
        ang = p_axis.astype(np.float32)[:, None] * inv_freq[None, :]
        cos_l += [np.cos(ang), np.cos(ang)]
        sin_l += [-np.sin(ang), np.sin(ang)]
    cos = np.concatenate(cos_l, axis=-1).astype(np.float32)
    sin = np.concatenate(sin_l, axis=-1).astype(np.float32)
    return jnp.asarray(np.tile(cos, (1, SWA_HEADS))), jnp.asarray(np.tile(sin, (1, SWA_HEADS)))


def _rope(x, cos, sin):
    n = x.shape[-1]
    lane = lax.broadcasted_iota(jnp.int32, x.shape, 1)
    partner = jnp.where((lane % 32) < 16, pltpu.roll(x, n - 16, 1), pltpu.roll(x, 16, 1))
    return x * cos + partner * sin


def _swa_kernel(q_ref, k_ref, v_ref, cos_ref, sin_ref, sink_ref, o_ref, qr_ref, kr_ref, *, t):
    seq = t - CTX_LEN
    nb = seq // SWA_BLOCK
    grp = SWA_HEADS // SWA_KV_HEADS
    kw = 3 * SWA_BLOCK
    scale = HEAD_DIM ** -0.5
    kv_w = SWA_KV_HEADS * HEAD_DIM
    lane = lax.broadcasted_iota(jnp.int32, (1, kv_w), 1)

    def to_kv_lanes(x, h):
        slab = x[:, (h // grp) * kv_w:(h // grp + 1) * kv_w]
        if h % grp != h // grp:
            slab = pltpu.roll(slab, HEAD_DIM, 1)
        return jnp.where(lane // HEAD_DIM == h // grp, slab, 0.0)

    def from_kv_lanes(o4, n):
        slabs = []
        for pair in range(SWA_HEADS // grp):
            lo = o4[(2 * pair) * n:(2 * pair + 1) * n]
            hi = o4[(2 * pair + 1) * n:(2 * pair + 2) * n]
            if pair == 0:
                hi = pltpu.roll(hi, HEAD_DIM, 1)
            else:
                lo = pltpu.roll(lo, HEAD_DIM, 1)
            slabs.append(jnp.where(lane < HEAD_DIM, lo, hi))
        return jnp.concatenate(slabs, axis=1)

    def sink_rows(n):
        row = lax.broadcasted_iota(jnp.int32, (SWA_HEADS * n, 1), 0) // n
        sk = jnp.full((SWA_HEADS * n, 1), sink_ref[0], F32)
        for h in range(1, SWA_HEADS):
            sk = jnp.where(row == h, sink_ref[h], sk)
        return sk

    cos = cos_ref[...]
    sin = sin_ref[...]
    qrot = _rope(q_ref[0, CTX_LEN:, :].astype(F32), cos, sin) * scale
    for h in range(SWA_HEADS):
        qr_ref[h] = to_kv_lanes(qrot, h).astype(BF16)
    kr_ref[...] = _rope(k_ref[0, CTX_LEN:, :].astype(F32), cos[:, :kv_w], sin[:, :kv_w]).astype(BF16)
    kc = k_ref[0, 0:CTX_LEN, :]
    vc = v_ref[0, 0:CTX_LEN, :]
    qc = q_ref[0, 0:CTX_LEN, :].astype(F32) * scale
    q4 = jnp.concatenate([to_kv_lanes(qc, h) for h in range(SWA_HEADS)], axis=0).astype(BF16)
    s = _dot_nt(q4, kc)
    sk = sink_rows(CTX_LEN)
    mx = jnp.maximum(jnp.max(s, axis=-1, keepdims=True), sk)
    p = jnp.exp(s - mx)
    den = jnp.sum(p, axis=-1, keepdims=True) + jnp.exp(sk - mx)
    o_ref[0, 0:CTX_LEN, :] = from_kv_lanes(_dot(p.astype(BF16), vc) / den, CTX_LEN).astype(o_ref.dtype)

    skb = sink_rows(SWA_BLOCK)
    row_b = lax.broadcasted_iota(jnp.int32, (SWA_HEADS * SWA_BLOCK, 1), 0) % SWA_BLOCK
    col_b = lax.broadcasted_iota(jnp.int32, (1, kw), 1)

    def logits(n):
        ws = jnp.clip(n - 1, 0, nb - 3) * SWA_BLOCK
        qrows = pl.ds(pl.multiple_of(n * SWA_BLOCK, SWA_BLOCK), SWA_BLOCK)
        krows = pl.ds(pl.multiple_of(ws, SWA_BLOCK), kw)
        qb = jnp.concatenate([qr_ref[h, qrows, :] for h in range(SWA_HEADS)], axis=0)
        return _dot_nt(qb, kr_ref[krows, :]), _dot_nt(qb, kc), n, ws

    def attend(s_b, s_c, n, ws):
        valid = jnp.abs(n * SWA_BLOCK + row_b - (ws + col_b)) <= SWA_WINDOW
        s_b = jnp.where(valid, s_b, NEG_INF)
        mxb = jnp.maximum(jnp.maximum(jnp.max(s_b, axis=-1, keepdims=True),
                                      jnp.max(s_c, axis=-1, keepdims=True)), skb)
        p_b = jnp.exp(s_b - mxb)
        p_c = jnp.exp(s_c - mxb)
        denb = (jnp.sum(p_b, axis=-1, keepdims=True) + jnp.sum(p_c, axis=-1, keepdims=True)
                + jnp.exp(skb - mxb))
        vrows = pl.ds(pl.multiple_of(CTX_LEN + ws, SWA_BLOCK), kw)
        o4 = (_dot(p_b.astype(BF16), v_ref[0, vrows, :]) + _dot(p_c.astype(BF16), vc)) / denb
        orows = pl.ds(pl.multiple_of(CTX_LEN + n * SWA_BLOCK, SWA_BLOCK), SWA_BLOCK)
        o_ref[0, orows, :] = from_kv_lanes(o4, SWA_BLOCK).astype(o_ref.dtype)

    def block_run(j, carry):
        ahead = logits(SWA_RUN * j)
        for i in range(1, SWA_RUN):
            cur, ahead = ahead, logits(SWA_RUN * j + i)
            attend(*cur)
        attend(*ahead)
        return carry

    lax.fori_loop(0, nb // SWA_RUN, block_run, 0)


def _swa_attention(p, cos, sin, sink):
    b, t, _ = p.shape
    seq = t - CTX_LEN
    return pl.pallas_call(
        functools.partial(_swa_kernel, t=t),
        grid=(b,),
        in_specs=[
            pl.BlockSpec((1, t, 256), lambda i: (i, 0, COL_SQ // 256)),
            pl.BlockSpec((1, t, 128), lambda i: (i, 0, COL_SK // 128)),
            pl.BlockSpec((1, t, 128), lambda i: (i, 0, COL_SV // 128)),
            pl.BlockSpec((seq, 256), lambda i: (0, 0)),
            pl.BlockSpec((seq, 256), lambda i: (0, 0)),
            pl.BlockSpec(memory_space=pltpu.SMEM),
        ],
        out_specs=pl.BlockSpec((1, t, 256), lambda i: (i, 0, 0)),
        out_shape=jax.ShapeDtypeStruct((b, t, 256), BF16),
        scratch_shapes=[pltpu.VMEM((SWA_HEADS, seq, 128), BF16), pltpu.VMEM((seq, 128), BF16)],
        compiler_params=_params(("arbitrary",)),
        name="swa_attention",
    )(p, p, p, cos, sin, sink)


def _out_proj_kernel(u_ref, yg_ref, d_ref, wg_ref, na_ref, gla_ref, swa_ref, wo_ref, g_ref,
                     modl_ref, modc_ref, x_ref, o_ref, *, tm, first):
    ys = jnp.concatenate(
        [jnp.concatenate([yg_ref[g, j, 0] for g in range(S5_GROUPS)], axis=0).T for j in range(tm // S5_TC)], axis=0)
    y = d_ref[...] * u_ref[0].astype(F32) + ys
    z = _gelu(y)
    a = z * _sigmoid(_dot(z.astype(BF16), wg_ref[...]))
    acc = _dot(jnp.concatenate([a.astype(BF16), na_ref[0], gla_ref[0], swa_ref[0]], axis=1), wo_ref[...])
    row = (pl.program_id(1) + first) * tm + lax.broadcasted_iota(jnp.int32, (tm, 1), 0)
    gate = jnp.where(row < CTX_LEN, modc_ref[0, 2:3, :], modl_ref[0, 2:3, :])
    o_ref[0] = x_ref[0] + gate * (_rms(acc) * g_ref[...])


def _out_proj(p, yg, s5_d, w_glu, y_na, y_gla, y_swa, w_out, g_post, modl, modc, x, *, tm, skip_ctx=False):
    b, t_in, d = x.shape
    first = CTX_LEN // tm if skip_ctx else 0
    assert tm % S5_TC == 0 and (not skip_ctx or CTX_LEN % tm == 0)
    t = t_in - first * tm
    nc = tm // S5_TC
    tok = lambda w: pl.BlockSpec((1, tm, w), lambda i, j: (i, j + first, 0))
    full = lambda a: pl.BlockSpec(a.shape, lambda i, j: (0,) * a.ndim)
    chunked = pl.BlockSpec((S5_GROUPS, nc, 1, S5_GROUP_CH, S5_TC), lambda i, j: (0, j + first, i, 0, 0))
    return pl.pallas_call(
        functools.partial(_out_proj_kernel, tm=tm, first=first),
        grid=(b, t // tm),
        in_specs=[
            tok(256), chunked, full(s5_d), full(w_glu), tok(256), tok(256), tok(256),
            full(w_out), full(g_post),
            pl.BlockSpec((1, 6, d), lambda i, j: (i, 0, 0)),
            pl.BlockSpec((1, 6, d), lambda i, j: (0, 0, 0)),
            tok(d),
        ],
        out_specs=pl.BlockSpec((1, tm, d), lambda i, j: (i, j, 0)),
        out_shape=jax.ShapeDtypeStruct((b, t, d), F32),
        compiler_params=_params(("arbitrary", "arbitrary")),
        name="out_proj",
    )(p, yg, s5_d, w_glu, y_na, y_gla, y_swa, w_out, g_post, modl, modc, x)


def _ffn_down_kernel(gt_ref, vl_ref, prev_ref, next_ref, cw_ref, wd_ref, g_ref, modl_ref, modc_ref, x_ref, o_ref,
                     a_ref, *, tm, t, ctx_rows):
    h = FFN_HALO
    row0 = pl.program_id(1) * tm
    w0, w1, w2 = cw_ref[0:1, :], cw_ref[1:2, :], cw_ref[2:3, :]
    g = gt_ref[0].astype(F32)
    gp = pltpu.roll(g, 1, 0)
    gn = pltpu.roll(g, tm - 1, 0)
    w0b, w1b, w2b = w0.astype(BF16), w1.astype(BF16), w2.astype(BF16)
    a_ref[...] = _gelu(gp.astype(BF16) * w0b + gt_ref[0] * w1b + gn.astype(BF16) * w2b) * vl_ref[0]
    loc = lax.broadcasted_iota(jnp.int32, (h, 1), 0)
    has_prev = jnp.where((row0 == 0) | (row0 == ctx_rows), 0.0, 1.0)
    has_next = jnp.where((row0 + tm == ctx_rows) | (row0 + tm == t), 0.0, 1.0)
    gp_top = jnp.where(loc == 0, prev_ref[0, h - 1:h, :].astype(F32) * has_prev, gp[0:h])
    top = gp_top * w0 + g[0:h] * w1 + gn[0:h] * w2
    a_ref[0:h, :] = _gelu(top.astype(BF16)) * vl_ref[0, 0:h, :]
    gn_bot = jnp.where(loc == h - 1, next_ref[0, 0:1, :].astype(F32) * has_next, gn[tm - h:tm])
    bot = gp[tm - h:tm] * w0 + g[tm - h:tm] * w1 + gn_bot * w2
    a_ref[tm - h:tm, :] = _gelu(bot.astype(BF16)) * vl_ref[0, tm - h:tm, :]
    acc = _dot(a_ref[...], wd_ref[...])
    gate = jnp.where(row0 < ctx_rows, modc_ref[0, 5:6, :], modl_ref[0, 5:6, :])
    o_ref[0] = x_ref[0] + gate * (_rms(acc) * g_ref[...])


def _ffn_down(gv, conv_w, w_down, g_post, modl, modc, x, *, tm, ctx_rows=CTX_LEN):
    b, t, d = x.shape
    assert ctx_rows % tm == 0 and t % tm == 0 and tm > 2 * FFN_HALO
    nbh = tm // FFN_HALO
    lasth = t // FFN_HALO - 1
    return pl.pallas_call(
        functools.partial(_ffn_down_kernel, tm=tm, t=t, ctx_rows=ctx_rows),
        grid=(b, t // tm),
        in_specs=[
            pl.BlockSpec((1, tm, D_FF), lambda i, j: (i, j, 0)),
            pl.BlockSpec((1, tm, D_FF), lambda i, j: (i, j, 1)),
            pl.BlockSpec((1, FFN_HALO, D_FF), lambda i, j: (i, jnp.maximum(j * nbh - 1, 0), 0)),
            pl.BlockSpec((1, FFN_HALO, D_FF), lambda i, j: (i, jnp.minimum((j + 1) * nbh, lasth), 0)),
            pl.BlockSpec((3, D_FF), lambda i, j: (0, 0)),
            pl.BlockSpec((D_FF, d), lambda i, j: (0, 0)),
            pl.BlockSpec((1, d), lambda i, j: (0, 0)),
            pl.BlockSpec((1, 6, d), lambda i, j: (i, 0, 0)),
            pl.BlockSpec((1, 6, d), lambda i, j: (0, 0, 0)),
            pl.BlockSpec((1, tm, d), lambda i, j: (i, j, 0)),
        ],
        out_specs=pl.BlockSpec((1, tm, d), lambda i, j: (i, j, 0)),
        out_shape=jax.ShapeDtypeStruct((b, t, d), F32),
        scratch_shapes=[pltpu.VMEM((tm, D_FF), BF16)],
        compiler_params=_params(("arbitrary", "arbitrary")),
        name="ffn_down",
    )(gv, gv, gv, gv, conv_w, w_down, g_post, modl, modc, x)


def _reorder_w_in(w_in):
    sizes = (256, 256, 256, 256, 256, 256, 256, 16, 16, 256, 256, 128, 128)
    offs = np.concatenate([[0], np.cumsum(sizes)])
    seg = lambda i: w_in[..., offs[i]:offs[i + 1]]
    pad = jnp.zeros(w_in.shape[:-1] + (P_WIDTH - COL_GG - 2 * GLA_RANK,), w_in.dtype)
    return jnp.concatenate([seg(0), seg(1), seg(2), seg(3), seg(4), seg(5), seg(6), seg(9),
                            seg(10), seg(11), seg(12), seg(7), seg(8), pad], axis=-1)


def kernel(x, c, ctx, c_ctx, w_mod, b_mod, g_pre_mix, g_post_mix, g_pre_ffn, g_post_ffn, w_in, w_out, s5_lam_re, s5_lam_im, s5_b_re, s5_b_im, s5_c_re, s5_c_im, s5_log_step, s5_d, s5_w_glu, na_rpb, gla_w_gate2, gla_b_gate, gla_g_norm, swa_sink, ffn_w_up, ffn_conv, ffn_w_down):
    bsz, seq, d = x.shape
    depth = w_mod.shape[0]
    t = CTX_LEN + seq
    tm = 768

    cvec = jnp.concatenate([c, c_ctx[None, :]], axis=0)
    cvec = jnp.pad(cvec, ((0, (-cvec.shape[0]) % 8), (0, 0)))
    mod = _modulation(cvec, w_mod, b_mod).reshape(depth, cvec.shape[0], 6, d)

    w_in_r = _reorder_w_in(w_in.astype(BF16))
    w_out_b = w_out.astype(BF16)
    w_glu_b = s5_w_glu.astype(BF16)
    w_up_b = ffn_w_up.astype(BF16)
    w_down_b = ffn_w_down.astype(BF16)
    w2 = jnp.stack([jnp.pad(gla_w_gate2[:, 0], ((0, 0), (0, 128 - GLA_RANK), (0, 0))),
                    jnp.pad(gla_w_gate2[:, 1], ((0, 0), (GLA_RANK, 128 - 2 * GLA_RANK), (0, 0)))], axis=1).astype(BF16)
    cos, sin = _rope_tables(seq)

    xc = jnp.concatenate([ctx, x], axis=1)
    for l in range(depth):
        modl = mod[l, :bsz]
        modc = mod[l, bsz:bsz + 1]
        p, ug = _nm_matmul(xc, g_pre_mix[l][None], modl, modc, w_in_r[l], k_shift=0, k_scale=1, tm=tm, tn=P_WIDTH,
                           s5_chunks=True)
        tabs = _s5_prep(s5_lam_re[l], s5_lam_im[l], s5_b_re[l], s5_b_im[l], s5_c_re[l], s5_c_im[l], s5_log_step[l])
        ys = _s5_scan(ug, *tabs)
        y_na = _na_attention(p, na_rpb[l])
        y_gla = _gla_attention(p, w2[l], gla_b_gate[l], gla_g_norm[l][None])
        y_swa = _swa_attention(p, cos, sin, swa_sink[l])
        last = l == depth - 1
        ctx_rows = 0 if last else CTX_LEN
        xc = _out_proj(p, ys, s5_d[l][None], w_glu_b[l], y_na, y_gla, y_swa, w_out_b[l],
                       g_post_mix[l][None], modl, modc, xc, tm=CTX_LEN if last else tm, skip_ctx=last)
        gv = _nm_matmul(xc, g_pre_ffn[l][None], modl, modc, w_up_b[l], k_shift=3, k_scale=4,
                        tm=1024 if last else tm, tn=D_FF, ctx_rows=ctx_rows)
        xc = _ffn_down(gv, ffn_conv[l], w_down_b[l], g_post_ffn[l][None], modl, modc, xc, tm=256, ctx_rows=ctx_rows)
    return xc
```

```python
import functools
import math

import numpy as np
import jax
import jax.numpy as jnp
from jax import lax
from jax.experimental import pallas as pl
from jax.experimental.pallas import tpu as pltpu

F32 = jnp.float32
BF16 = jnp.bfloat16

D_MODEL = 1024
GRID_W = 64
CTX_LEN = 256
HEAD_DIM = 64
EPS = 1e-6
NEG_INF = -1e30

S5_CH = 256
S5_GROUP_CH = 16
S5_GROUPS = S5_CH // S5_GROUP_CH
S5_STATE = 64
S5_TC = 128
S5_K = S5_TC * S5_GROUP_CH

NA_HEADS = 4
NA_KH = 8
NA_KW = 16
NA_QROWS = 4
NA_WROWS = NA_QROWS + NA_KH
NA_RUN = 2

GLA_HEADS = 4
GLA_DK = 64
GLA_RANK = 16
GLA_TAU = 16.0
GLA_CHUNK = 64
GLA_GROUP = CTX_LEN // GLA_CHUNK
GLA_RUN = 3

SWA_HEADS = 4
SWA_KV_HEADS = 2
SWA_WINDOW = 128
SWA_BLOCK = 128
SWA_RUN = 4
ROPE_BASE = 10000.0

D_FF = 2816
NM_PIECE_COLS = D_FF
FFN_HALO = 16

COL_S5, COL_NAQ, COL_NAK, COL_NAV = 0, 256, 512, 768
COL_GQ, COL_GK, COL_GV, COL_GR = 1024, 1280, 1536, 1792
COL_SQ, COL_SK, COL_SV, COL_GG = 2048, 2304, 2432, 2560
P_WIDTH = 2688

VMEM_LIMIT = 56 * 1024 * 1024


def _dot(a, b):
    return jnp.dot(a, b, preferred_element_type=F32)


def _dot_exact(a, b):
    return jnp.dot(a, b, preferred_element_type=F32, precision=lax.Precision.HIGHEST)


def _dot_nt(a, b):
    return lax.dot_general(a, b, (((1,), (1,)), ((), ())), preferred_element_type=F32)


def _dot_tn(a, b):
    return lax.dot_general(a, b, (((0,), (0,)), ((), ())), preferred_element_type=F32)


def _gelu(x):
    k = math.sqrt(2.0 / math.pi)
    return x * (0.5 + 0.5 * jnp.tanh(x * (k + (k * 0.044715) * (x * x))))


def _sigmoid(x):
    return 1.0 / (1.0 + jnp.exp(-x))


def _rms(x):
    return x * lax.rsqrt(jnp.mean(x * x, axis=-1, keepdims=True) + EPS)


def _params(sem):
    return pltpu.CompilerParams(dimension_semantics=sem, vmem_limit_bytes=VMEM_LIMIT)


def _mod_kernel(c_ref, w_ref, b_ref, o_ref):
    c = c_ref[...]
    s = (c * _sigmoid(c)).astype(BF16)
    o_ref[0] = _dot(s, w_ref[0].astype(BF16)) + b_ref[0]


def _modulation(cvec, w_mod, b_mod):
    depth, d, n = w_mod.shape
    r = cvec.shape[0]
    tn = 1536
    return pl.pallas_call(
        _mod_kernel,
        grid=(depth, n // tn),
        in_specs=[
            pl.BlockSpec((r, d), lambda l, j: (0, 0)),
            pl.BlockSpec((1, d, tn), lambda l, j: (l, 0, j)),
            pl.BlockSpec((1, 1, tn), lambda l, j: (l, 0, j)),
        ],
        out_specs=pl.BlockSpec((1, r, tn), lambda l, j: (l, 0, j)),
        out_shape=jax.ShapeDtypeStruct((depth, r, n), F32),
        compiler_params=_params(("arbitrary", "arbitrary")),
        name="modulation",
    )(cvec, w_mod, b_mod.reshape(depth, 1, n))


def _nm_matmul_kernel(x0_ref, xn_ref, g_ref, modl0_ref, modln_ref, modc_ref, w_ref, o_ref, *rest,
                      tm, nt, nk, k_shift, k_scale, ctx_rows, s5_chunks):
    ha_ref, hb_ref = rest[-2], rest[-1]
    tile = pl.program_id(0) * nt + pl.program_id(1)
    k = pl.program_id(2)
    part = tm // nk

    def normed(x, modl_ref, row0):
        y = _rms(x) * g_ref[...]
        row = row0 + lax.broadcasted_iota(jnp.int32, (x.shape[0], 1), 0)
        is_ctx = row < ctx_rows
        scale = jnp.where(is_ctx, modc_ref[0, k_scale:k_scale + 1, :], modl_ref[0, k_scale:k_scale + 1, :])
        shift = jnp.where(is_ctx, modc_ref[0, k_shift:k_shift + 1, :], modl_ref[0, k_shift:k_shift + 1, :])
        return (y * (1.0 + scale) + shift).astype(BF16)

    @pl.when((tile == 0) & (k == 0))
    def _():
        ha_ref[...] = normed(x0_ref[0], modl0_ref, 0)

    def step(cur_ref, nxt_ref):
        rows = pl.ds(pl.multiple_of(k * part, part), part)
        nxt_ref[rows, :] = normed(xn_ref[0, rows, :], modln_ref, ((pl.program_id(1) + 1) % nt) * tm + k * part)
        tn = w_ref.shape[1]
        pieces = max(1, tn // NM_PIECE_COLS)
        wid = tn // pieces
        for c in range(pieces):
            res = _dot(cur_ref[...], w_ref[:, c * wid:(c + 1) * wid])
            o_ref[0, :, c * wid:(c + 1) * wid] = res.astype(o_ref.dtype)
        if s5_chunks:
            ug_ref = rest[0]
            for j in range(tm // S5_TC):
                z = res[j * S5_TC:(j + 1) * S5_TC, COL_S5:COL_S5 + S5_CH].T.astype(BF16)
                for g in range(S5_GROUPS):
                    ug_ref[g, j, 0] = z[g * S5_GROUP_CH:(g + 1) * S5_GROUP_CH, :]

    @pl.when(tile % 2 == 0)
    def _():
        step(ha_ref, hb_ref)

    @pl.when(tile % 2 == 1)
    def _():
        step(hb_ref, ha_ref)


def _nm_matmul(x, g, modl, modc, w, *, k_shift, k_scale, tm, tn, ctx_rows=CTX_LEN, s5_chunks=False):
    b, t, d = x.shape
    n = w.shape[1]
    assert t % tm == 0 and n % tn == 0 and (not s5_chunks or (tn == n and tm % S5_TC == 0))
    out_specs = [pl.BlockSpec((1, tm, tn), lambda i, j, k: (i, j, k))]
    out_shape = [jax.ShapeDtypeStruct((b, t, n), BF16)]
    if s5_chunks:
        out_specs.append(pl.BlockSpec((S5_GROUPS, tm // S5_TC, 1, S5_GROUP_CH, S5_TC), lambda i, j, k: (0, j, i, 0, 0)))
        out_shape.append(jax.ShapeDtypeStruct((S5_GROUPS, t // S5_TC, b, S5_GROUP_CH, S5_TC), BF16))
    nt, nk = t // tm, n // tn
    assert (tm // nk) % 16 == 0 and (not s5_chunks or n < 2 * NM_PIECE_COLS)
    once = pl.Buffered(1)

    def next_tile(i, j):
        s = i * nt + j + 1
        return jnp.minimum(s // nt, b - 1), s % nt

    out = pl.pallas_call(
        functools.partial(_nm_matmul_kernel, tm=tm, nt=nt, nk=nk, k_shift=k_shift, k_scale=k_scale,
                          ctx_rows=ctx_rows, s5_chunks=s5_chunks),
        grid=(b, nt, nk),
        in_specs=[
            pl.BlockSpec((1, tm, d), lambda i, j, k: (0, 0, 0), pipeline_mode=once),
            pl.BlockSpec((1, tm, d), lambda i, j, k: next_tile(i, j) + (0,)),
            pl.BlockSpec((1, d), lambda i, j, k: (0, 0)),
            pl.BlockSpec((1, 6, d), lambda i, j, k: (0, 0, 0)),
            pl.BlockSpec((1, 6, d), lambda i, j, k: (next_tile(i, j)[0], 0, 0)),
            pl.BlockSpec((1, 6, d), lambda i, j, k: (0, 0, 0)),
            pl.BlockSpec((d, tn), lambda i, j, k: (0, k), pipeline_mode=once if nk == 1 else None),
        ],
        out_specs=out_specs,
        out_shape=out_shape,
        scratch_shapes=[pltpu.VMEM((tm, d), BF16), pltpu.VMEM((tm, d), BF16)],
        compiler_params=_params(("arbitrary", "arbitrary", "arbitrary")),
        name="norm_mod_matmul",
    )(x, x, g, modl, modl, modc, w)
    return tuple(out) if s5_chunks else out[0]


def _s5_prep_kernel(lr_ref, lc_ref, ls_ref, bt_ref, ct_ref, t_ref, s_ref, c_ref, a_ref, kap_ref):
    tc = S5_TC
    p = S5_STATE
    lane = lax.broadcasted_iota(jnp.int32, (1, tc), 1).astype(F32)
    sub = lax.broadcasted_iota(jnp.int32, (tc, 1), 0).astype(F32)
    s_idx = lax.broadcasted_iota(jnp.int32, (tc, tc), 0)
    t_idx = lax.broadcasted_iota(jnp.int32, (tc, tc), 1)
    for d in range(2):
        step = jnp.exp(ls_ref[0, d:d + 1, :])
        lre = lr_ref[0, d, 0:1, :]
        lim = lr_ref[0, d, 1:2, :]
        zr = lre * step
        zi = lim * step
        mag = jnp.exp(zr)
        nr = mag * jnp.cos(zi) - 1.0
        ni = mag * jnp.sin(zi)
        den = lre * lre + lim * lim
        cr = (nr * lre + ni * lim) / den
        ci = (ni * lre - nr * lim) / den
        btr = bt_ref[0, d, 0]
        bti = bt_ref[0, d, 1]
        bbr = cr * btr - ci * bti
        bbi = cr * bti + ci * btr
        ma = jnp.exp(zr * tc)
        ar = ma * jnp.cos(zi * tc)
        ai = ma * jnp.sin(zi * tc)
        a_ref[0, 0:1, d * 2 * p:d * 2 * p + p] = ar
        a_ref[0, 0:1, d * 2 * p + p:(d + 1) * 2 * p] = ar
        a_ref[0, 1:2, d * 2 * p:d * 2 * p + p] = -ai
        a_ref[0, 1:2, d * 2 * p + p:(d + 1) * 2 * p] = ai
        zrc = lc_ref[0, d, :, 0:1] * step
        zic = lc_ref[0, d, :, 1:2] * step

        def powers(tau):
            m = jnp.exp(zrc * tau)
            ang = zic * tau
            return m * jnp.cos(ang), m * jnp.sin(ang)

        if d == 0:
            pkr, pki = powers(lane)
            pcr, pci = powers(lane + 1.0)
            es = (tc - 1.0) - sub
        else:
            pkr, pki = powers(jnp.where(lane == 0.0, 0.0, tc - lane))
            pcr, pci = powers(tc - lane)
            es = sub
        ctr = ct_ref[0, d, 0]
        cti = ct_ref[0, d, 1]
        for co in range(S5_GROUP_CH):
            ccr = ctr[:, co:co + 1]
            cci = cti[:, co:co + 1]
            kap_ref[d, co] = (_dot_exact(bbr, ccr * pkr - cci * pki)
                              - _dot_exact(bbi, ccr * pki + cci * pkr))
            cols = slice(co * tc, (co + 1) * tc)
            c_ref[0, d * 2 * p:d * 2 * p + p, cols] = (ccr * pcr - cci * pci).astype(BF16)
            c_ref[0, d * 2 * p + p:(d + 1) * 2 * p, cols] = (-(ccr * pci + cci * pcr)).astype(BF16)
        me = jnp.exp(zr * es)
        er = me * jnp.cos(zi * es)
        ei = me * jnp.sin(zi * es)
        for cin in range(S5_GROUP_CH):
            br = bbr[cin:cin + 1, :]
            bi = bbi[cin:cin + 1, :]
            rows = slice(cin * tc, (cin + 1) * tc)
            s_ref[0, rows, d * 2 * p:d * 2 * p + p] = (er * br - ei * bi).astype(BF16)
            s_ref[0, rows, d * 2 * p + p:(d + 1) * 2 * p] = (er * bi + ei * br).astype(BF16)

    for co in range(S5_GROUP_CH):
        cols = slice(co * tc, (co + 1) * tc)

        def toeplitz(i, carry):
            cins = [i * 8 + j for j in range(8)]
            kr = [kap_ref[1, co, pl.ds(cin, 1), :] for cin in cins]
            kf = [kap_ref[0, co, pl.ds(cin, 1), :] + jnp.where(lane == 0.0, r, 0.0)
                  for cin, r in zip(cins, kr)]
            mixed = [jnp.where(t_idx + s_idx < tc, jnp.broadcast_to(f, (tc, tc)), jnp.broadcast_to(r, (tc, tc)))
                     for f, r in zip(kf, kr)]
            blk = [pltpu.roll(m, 0, 1, stride=1, stride_axis=0).astype(BF16) for m in mixed]
            for cin, b in zip(cins, blk):
                t_ref[0, pl.ds(pl.multiple_of(cin * tc, tc), tc), cols] = b
            return carry

        lax.fori_loop(0, S5_GROUP_CH // 8, toeplitz, 0)


def _s5_prep(lam_re, lam_im, b_re, b_im, c_re, c_im, log_step):
    g = S5_GROUPS
    lam = jnp.stack([lam_re, lam_im], axis=-2).astype(F32).transpose(1, 0, 2, 3)
    lam_c = lam.transpose(0, 1, 3, 2)
    ls = log_step.astype(F32).T[:, :, None]
    bt = jnp.stack([b_re, b_im], axis=2).astype(F32).transpose(1, 0, 2, 4, 3)
    ct = jnp.stack([c_re, c_im], axis=2).astype(F32).transpose(1, 0, 2, 4, 3)
    blk = lambda a: pl.BlockSpec((1,) + a.shape[1:], lambda i: (i,) + (0,) * (a.ndim - 1))
    return pl.pallas_call(
        _s5_prep_kernel,
        grid=(g,),
        in_specs=[blk(lam), blk(lam_c), blk(ls), blk(bt), blk(ct)],
        out_specs=[
            pl.BlockSpec((1, S5_K, S5_K), lambda i: (i, 0, 0)),
            pl.BlockSpec((1, S5_K, 4 * S5_STATE), lambda i: (i, 0, 0)),
            pl.BlockSpec((1, 4 * S5_STATE, S5_K), lambda i: (i, 0, 0)),
            pl.BlockSpec((1, 2, 4 * S5_STATE), lambda i: (i, 0, 0)),
        ],
        out_shape=[
            jax.ShapeDtypeStruct((g, S5_K, S5_K), BF16),
            jax.ShapeDtypeStruct((g, S5_K, 4 * S5_STATE), BF16),
            jax.ShapeDtypeStruct((g, 4 * S5_STATE, S5_K), BF16),
            jax.ShapeDtypeStruct((g, 2, 4 * S5_STATE), F32),
        ],
        scratch_shapes=[pltpu.VMEM((2, S5_GROUP_CH, S5_GROUP_CH, S5_TC), F32)],
        compiler_params=_params(("arbitrary",)),
        name="s5_prep",
    )(lam, lam_c, ls, bt, ct)


def _s5_scan_kernel(u_ref, t_ref, s_ref, c_ref, a_ref, y_ref, loc_ref, hin_ref, *, bsz, nch):
    w = 2 * S5_STATE
    u = u_ref[0]
    loc_ref[...] = _dot(u, s_ref[0])
    a1f, a2f = a_ref[0, 0:1, 0:w], a_ref[0, 1:2, 0:w]
    a1r, a2r = a_ref[0, 0:1, w:2 * w], a_ref[0, 1:2, w:2 * w]
    nctx = CTX_LEN // S5_TC

    def body(i, carry):
        hf, hr = carry
        kr = jnp.where(i < nctx, nctx - 1 - i, nch + nctx - 1 - i)
        rf = pl.ds(pl.multiple_of(i * bsz, bsz), bsz)
        rr = pl.ds(pl.multiple_of(kr * bsz, bsz), bsz)
        hin_ref[rf, 0:w] = hf
        hin_ref[rr, w:2 * w] = hr
        hf = a1f * hf + a2f * pltpu.roll(hf, S5_STATE, 1) + loc_ref[rf, 0:w]
        hr = a1r * hr + a2r * pltpu.roll(hr, S5_STATE, 1) + loc_ref[rr, w:2 * w]
        return hf, hr

    zero = jnp.zeros((bsz, w), F32)
    lax.fori_loop(0, nch, body, (zero, zero))
    y_ref[0] = _dot(u, t_ref[0]) + _dot(hin_ref[...].astype(BF16), c_ref[0])


def _s5_scan(ug, tmat, smat, cmat, amat):
    g, nch, b = ug.shape[:3]
    m = nch * b
    yg = pl.pallas_call(
        functools.partial(_s5_scan_kernel, bsz=b, nch=nch),
        grid=(g,),
        in_specs=[
            pl.BlockSpec((1, m, S5_K), lambda i: (i, 0, 0)),
            pl.BlockSpec((1, S5_K, S5_K), lambda i: (i, 0, 0)),
            pl.BlockSpec((1, S5_K, 4 * S5_STATE), lambda i: (i, 0, 0)),
            pl.BlockSpec((1, 4 * S5_STATE, S5_K), lambda i: (i, 0, 0)),
            pl.BlockSpec((1, 2, 4 * S5_STATE), lambda i: (i, 0, 0)),
        ],
        out_specs=pl.BlockSpec((1, m, S5_K), lambda i: (i, 0, 0)),
        out_shape=jax.ShapeDtypeStruct((g, m, S5_K), F32),
        scratch_shapes=[pltpu.VMEM((m, 4 * S5_STATE), F32), pltpu.VMEM((m, 4 * S5_STATE), F32)],
        compiler_params=_params(("arbitrary",)),
        name="s5_scan",
    )(ug.reshape(g, m, S5_K), tmat, smat, cmat, amat)
    return yg.reshape(g, nch, b, S5_GROUP_CH, S5_TC)


def _na_block_geometry(rows):
    geo = []
    for r0 in (0, NA_QROWS, rows - NA_QROWS):
        geo.append((r0, min(max(r0 - NA_KH // 2, 0), rows - NA_WROWS)))
    return geo


def _na_fill_bias(rpb_ref, bt_ref, rows):
    w = GRID_W
    qc = lax.broadcasted_iota(jnp.int32, (w, w), 0)
    kc = lax.broadcasted_iota(jnp.int32, (w, w), 1)
    ws = jnp.clip(qc - NA_KW // 2, 0, w - NA_KW)
    in_win = (kc >= ws) & (kc < ws + NA_KW)
    neg = jnp.full((w, w), NEG_INF, F32)
    for h in range(NA_HEADS):
        cm = []
        for dr in range(2 * NA_KH - 1):
            v = jnp.broadcast_to(rpb_ref[h, dr:dr + 1, :], (w, 128))
            v = pltpu.roll(v, 128 - (NA_KW - 1), 1, stride=1, stride_axis=0)
            cm.append(jnp.where(in_win, v[:, :w], NEG_INF))
        for kind, (r0, w0) in enumerate(_na_block_geometry(rows)):
            for qr in range(NA_QROWS):
                start = min(max(r0 + qr - NA_KH // 2, 0), rows - NA_KH)
                for kr in range(0, NA_WROWS, 2):
                    pair = []
                    for k in (kr, kr + 1):
                        inside = start <= w0 + k < start + NA_KH
                        pair.append(cm[(w0 + k) - (r0 + qr) + NA_KH - 1] if inside else neg)
                    r_lo = (h * NA_QROWS + qr) * w
                    bt_ref[kind, r_lo:r_lo + w, kr * w:(kr + 2) * w] = jnp.concatenate(pair, axis=1)


def _na_kernel(q_ref, k_ref, v_ref, rpb_ref, o_ref, bt_ref, *, rows):
    qn = NA_QROWS * GRID_W
    kn = NA_WROWS * GRID_W
    nblk = rows // NA_QROWS
    scale = HEAD_DIM ** -0.5

    @pl.when(pl.program_id(0) == 0)
    def _():
        _na_fill_bias(rpb_ref, bt_ref, rows)

    lane_head = lax.broadcasted_iota(jnp.int32, (1, NA_HEADS * HEAD_DIM), 1) // HEAD_DIM

    def stack_heads(x):
        return jnp.concatenate([jnp.where(lane_head == h, x, 0.0) for h in range(NA_HEADS)], axis=0)

    def pick_heads(o4, n):
        out = o4[0:n]
        for h in range(1, NA_HEADS):
            out = jnp.where(lane_head == h, o4[h * n:(h + 1) * n], out)
        return out

    kc = k_ref[0, 0:CTX_LEN, :]
    vc = v_ref[0, 0:CTX_LEN, :]
    s = _dot_nt(stack_heads(q_ref[0, 0:CTX_LEN, :] * scale), kc)
    p = jnp.exp(s - jnp.max(s, axis=-1, keepdims=True))
    o = _dot(p.astype(BF16), vc) / jnp.sum(p, axis=-1, keepdims=True)
    o_ref[0, 0:CTX_LEN, :] = pick_heads(o, CTX_LEN).astype(o_ref.dtype)

    def logits(i):
        w0 = jnp.clip(i * NA_QROWS - NA_KH // 2, 0, rows - NA_WROWS)
        kind = jnp.where(i > 0, 1, 0) + jnp.where(i == nblk - 1, 1, 0)
        qrows = pl.ds(pl.multiple_of(CTX_LEN + i * qn, qn), qn)
        krows = pl.ds(pl.multiple_of(CTX_LEN + w0 * GRID_W, GRID_W), kn)
        q4 = stack_heads(q_ref[0, qrows, :] * scale)
        return _dot_nt(q4, k_ref[0, krows, :]), _dot_nt(q4, kc), kind, qrows, krows

    def attend(s_nb, s_cx, kind, qrows, krows):
        p_nb, p_cx, inv = [], [], []
        for h in range(NA_HEADS):
            r = slice(h * qn, (h + 1) * qn)
            a = s_nb[r] + bt_ref[kind, r, :]
            c = s_cx[r]
            mx = jnp.maximum(jnp.max(a, axis=-1, keepdims=True), jnp.max(c, axis=-1, keepdims=True))
            ea = jnp.exp(a - mx)
            ec = jnp.exp(c - mx)
            inv.append(1.0 / (jnp.sum(ea, axis=-1, keepdims=True) + jnp.sum(ec, axis=-1, keepdims=True)))
            p_nb.append(ea.astype(BF16))
            p_cx.append(ec.astype(BF16))
        o4 = (_dot(jnp.concatenate(p_nb, axis=0), v_ref[0, krows, :])
              + _dot(jnp.concatenate(p_cx, axis=0), vc)) * jnp.concatenate(inv, axis=0)
        o_ref[0, qrows, :] = pick_heads(o4, qn).astype(o_ref.dtype)

    def block_run(j, carry):
        ahead = logits(NA_RUN * j)
        for i in range(1, NA_RUN):
            cur, ahead = ahead, logits(NA_RUN * j + i)
            attend(*cur)
        attend(*ahead)
        return carry

    lax.fori_loop(0, nblk // NA_RUN, block_run, 0)


def _na_attention(p, rpb):
    b, t, _ = p.shape
    rows = (t - CTX_LEN) // GRID_W
    rpb_p = jnp.pad(rpb.astype(F32), ((0, 0), (0, 1), (0, 128 - rpb.shape[2])))
    cb = lambda c: (lambda i: (i, 0, c // 256))
    return pl.pallas_call(
        functools.partial(_na_kernel, rows=rows),
        grid=(b,),
        in_specs=[
            pl.BlockSpec((1, t, 256), cb(COL_NAQ)),
            pl.BlockSpec((1, t, 256), cb(COL_NAK)),
            pl.BlockSpec((1, t, 256), cb(COL_NAV)),
            pl.BlockSpec(rpb_p.shape, lambda i: (0, 0, 0)),
        ],
        out_specs=pl.BlockSpec((1, t, 256), lambda i: (i, 0, 0)),
        out_shape=jax.ShapeDtypeStruct((b, t, 256), BF16),
        scratch_shapes=[pltpu.VMEM((3, NA_HEADS * NA_QROWS * GRID_W, NA_WROWS * GRID_W), F32)],
        compiler_params=_params(("arbitrary",)),
        name="na_attention",
    )(p, p, p, rpb_p)


def _log_sigmoid(x):
    return jnp.minimum(x, 0.0) - jnp.log(1.0 + jnp.exp(-jnp.abs(x)))


def _gla_kernel(q_ref, k_ref, v_ref, r_ref, gg_ref, w2_ref, bg_ref, gn_ref, o_ref,
                la_ref, acc_ref, qd_ref, ke_ref, el_ref, st_ref, *, t):
    c = GLA_CHUNK
    ng = GLA_GROUP
    gc = ng * c
    hd = GLA_HEADS * GLA_DK
    ngrp = t // gc
    scale = GLA_DK ** -0.5
    gi = lax.broadcasted_iota(jnp.int32, (gc, gc), 0)
    gj = lax.broadcasted_iota(jnp.int32, (gc, gc), 1)
    same_chunk = (gi // c) == (gj // c)
    same_head = (lax.broadcasted_iota(jnp.int32, (hd, hd), 0) // GLA_DK
                 == lax.broadcasted_iota(jnp.int32, (hd, hd), 1) // GLA_DK)
    row_i = lax.broadcasted_iota(jnp.int32, (c, hd), 0)
    col_j = lax.broadcasted_iota(jnp.int32, (c, hd), 1) % c
    ones4 = same_chunk.astype(BF16)
    tri4 = [(same_chunk & ((gj % c) <= (gi % c))).astype(BF16), (same_chunk & ((gj % c) >= (gi % c))).astype(BF16)]
    keep = [col_j <= row_i, col_j >= row_i]
    for d in range(2):
        x = _dot(gg_ref[0], w2_ref[d]) + bg_ref[d:d + 1, :]
        la_ref[d] = _log_sigmoid(x) * (1.0 / GLA_TAU)

    def interleave(stages):
        out = [None] * len(stages)
        live = list(range(len(stages)))
        while live:
            for i in list(live):
                try:
                    out[i] = next(stages[i])
                except StopIteration:
                    live.remove(i)
        return out

    def group_rows(sg):
        return pl.ds(sg * gc if isinstance(sg, int) else pl.multiple_of(sg * gc, gc), gc)

    def intra_stages(d, sg):
        rows4 = group_rows(sg)
        la = la_ref[d, rows4, :]
        hi = la.astype(BF16)
        lo = (la - hi.astype(F32)).astype(BF16)
        bc = _dot(tri4[d], hi) + _dot(tri4[d], lo)
        bl = _dot(ones4, hi) + _dot(ones4, lo)
        yield None
        qd = (q_ref[0, rows4, :].astype(F32) * scale * jnp.exp(bc)).astype(BF16)
        kf = k_ref[0, rows4, :].astype(F32)
        kd = (kf * jnp.exp(-bc)).astype(BF16)
        vv = v_ref[0, rows4, :]
        qd_ref[d, rows4, :] = qd
        ke_ref[d, rows4, :] = (kf * jnp.exp(bl - bc)).astype(BF16)
        el = jnp.exp(bl)
        for g in range(ng):
            el_ref[d, sg * ng + g] = el[g * c:g * c + 1, :]
        kbd = [jnp.where(same_head, jnp.concatenate([kd[g * c:(g + 1) * c]] * GLA_HEADS, axis=0), 0.0)
               for g in range(ng)]
        sc = [_dot_nt(qd[g * c:(g + 1) * c], kbd[g]) for g in range(ng)]
        yield None
        a = [jnp.where(keep[d], s, 0.0).astype(BF16) for s in sc]
        vbd = [jnp.where(same_head, jnp.concatenate([vv[g * c:(g + 1) * c]] * GLA_HEADS, axis=0), 0.0)
               for g in range(ng)]
        yield jnp.concatenate([_dot(a[g], vbd[g]) for g in range(ng)], axis=0)

    def intra(groups):
        outs = interleave([intra_stages(d, sg) for sg in groups for d in range(2)])
        for i, sg in enumerate(groups):
            acc_ref[group_rows(sg), :] = outs[2 * i] + outs[2 * i + 1]

    run = GLA_RUN if ngrp % GLA_RUN == 0 else 1

    def intra_run(j, carry):
        intra([run * j + i for i in range(run)])
        return carry

    lax.fori_loop(0, ngrp // run, intra_run, 0)
    st_ref[...] = jnp.zeros_like(st_ref)

    def inter_stages(d, sg):
        if d == 0:
            grp, visit = sg, list(range(ng))
        else:
            grp, visit = jnp.where(sg == 0, 0, ngrp - sg), list(range(ng - 1, -1, -1))
        rows = [pl.ds(pl.multiple_of((grp * ng + g) * c, c), c) for g in visit]
        ds = [_dot_tn(v_ref[0, r, :], ke_ref[d, r, :]) for r in rows]
        yield None
        st = st_ref[d]
        sprev = []
        for i, g in enumerate(visit):
            sprev.append(st.astype(BF16))
            st = st * el_ref[d, grp * ng + g] + jnp.where(same_head, ds[i], 0.0)
        st_ref[d] = st
        oi = [_dot_nt(qd_ref[d, r, :], sprev[i]) for i, r in enumerate(rows)]
        yield None
        for i, r in enumerate(rows):
            acc_ref[r, :] += oi[i]
        yield None

    def inter(sg, carry):
        interleave([inter_stages(0, sg), inter_stages(1, sg)])
        return carry

    lax.fori_loop(0, ngrp, inter, 0)

    acc = acc_ref[...]
    sq = acc * acc
    sq_hi = sq.astype(BF16)
    sq_lo = (sq - sq_hi.astype(F32)).astype(BF16)
    avg = jnp.where(same_head, 1.0 / GLA_DK, 0.0).astype(BF16)
    ms = _dot(sq_hi, avg) + _dot(sq_lo, avg)
    r = r_ref[0].astype(F32)
    o_ref[0] = (acc * lax.rsqrt(ms + EPS) * gn_ref[...] * (r * _sigmoid(r))).astype(o_ref.dtype)


def _gla_attention(p, w2cat, b_gate, g_norm):
    b, t, _ = p.shape
    hd = GLA_HEADS * GLA_DK
    cb = lambda c: (lambda i: (i, 0, c // 256))
    return pl.pallas_call(
        functools.partial(_gla_kernel, t=t),
        grid=(b,),
        in_specs=[
            pl.BlockSpec((1, t, 256), cb(COL_GQ)),
            pl.BlockSpec((1, t, 256), cb(COL_GK)),
            pl.BlockSpec((1, t, 256), cb(COL_GV)),
            pl.BlockSpec((1, t, 256), cb(COL_GR)),
            pl.BlockSpec((1, t, 128), lambda i: (i, 0, COL_GG // 128)),
            pl.BlockSpec((2, 128, 256), lambda i: (0, 0, 0)),
            pl.BlockSpec((2, 256), lambda i: (0, 0)),
            pl.BlockSpec((1, hd), lambda i: (0, 0)),
        ],
        out_specs=pl.BlockSpec((1, t, 256), lambda i: (i, 0, 0)),
        out_shape=jax.ShapeDtypeStruct((b, t, 256), BF16),
        scratch_shapes=[
            pltpu.VMEM((2, t, hd), F32),
            pltpu.VMEM((t, hd), F32),
            pltpu.VMEM((2, t, hd), BF16),
            pltpu.VMEM((2, t, hd), BF16),
            pltpu.VMEM((2, t // GLA_CHUNK, 1, hd), F32),
            pltpu.VMEM((2, hd, hd), F32),
        ],
        compiler_params=_params(("arbitrary",)),
        name="gla_attention",
    )(p, p, p, p, p, w2cat, b_gate, jnp.tile(g_norm.reshape(1, GLA_DK), (1, GLA_HEADS)))


def _rope_tables(seq):
    pos = np.arange(seq)
    half = HEAD_DIM // 2
    inv_freq = ROPE_BASE ** (-np.arange(0, half, 2, dtype=np.float32) / half)
    cos_l, sin_l = [], []
    for p_axis in (pos // GRID_W, pos % GRID_W):
        ang = p_axis.astype(np.float32)[:, None] * inv_freq[None, :]
        cos_l += [np.cos(ang), np.cos(ang)]
        sin_l += [-np.sin(ang), np.sin(ang)]
    cos = np.concatenate(cos_l, axis=-1).astype(np.float32)
    sin = np.concatenate(sin_l, axis=-1).astype(np.float32)
    return jnp.asarray(np.tile(cos, (1, SWA_HEADS))), jnp.asarray(np.tile(sin, (1, SWA_HEADS)))


def _rope(x, cos, sin):
    n = x.shape[-1]
    lane = lax.broadcasted_iota(jnp.int32, x.shape, 1)
    partner = jnp.where((lane % 32) < 16, pltpu.roll(x, n - 16, 1), pltpu.roll(x, 16, 1))
    return x * cos + partner * sin


def _swa_kernel(q_ref, k_ref, v_ref, cos_ref, sin_ref, sink_ref, o_ref, qr_ref, kr_ref, *, t):
    seq = t - CTX_LEN
    nb = seq // SWA_BLOCK
    grp = SWA_HEADS // SWA_KV_HEADS
    kw = 3 * SWA_BLOCK
    scale = HEAD_DIM ** -0.5
    kv_w = SWA_KV_HEADS * HEAD_DIM
    lane = lax.broadcasted_iota(jnp.int32, (1, kv_w), 1)

    def to_kv_lanes(x, h):
        slab = x[:, (h // grp) * kv_w:(h // grp + 1) * kv_w]
        if h % grp != h // grp:
            slab = pltpu.roll(slab, HEAD_DIM, 1)
        return jnp.where(lane // HEAD_DIM == h // grp, slab, 0.0)

    def from_kv_lanes(o4, n):
        slabs = []
        for pair in range(SWA_HEADS // grp):
            lo = o4[(2 * pair) * n:(2 * pair + 1) * n]
            hi = o4[(2 * pair + 1) * n:(2 * pair + 2) * n]
            if pair == 0:
                hi = pltpu.roll(hi, HEAD_DIM, 1)
            else:
                lo = pltpu.roll(lo, HEAD_DIM, 1)
            slabs.append(jnp.where(lane < HEAD_DIM, lo, hi))
        return jnp.concatenate(slabs, axis=1)

    def sink_rows(n):
        row = lax.broadcasted_iota(jnp.int32, (SWA_HEADS * n, 1), 0) // n
        sk = jnp.full((SWA_HEADS * n, 1), sink_ref[0], F32)
        for h in range(1, SWA_HEADS):
            sk = jnp.where(row == h, sink_ref[h], sk)
        return sk

    cos = cos_ref[...]
    sin = sin_ref[...]
    qrot = _rope(q_ref[0, CTX_LEN:, :].astype(F32), cos, sin) * scale
    for h in range(SWA_HEADS):
        qr_ref[h] = to_kv_lanes(qrot, h).astype(BF16)
    kr_ref[...] = _rope(k_ref[0, CTX_LEN:, :].astype(F32), cos[:, :kv_w], sin[:, :kv_w]).astype(BF16)
    kc = k_ref[0, 0:CTX_LEN, :]
    vc = v_ref[0, 0:CTX_LEN, :]
    qc = q_ref[0, 0:CTX_LEN, :].astype(F32) * scale
    q4 = jnp.concatenate([to_kv_lanes(qc, h) for h in range(SWA_HEADS)], axis=0).astype(BF16)
    s = _dot_nt(q4, kc)
    sk = sink_rows(CTX_LEN)
    mx = jnp.maximum(jnp.max(s, axis=-1, keepdims=True), sk)
    p = jnp.exp(s - mx)
    den = jnp.sum(p, axis=-1, keepdims=True) + jnp.exp(sk - mx)
    o_ref[0, 0:CTX_LEN, :] = from_kv_lanes(_dot(p.astype(BF16), vc) / den, CTX_LEN).astype(o_ref.dtype)

    skb = sink_rows(SWA_BLOCK)
    row_b = lax.broadcasted_iota(jnp.int32, (SWA_HEADS * SWA_BLOCK, 1), 0) % SWA_BLOCK
    col_b = lax.broadcasted_iota(jnp.int32, (1, kw), 1)

    def logits(n):
        ws = jnp.clip(n - 1, 0, nb - 3) * SWA_BLOCK
        qrows = pl.ds(pl.multiple_of(n * SWA_BLOCK, SWA_BLOCK), SWA_BLOCK)
        krows = pl.ds(pl.multiple_of(ws, SWA_BLOCK), kw)
        qb = jnp.concatenate([qr_ref[h, qrows, :] for h in range(SWA_HEADS)], axis=0)
        return _dot_nt(qb, kr_ref[krows, :]), _dot_nt(qb, kc), n, ws

    def attend(s_b, s_c, n, ws):
        valid = jnp.abs(n * SWA_BLOCK + row_b - (ws + col_b)) <= SWA_WINDOW
        s_b = jnp.where(valid, s_b, NEG_INF)
        mxb = jnp.maximum(jnp.maximum(jnp.max(s_b, axis=-1, keepdims=True),
                                      jnp.max(s_c, axis=-1, keepdims=True)), skb)
        p_b = jnp.exp(s_b - mxb)
        p_c = jnp.exp(s_c - mxb)
        denb = (jnp.sum(p_b, axis=-1, keepdims=True) + jnp.sum(p_c, axis=-1, keepdims=True)
                + jnp.exp(skb - mxb))
        vrows = pl.ds(pl.multiple_of(CTX_LEN + ws, SWA_BLOCK), kw)
        o4 = (_dot(p_b.astype(BF16), v_ref[0, vrows, :]) + _dot(p_c.astype(BF16), vc)) / denb
        orows = pl.ds(pl.multiple_of(CTX_LEN + n * SWA_BLOCK, SWA_BLOCK), SWA_BLOCK)
        o_ref[0, orows, :] = from_kv_lanes(o4, SWA_BLOCK).astype(o_ref.dtype)

    def block_run(j, carry):
        ahead = logits(SWA_RUN * j)
        for i in range(1, SWA_RUN):
            cur, ahead = ahead, logits(SWA_RUN * j + i)
            attend(*cur)
        attend(*ahead)
        return carry

    lax.fori_loop(0, nb // SWA_RUN, block_run, 0)


def _swa_attention(p, cos, sin, sink):
    b, t, _ = p.shape
    seq = t - CTX_LEN
    return pl.pallas_call(
        functools.partial(_swa_kernel, t=t),
        grid=(b,),
        in_specs=[
            pl.BlockSpec((1, t, 256), lambda i: (i, 0, COL_SQ // 256)),
            pl.BlockSpec((1, t, 128), lambda i: (i, 0, COL_SK // 128)),
            pl.BlockSpec((1, t, 128), lambda i: (i, 0, COL_SV // 128)),
            pl.BlockSpec((seq, 256), lambda i: (0, 0)),
            pl.BlockSpec((seq, 256), lambda i: (0, 0)),
            pl.BlockSpec(memory_space=pltpu.SMEM),
        ],
        out_specs=pl.BlockSpec((1, t, 256), lambda i: (i, 0, 0)),
        out_shape=jax.ShapeDtypeStruct((b, t, 256), BF16),
        scratch_shapes=[pltpu.VMEM((SWA_HEADS, seq, 128), BF16), pltpu.VMEM((seq, 128), BF16)],
        compiler_params=_params(("arbitrary",)),
        name="swa_attention",
    )(p, p, p, cos, sin, sink)


def _out_proj_kernel(u_ref, yg_ref, d_ref, wg_ref, na_ref, gla_ref, swa_ref, wo_ref, g_ref,
                     modl_ref, modc_ref, x_ref, o_ref, *, tm, first):
    ys = jnp.concatenate(
        [jnp.concatenate([yg_ref[g, j, 0] for g in range(S5_GROUPS)], axis=0).T for j in range(tm // S5_TC)], axis=0)
    y = d_ref[...] * u_ref[0].astype(F32) + ys
    z = _gelu(y)
    a = z * _sigmoid(_dot(z.astype(BF16), wg_ref[...]))
    acc = _dot(jnp.concatenate([a.astype(BF16), na_ref[0], gla_ref[0], swa_ref[0]], axis=1), wo_ref[...])
    row = (pl.program_id(1) + first) * tm + lax.broadcasted_iota(jnp.int32, (tm, 1), 0)
    gate = jnp.where(row < CTX_LEN, modc_ref[0, 2:3, :], modl_ref[0, 2:3, :])
    o_ref[0] = x_ref[0] + gate * (_rms(acc) * g_ref[...])


def _out_proj(p, yg, s5_d, w_glu, y_na, y_gla, y_swa, w_out, g_post, modl, modc, x, *, tm, skip_ctx=False):
    b, t_in, d = x.shape
    first = CTX_LEN // tm if skip_ctx else 0
    assert tm % S5_TC == 0 and (not skip_ctx or CTX_LEN % tm == 0)
    t = t_in - first * tm
    nc = tm // S5_TC
    tok = lambda w: pl.BlockSpec((1, tm, w), lambda i, j: (i, j + first, 0))
    full = lambda a: pl.BlockSpec(a.shape, lambda i, j: (0,) * a.ndim)
    chunked = pl.BlockSpec((S5_GROUPS, nc, 1, S5_GROUP_CH, S5_TC), lambda i, j: (0, j + first, i, 0, 0))
    return pl.pallas_call(
        functools.partial(_out_proj_kernel, tm=tm, first=first),
        grid=(b, t // tm),
        in_specs=[
            tok(256), chunked, full(s5_d), full(w_glu), tok(256), tok(256), tok(256),
            full(w_out), full(g_post),
            pl.BlockSpec((1, 6, d), lambda i, j: (i, 0, 0)),
            pl.BlockSpec((1, 6, d), lambda i, j: (0, 0, 0)),
            tok(d),
        ],
        out_specs=pl.BlockSpec((1, tm, d), lambda i, j: (i, j, 0)),
        out_shape=jax.ShapeDtypeStruct((b, t, d), F32),
        compiler_params=_params(("arbitrary", "arbitrary")),
        name="out_proj",
    )(p, yg, s5_d, w_glu, y_na, y_gla, y_swa, w_out, g_post, modl, modc, x)


def _ffn_down_kernel(gt_ref, vl_ref, prev_ref, next_ref, cw_ref, wd_ref, g_ref, modl_ref, modc_ref, x_ref, o_ref,
                     a_ref, *, tm, t, ctx_rows):
    h = FFN_HALO
    row0 = pl.program_id(1) * tm
    w0, w1, w2 = cw_ref[0:1, :], cw_ref[1:2, :], cw_ref[2:3, :]
    g = gt_ref[0].astype(F32)
    gp = pltpu.roll(g, 1, 0)
    gn = pltpu.roll(g, tm - 1, 0)
    w0b, w1b, w2b = w0.astype(BF16), w1.astype(BF16), w2.astype(BF16)
    a_ref[...] = _gelu(gp.astype(BF16) * w0b + gt_ref[0] * w1b + gn.astype(BF16) * w2b) * vl_ref[0]
    loc = lax.broadcasted_iota(jnp.int32, (h, 1), 0)
    has_prev = jnp.where((row0 == 0) | (row0 == ctx_rows), 0.0, 1.0)
    has_next = jnp.where((row0 + tm == ctx_rows) | (row0 + tm == t), 0.0, 1.0)
    gp_top = jnp.where(loc == 0, prev_ref[0, h - 1:h, :].astype(F32) * has_prev, gp[0:h])
    top = gp_top * w0 + g[0:h] * w1 + gn[0:h] * w2
    a_ref[0:h, :] = _gelu(top.astype(BF16)) * vl_ref[0, 0:h, :]
    gn_bot = jnp.where(loc == h - 1, next_ref[0, 0:1, :].astype(F32) * has_next, gn[tm - h:tm])
    bot = gp[tm - h:tm] * w0 + g[tm - h:tm] * w1 + gn_bot * w2
    a_ref[tm - h:tm, :] = _gelu(bot.astype(BF16)) * vl_ref[0, tm - h:tm, :]
    acc = _dot(a_ref[...], wd_ref[...])
    gate = jnp.where(row0 < ctx_rows, modc_ref[0, 5:6, :], modl_ref[0, 5:6, :])
    o_ref[0] = x_ref[0] + gate * (_rms(acc) * g_ref[...])


def _ffn_down(gv, conv_w, w_down, g_post, modl, modc, x, *, tm, ctx_rows=CTX_LEN):
    b, t, d = x.shape
    assert ctx_rows % tm == 0 and t % tm == 0 and tm > 2 * FFN_HALO
    nbh = tm // FFN_HALO
    lasth = t // FFN_HALO - 1
    return pl.pallas_call(
        functools.partial(_ffn_down_kernel, tm=tm, t=t, ctx_rows=ctx_rows),
        grid=(b, t // tm),
        in_specs=[
            pl.BlockSpec((1, tm, D_FF), lambda i, j: (i, j, 0)),
            pl.BlockSpec((1, tm, D_FF), lambda i, j: (i, j, 1)),
            pl.BlockSpec((1, FFN_HALO, D_FF), lambda i, j: (i, jnp.maximum(j * nbh - 1, 0), 0)),
            pl.BlockSpec((1, FFN_HALO, D_FF), lambda i, j: (i, jnp.minimum((j + 1) * nbh, lasth), 0)),
            pl.BlockSpec((3, D_FF), lambda i, j: (0, 0)),
            pl.BlockSpec((D_FF, d), lambda i, j: (0, 0)),
            pl.BlockSpec((1, d), lambda i, j: (0, 0)),
            pl.BlockSpec((1, 6, d), lambda i, j: (i, 0, 0)),
            pl.BlockSpec((1, 6, d), lambda i, j: (0, 0, 0)),
            pl.BlockSpec((1, tm, d), lambda i, j: (i, j, 0)),
        ],
        out_specs=pl.BlockSpec((1, tm, d), lambda i, j: (i, j, 0)),
        out_shape=jax.ShapeDtypeStruct((b, t, d), F32),
        scratch_shapes=[pltpu.VMEM((tm, D_FF), BF16)],
        compiler_params=_params(("arbitrary", "arbitrary")),
        name="ffn_down",
    )(gv, gv, gv, gv, conv_w, w_down, g_post, modl, modc, x)


def _reorder_w_in(w_in):
    sizes = (256, 256, 256, 256, 256, 256, 256, 16, 16, 256, 256, 128, 128)
    offs = np.concatenate([[0], np.cumsum(sizes)])
    seg = lambda i: w_in[..., offs[i]:offs[i + 1]]
    pad = jnp.zeros(w_in.shape[:-1] + (P_WIDTH - COL_GG - 2 * GLA_RANK,), w_in.dtype)
    return jnp.concatenate([seg(0), seg(1), seg(2), seg(3), seg(4), seg(5), seg(6), seg(9),
                            seg(10), seg(11), seg(12), seg(7), seg(8), pad], axis=-1)


def kernel(x, c, ctx, c_ctx, w_mod, b_mod, g_pre_mix, g_post_mix, g_pre_ffn, g_post_ffn, w_in, w_out, s5_lam_re, s5_lam_im, s5_b_re, s5_b_im, s5_c_re, s5_c_im, s5_log_step, s5_d, s5_w_glu, na_rpb, gla_w_gate2, gla_b_gate, gla_g_norm, swa_sink, ffn_w_up, ffn_conv, ffn_w_down):
    bsz, seq, d = x.shape
    depth = w_mod.shape[0]
    t = CTX_LEN + seq
    tm = 768

    cvec = jnp.concatenate([c, c_ctx[None, :]], axis=0)
    cvec = jnp.pad(cvec, ((0, (-cvec.shape[0]) % 8), (0, 0)))
    mod = _modulation(cvec, w_mod, b_mod).reshape(depth, cvec.shape[0], 6, d)

    w_in_r = _reorder_w_in(w_in.astype(BF16))
    w_out_b = w_out.astype(BF16)
    w_glu_b = s5_w_glu.astype(BF16)
    w_up_b = ffn_w_up.astype(BF16)
    w_down_b = ffn_w_down.astype(BF16)
    w2 = jnp.stack([jnp.pad(gla_w_gate2[:, 0], ((0, 0), (0, 128 - GLA_RANK), (0, 0))),
                    jnp.pad(gla_w_gate2[:, 1], ((0, 0), (GLA_RANK, 128 - 2 * GLA_RANK), (0, 0)))], axis=1).astype(BF16)
    cos, sin = _rope_tables(seq)

    xc = jnp.concatenate([ctx, x], axis=1)
    for l in range(depth):
        modl = mod[l, :bsz]
        modc = mod[l, bsz:bsz + 1]
        p, ug = _nm_matmul(xc, g_pre_mix[l][None], modl, modc, w_in_r[l], k_shift=0, k_scale=1, tm=tm, tn=P_WIDTH,
                           s5_chunks=True)
        tabs = _s5_prep(s5_lam_re[l], s5_lam_im[l], s5_b_re[l], s5_b_im[l], s5_c_re[l], s5_c_im[l], s5_log_step[l])
        ys = _s5_scan(ug, *tabs)
        y_na = _na_attention(p, na_rpb[l])
        y_gla = _gla_attention(p, w2[l], gla_b_gate[l], gla_g_norm[l][None])
        y_swa = _swa_attention(p, cos, sin, swa_sink[l])
        last = l == depth - 1
        ctx_rows = 0 if last else CTX_LEN
        xc = _out_proj(p, ys, s5_d[l][None], w_glu_b[l], y_na, y_gla, y_swa, w_out_b[l],
                       g_post_mix[l][None], modl, modc, xc, tm=CTX_LEN if last else tm, skip_ctx=last)
        gv = _nm_matmul(xc, g_pre_ffn[l][None], modl, modc, w_up_b[l], k_shift=3, k_scale=4,
                        tm=512 if last else tm, tn=2 * D_FF, ctx_rows=ctx_rows)
        xc = _ffn_down(gv, ffn_conv[l], w_down_b[l], g_post_ffn[l][None], modl, modc, xc, tm=256, ctx_rows=ctx_rows)
    return xc
```
